```python
import math
import jax
import jax.numpy as jnp
from jax import lax
import numpy as np

D_MODEL = 2048
BATCH = 32
SEQ = 256
DEPTH = 1
DEC_BATCH = 2
DEC_SEQ = 1024
PAST_LEN = 512

GRID_W = 64
NA_HEADS = 16
NA_HEAD_DIM = 128
NA_WIDTH = NA_HEADS * NA_HEAD_DIM
NA_WIN_R = 8
NA_WIN_C = 16
DENSE_KEYS_LIMIT = 2048
Q_BLOCK = 128
SSD_HEADS = 32
SSD_HEAD_DIM = 64
SSD_WIDTH = SSD_HEADS * SSD_HEAD_DIM
SSD_GROUPS = 4
SSD_STATE = 128
SSD_CONV = 5
SSD_CHUNK = 128
SSD_CONV_CH = SSD_WIDTH + 2 * SSD_GROUPS * SSD_STATE
N_EXPERTS = 32
TOP_K = 4
D_EXPERT = 2048
SWIGLU_LIMIT = 7.0
SWIGLU_ALPHA = 1.702
DN_ALPHA = (2.0 * DEPTH) ** 0.25
DN_BETA = (8.0 * DEPTH) ** -0.25
LN_EPS = 1e-5
IN_SPLITS = (NA_WIDTH, NA_WIDTH, NA_WIDTH, SSD_WIDTH, SSD_CONV_CH, 2 * SSD_HEADS, D_MODEL, D_MODEL)
IN_COLS = 3 * NA_WIDTH + SSD_WIDTH + SSD_CONV_CH + 2 * SSD_HEADS + 2 * D_MODEL

kernel_name = 'hybrid_ssd_natten_moe_prefix_dit_step'

F32 = jnp.float32


def _split_cols(t, sizes):
    idx, acc = [], 0
    for s in sizes[:-1]:
        acc += s
        idx.append(acc)
    return jnp.split(t, idx, axis=-1)


def _layer_norm(x, g=None, b=None):
    xf = x.astype(F32)
    mu = jnp.mean(xf, axis=-1, keepdims=True)
    var = jnp.mean(jnp.square(xf - mu), axis=-1, keepdims=True)
    y = (xf - mu) * lax.rsqrt(var + LN_EPS)
    if g is not None:
        y = y * g.astype(F32) + b.astype(F32)
    return y.astype(x.dtype)


def _rms_norm(x, g):
    xf = x.astype(F32)
    y = xf * lax.rsqrt(jnp.mean(jnp.square(xf), axis=-1, keepdims=True) + LN_EPS)
    return (y * g.astype(F32)).astype(x.dtype)


def _modulation(cvec, w_mod, b_mod):
    mods = jax.nn.silu(cvec) @ w_mod + b_mod
    parts = jnp.split(mods, 6, axis=-1)
    if cvec.ndim == 2:
        parts = [p[:, None, :] for p in parts]
    return parts


def _modulate(x, shift, scale):
    return _layer_norm(x) * (1.0 + scale) + shift


def _project_mixer_inputs(h, w_in):
    b, L, _ = h.shape
    q, k, v, z, xbc, dt_raw, g_ssd, g_na = _split_cols(h @ w_in, IN_SPLITS)
    heads = lambda t: t.reshape(b, L, NA_HEADS, NA_HEAD_DIM)
    return heads(q), heads(k), heads(v), z, xbc, dt_raw, g_ssd, g_na


def _dwconv_centred(u, w, bias):
    ch, kw = u.shape[-1], w.shape[0]
    y = lax.conv_general_dilated(u, w[:, None, :].astype(u.dtype), window_strides=(1,),
                                 padding=[(kw // 2, kw // 2)],
                                 dimension_numbers=('NWC', 'WIO', 'NWC'),
                                 feature_group_count=ch)
    return y + bias


def _segsum_exp(cs):
    t = cs.shape[-1]
    diff = cs[..., :, None] - cs[..., None, :]
    mask = jnp.tril(jnp.ones((t, t), dtype=bool))
    return jnp.exp(jnp.where(mask, diff, -jnp.inf))


def _ssd_scan(x, dt, a, bm, cm, h0):
    b, L, H, P = x.shape
    G, N = bm.shape[2], bm.shape[3]
    R, Q = H // G, SSD_CHUNK
    nc = L // Q
    xdt = (x.astype(F32) * dt[..., None]).reshape(b, nc, Q, G, R, P)
    a_cs = jnp.cumsum((dt * a).reshape(b, nc, Q, G, R).transpose(0, 3, 4, 1, 2), axis=-1)
    bc = bm.astype(F32).reshape(b, nc, Q, G, N)
    cc = cm.astype(F32).reshape(b, nc, Q, G, N)
    l_intra = _segsum_exp(a_cs)
    cb = jnp.einsum('bcign,bcjgn->bgcij', cc, bc)
    y_diag = jnp.einsum('bgcij,bgrcij,bcjgrp->bcigrp', cb, l_intra, xdt)
    decay_to_end = jnp.exp(a_cs[..., -1:] - a_cs)
    chunk_states = jnp.einsum('bcjgn,bgrcj,bcjgrp->bcgrpn', bc, decay_to_end, xdt)
    h_init = h0.astype(F32).reshape(b, 1, G, R, P, N)
    chunk_states = jnp.concatenate([h_init, chunk_states], axis=1)
    chunk_cs = jnp.cumsum(jnp.pad(a_cs[..., -1], ((0, 0), (0, 0), (0, 0), (1, 0))), axis=-1)
    l_inter = _segsum_exp(chunk_cs)
    states = jnp.einsum('bgrzc,bcgrpn->bzgrpn', l_inter, chunk_states)
    y_off = jnp.einsum('bcign,bcgrpn,bgrci->bcigrp', cc, states[:, :-1], jnp.exp(a_cs))
    y = (y_diag + y_off).reshape(b, L, H, P)
    return y.astype(x.dtype), states[:, -1].reshape(b, H, P, N).astype(x.dtype)


def _ssd_branch(z, xbc, dt_raw, conv_w, conv_b, dt_bias, a_log, d_skip, norm_w, h0_fwd, h0_bwd):
    b, L, _ = z.shape
    xbc = jax.nn.silu(_dwconv_centred(xbc, conv_w, conv_b))
    xs, bm, cm = _split_cols(xbc, (SSD_WIDTH, SSD_GROUPS * SSD_STATE, SSD_GROUPS * SSD_STATE))
    xs = xs.reshape(b, L, SSD_HEADS, SSD_HEAD_DIM)
    bm = bm.reshape(b, L, SSD_GROUPS, SSD_STATE)
    cm = cm.reshape(b, L, SSD_GROUPS, SSD_STATE)
    dt = jax.nn.softplus(dt_raw.astype(F32).reshape(b, L, 2, SSD_HEADS) + dt_bias.astype(F32))
    a = -jnp.exp(a_log.astype(F32))
    flip = lambda t: jnp.flip(t, axis=1)
    y_f, h_f = _ssd_scan(xs, dt[:, :, 0], a[0], bm, cm, h0_fwd)
    y_b, h_b = _ssd_scan(flip(xs), flip(dt[:, :, 1]), a[1], flip(bm), flip(cm), h0_bwd)
    y = y_f + flip(y_b) + d_skip[:, None] * xs
    y = _rms_norm(y.reshape(b, L, SSD_WIDTH) * jax.nn.silu(z), norm_w)
    return y, h_f, h_b


def _softmax_attend(q, k, v):
    s = jnp.einsum('bqhd,bkhd->bhqk', q, k).astype(F32) * (NA_HEAD_DIM ** -0.5)
    p = jax.nn.softmax(s, axis=-1).astype(v.dtype)
    return jnp.einsum('bhqk,bkhd->bqhd', p, v)


def _context_attention(q, k, v):
    b, S, h, d = q.shape
    if k.shape[1] < DENSE_KEYS_LIMIT:
        return _softmax_attend(q, k, v)
    qb = jnp.moveaxis(q.reshape(b, S // Q_BLOCK, Q_BLOCK, h, d), 1, 0)
    ob = lax.map(lambda blk: _softmax_attend(blk, k, v), qb)
    return jnp.moveaxis(ob, 0, 1).reshape(b, S, h, d)


def _neighbourhood_attention(q, k, v, k_ctx, v_ctx, rpb):
    b, L, h, d = q.shape
    rows = L // GRID_W
    kr = min(NA_WIN_R, rows)
    scale = NA_HEAD_DIM ** -0.5
    qg = q.reshape(b, rows, GRID_W, h, d)
    kg = k.reshape(b, rows, GRID_W, h, d)
    vg = v.reshape(b, rows, GRID_W, h, d)
    col = jnp.arange(GRID_W)
    c0 = jnp.clip(col - NA_WIN_C // 2, 0, GRID_W - NA_WIN_C)
    col_mask = (col[None, :] >= c0[:, None]) & (col[None, :] < c0[:, None] + NA_WIN_C)
    dc_idx = jnp.clip(col[None, :] - col[:, None] + NA_WIN_C - 1, 0, 2 * NA_WIN_C - 2)

    def row_block(r):
        r0 = jnp.clip(r - kr // 2, 0, rows - kr)
        q_r = lax.dynamic_index_in_dim(qg, r, axis=1, keepdims=False)
        k_band = lax.dynamic_slice_in_dim(kg, r0, kr, axis=1)
        v_band = lax.dynamic_slice_in_dim(vg, r0, kr, axis=1)
        dr_idx = r0 + jnp.arange(kr) - r + NA_WIN_R - 1
        bias = rpb[:, dr_idx[None, :, None], dc_idx[:, None, :]].astype(F32)
        s_loc = jnp.einsum('bqhd,bkchd->bhqkc', q_r, k_band).astype(F32) * scale + bias
        s_loc = jnp.where(col_mask[:, None, :], s_loc, -jnp.inf).reshape(b, h, GRID_W, kr * GRID_W)
        s_ctx = jnp.einsum('bqhd,bkhd->bhqk', q_r, k_ctx).astype(F32) * scale
        p = jax.nn.softmax(jnp.concatenate([s_loc, s_ctx], axis=-1), axis=-1).astype(v.dtype)
        p_loc = p[..., :kr * GRID_W].reshape(b, h, GRID_W, kr, GRID_W)
        p_ctx = p[..., kr * GRID_W:]
        o = jnp.einsum('bhqkc,bkchd->bqhd', p_loc, v_band) + jnp.einsum('bhqk,bkhd->bqhd', p_ctx, v_ctx)
        return o.astype(q.dtype)

    out = lax.map(row_block, jnp.arange(rows))
    return jnp.moveaxis(out, 0, 1).reshape(b, L, h, d)


def _merge_branches(y_na, y_ssd, g_na, g_ssd, w_na_out, w_ssd_out, w_o):
    b, L = y_na.shape[:2]
    y = (jax.nn.sigmoid(g_na) * (y_na.reshape(b, L, NA_WIDTH) @ w_na_out)
         + jax.nn.sigmoid(g_ssd) * (y_ssd @ w_ssd_out))
    return y @ w_o


def _moe(x, w_router, b_router, w_gate_up, b_gate_up, w_down, b_down):
    b, L, d = x.shape
    xf = x.reshape(b * L, d)
    logits = xf.astype(F32) @ w_router.astype(F32) + b_router.astype(F32)
    top_vals, top_idx = lax.top_k(logits, TOP_K)
    top_w = jax.nn.softmax(top_vals, axis=-1)
    combine = jnp.einsum('tk,tke->te', top_w, jax.nn.one_hot(top_idx, N_EXPERTS, dtype=F32)).astype(x.dtype)
    out = jnp.zeros_like(xf)
    for e in range(N_EXPERTS):
        gu = xf @ w_gate_up[e] + b_gate_up[e]
        gate = jnp.minimum(gu[:, :D_EXPERT], SWIGLU_LIMIT)
        up = jnp.clip(gu[:, D_EXPERT:], -SWIGLU_LIMIT, SWIGLU_LIMIT)
        hid = (up + 1.0) * gate * jax.nn.sigmoid(SWIGLU_ALPHA * gate)
        out = out + combine[:, e:e + 1] * (hid @ w_down[e] + b_down[e])
    return out.reshape(b, L, d)


def setup_inputs(seed: int = 0) -> dict:
    key = jax.random.key(seed)
    ks = jax.random.split(key, 32)
    d = D_MODEL

    def nrm(k, shape, scale):
        return jax.random.normal(k, shape, F32) * scale

    dt0 = jnp.exp(jax.random.uniform(ks[10], (DEPTH, 2, SSD_HEADS), F32, math.log(1e-3), math.log(1e-1)))
    return {
        'x_prompt': nrm(ks[0], (BATCH, SEQ, d), 1.0),
        'x_sample': nrm(ks[1], (DEC_BATCH, DEC_SEQ, d), 1.0),
        'cache_na_k': nrm(ks[2], (DEC_BATCH, DEPTH, PAST_LEN, NA_HEADS, NA_HEAD_DIM), 1.0),
        'cache_na_v': nrm(ks[3], (DEC_BATCH, DEPTH, PAST_LEN, NA_HEADS, NA_HEAD_DIM), 1.0),
        'state_ssd_fwd': nrm(ks[4], (DEC_BATCH, DEPTH, SSD_HEADS, SSD_HEAD_DIM, SSD_STATE), 0.1),
        'state_ssd_bwd': nrm(ks[5], (DEC_BATCH, DEPTH, SSD_HEADS, SSD_HEAD_DIM, SSD_STATE), 0.1),
        'c': nrm(ks[6], (DEC_BATCH, d), 1.0),
        'c_ctx': nrm(ks[7], (d,), 1.0),
        'w_mod': nrm(ks[8], (DEPTH, d, 6 * d), 0.5 * d ** -0.5),
        'b_mod': nrm(ks[9], (DEPTH, 6 * d), 0.02),
        'w_in': nrm(ks[11], (DEPTH, d, IN_COLS), d ** -0.5),
        'ssd_conv_w': nrm(ks[12], (DEPTH, SSD_CONV, SSD_CONV_CH), SSD_CONV ** -0.5),
        'ssd_conv_b': nrm(ks[13], (DEPTH, SSD_CONV_CH), 0.02),
        'ssd_dt_bias': dt0 + jnp.log(-jnp.expm1(-dt0)),
        'ssd_a_log': jnp.log(jax.random.uniform(ks[14], (DEPTH, 2, SSD_HEADS), F32, 1.0, 16.0)),
        'ssd_d': 1.0 + nrm(ks[15], (DEPTH, SSD_HEADS), 0.1),
        'ssd_norm_w': 1.0 + nrm(ks[16], (DEPTH, SSD_WIDTH), 0.1),
        'na_rpb': nrm(ks[17], (DEPTH, NA_HEADS, 2 * NA_WIN_R - 1, 2 * NA_WIN_C - 1), 0.1),
        'w_ssd_out': nrm(ks[18], (DEPTH, SSD_WIDTH, d), SSD_WIDTH ** -0.5),
        'w_na_out': nrm(ks[19], (DEPTH, NA_WIDTH, d), NA_WIDTH ** -0.5),
        'w_o': nrm(ks[20], (DEPTH, d, d), DN_BETA * d ** -0.5),
        'ln1_g': 1.0 + nrm(ks[21], (DEPTH, d), 0.1),
        'ln1_b': nrm(ks[22], (DEPTH, d), 0.02),
        'ln2_g': 1.0 + nrm(ks[23], (DEPTH, d), 0.1),
        'ln2_b': nrm(ks[24], (DEPTH, d), 0.02),
        'w_router': nrm(ks[25], (DEPTH, d, N_EXPERTS), d ** -0.5),
        'b_router': nrm(ks[26], (DEPTH, N_EXPERTS), 0.01),
        'w_gate_up': nrm(ks[27], (DEPTH, N_EXPERTS, d, 2 * D_EXPERT), d ** -0.5),
        'b_gate_up': nrm(ks[28], (DEPTH, N_EXPERTS, 2 * D_EXPERT), 0.02),
        'w_down': nrm(ks[29], (DEPTH, N_EXPERTS, D_EXPERT, d), DN_BETA * D_EXPERT ** -0.5),
        'b_down': nrm(ks[30], (DEPTH, N_EXPERTS, d), 0.02),
    }


def reference(x_prompt, x_sample, cache_na_k, cache_na_v, state_ssd_fwd, state_ssd_bwd, c, c_ctx,
              w_mod, b_mod, w_in, ssd_conv_w, ssd_conv_b, ssd_dt_bias, ssd_a_log, ssd_d, ssd_norm_w,
              na_rpb, w_ssd_out, w_na_out, w_o, ln1_g, ln1_b, ln2_g, ln2_b,
              w_router, b_router, w_gate_up, b_gate_up, w_down, b_down):
    xp, xs = x_prompt, x_sample
    new_k, new_v, new_hf, new_hb = [], [], [], []
    for l in range(DEPTH):
        m = _modulation(c_ctx, w_mod[l], b_mod[l])
        h = _modulate(xp, m[0], m[1])
        q, k, v, z, xbc, dt_raw, g_ssd, g_na = _project_mixer_inputs(h, w_in[l])
        y_na = _context_attention(q, k, v)
        bp = xp.shape[0]
        zero_state = jnp.zeros((bp, SSD_HEADS, SSD_HEAD_DIM, SSD_STATE), xp.dtype)
        y_ssd, h_f, h_b = _ssd_branch(z, xbc, dt_raw, ssd_conv_w[l], ssd_conv_b[l], ssd_dt_bias[l],
                                      ssd_a_log[l], ssd_d[l], ssd_norm_w[l], zero_state, zero_state)
        mix = _merge_branches(y_na, y_ssd, g_na, g_ssd, w_na_out[l], w_ssd_out[l], w_o[l])
        xp = _layer_norm(DN_ALPHA * xp + m[2] * mix, ln1_g[l], ln1_b[l])
        ffn = _moe(_modulate(xp, m[3], m[4]), w_router[l], b_router[l], w_gate_up[l], b_gate_up[l],
                   w_down[l], b_down[l])
        xp = _layer_norm(DN_ALPHA * xp + m[5] * ffn, ln2_g[l], ln2_b[l])
        new_k.append(k)
        new_v.append(v)
        new_hf.append(h_f)
        new_hb.append(h_b)

        m = _modulation(c, w_mod[l], b_mod[l])
        h = _modulate(xs, m[0], m[1])
        q, k, v, z, xbc, dt_raw, g_ssd, g_na = _project_mixer_inputs(h, w_in[l])
        y_na = _neighbourhood_attention(q, k, v, cache_na_k[:, l], cache_na_v[:, l], na_rpb[l])
        y_ssd, _, _ = _ssd_branch(z, xbc, dt_raw, ssd_conv_w[l], ssd_conv_b[l], ssd_dt_bias[l],
                                  ssd_a_log[l], ssd_d[l], ssd_norm_w[l],
                                  state_ssd_fwd[:, l], state_ssd_bwd[:, l])
        mix = _merge_branches(y_na, y_ssd, g_na, g_ssd, w_na_out[l], w_ssd_out[l], w_o[l])
        xs = _layer_norm(DN_ALPHA * xs + m[2] * mix, ln1_g[l], ln1_b[l])
        ffn = _moe(_modulate(xs, m[3], m[4]), w_router[l], b_router[l], w_gate_up[l], b_gate_up[l],
                   w_down[l], b_down[l])
        xs = _layer_norm(DN_ALPHA * xs + m[5] * ffn, ln2_g[l], ln2_b[l])

    new_na_k = jnp.stack(new_k, axis=1)
    new_na_v = jnp.stack(new_v, axis=1)
    new_ssd_fwd = jnp.stack(new_hf, axis=1)
    new_ssd_bwd = jnp.stack(new_hb, axis=1)
    return (xp, xs, new_na_k, new_na_v, new_ssd_fwd, new_ssd_bwd)
```

```python
import functools

import jax
import jax.numpy as jnp
from jax import lax
from jax.experimental import pallas as pl
from jax.experimental.pallas import tpu as pltpu

F32 = jnp.float32
BF16 = jnp.bfloat16

D_MODEL = 2048
BATCH = 32
SEQ = 256
DEC_BATCH = 2
DEC_SEQ = 1024
PAST_LEN = 512
GRID_W = 64
NA_HEADS = 16
NA_HEAD_DIM = 128
NA_WIDTH = NA_HEADS * NA_HEAD_DIM
NA_WIN_R = 8
NA_WIN_C = 16
SSD_HEADS = 32
SSD_HEAD_DIM = 64
SSD_WIDTH = SSD_HEADS * SSD_HEAD_DIM
SSD_GROUPS = 4
SSD_STATE = 128
SSD_CONV = 5
SSD_CHUNK = 128
SSD_CONV_CH = SSD_WIDTH + 2 * SSD_GROUPS * SSD_STATE
N_EXPERTS = 32
TOP_K = 4
D_EXPERT = 2048
SWIGLU_LIMIT = 7.0
SWIGLU_ALPHA = 1.702
DN_ALPHA = 2.0 ** 0.25
LN_EPS = 1e-5

T_CTX = BATCH * SEQ
T_DEC = DEC_BATCH * DEC_SEQ
T_ALL = T_CTX + T_DEC
HEADS_PER_GROUP = SSD_HEADS // SSD_GROUPS
GROUP_W = HEADS_PER_GROUP * SSD_HEAD_DIM
ROWS = DEC_SEQ // GRID_W
RPB_SLOTS = 16

MOE_TM = 256
MOE_TILES = T_ALL * TOP_K // MOE_TM + N_EXPERTS
MOE_ROWS = MOE_TILES * MOE_TM

VMEM_LIMIT = 56 * 1024 * 1024


def _cparams(sem):
    return pltpu.CompilerParams(dimension_semantics=sem, vmem_limit_bytes=VMEM_LIMIT)


def _split3(x):
    hi = x.astype(BF16)
    r1 = x - hi.astype(F32)
    mid = r1.astype(BF16)
    lo = (r1 - mid.astype(F32)).astype(BF16)
    return hi, mid, lo


def _dot(a, b):
    return jnp.dot(a, b, preferred_element_type=F32)


def _dot_nt(a, b):
    return lax.dot_general(a, b, (((1,), (1,)), ((), ())), preferred_element_type=F32)


def _dot_exact_rhs(x, m_bf):
    hi, mid, lo = _split3(x)
    return _dot(hi, m_bf) + (_dot(mid, m_bf) + _dot(lo, m_bf))


def _dot_exact_lhs(m_bf, x):
    hi, mid, lo = _split3(x)
    return _dot(m_bf, hi) + (_dot(m_bf, mid) + _dot(m_bf, lo))


def _dot_nt_f32(a, b):
    ah, am, al = _split3(a)
    bh, bm, bl = _split3(b)
    small = _dot_nt(ah, bl) + _dot_nt(am, bm) + _dot_nt(al, bh)
    mid = _dot_nt(ah, bm) + _dot_nt(am, bh)
    return _dot_nt(ah, bh) + (mid + small)


def _dot_f32(a, b):
    ah, am, al = _split3(a)
    bh, bm, bl = _split3(b)
    small = _dot(ah, bl) + _dot(am, bm) + _dot(al, bh)
    mid = _dot(ah, bm) + _dot(am, bh)
    return _dot(ah, bh) + (mid + small)


def _silu(x):
    return x * jax.nn.sigmoid(x)


def _softplus(x):
    return jnp.maximum(x, 0.0) + jnp.log1p(jnp.exp(-jnp.abs(x)))


def _layer_norm(x):
    mu = jnp.mean(x, axis=-1, keepdims=True)
    xc = x - mu
    var = jnp.mean(xc * xc, axis=-1, keepdims=True)
    return xc * lax.rsqrt(var + LN_EPS)


def _mod_row(i, tm):
    n_ctx = T_CTX // tm
    per_b = DEC_SEQ // tm
    return jnp.where(i < n_ctx, 0, 1 + (i - n_ctx) // per_b)


def _mod_kernel(c_ref, w_ref, b_ref, o_ref):
    o_ref[...] = _dot_f32(_silu(c_ref[...]), w_ref[...]) + b_ref[...]


def _modulation(cvec, w_mod, b_mod):
    tn = 1024
    n = w_mod.shape[1]
    return pl.pallas_call(
        _mod_kernel,
        grid=(n // tn,),
        in_specs=[pl.BlockSpec((8, D_MODEL), lambda j: (0, 0)),
                  pl.BlockSpec((D_MODEL, tn), lambda j: (0, j)),
                  pl.BlockSpec((1, tn), lambda j: (0, j))],
        out_specs=pl.BlockSpec((8, tn), lambda j: (0, j)),
        out_shape=jax.ShapeDtypeStruct((8, n), F32),
        compiler_params=_cparams(("arbitrary",)),
        name="modulation",
    )(cvec, w_mod, b_mod.reshape(1, n))


def _ln_mod_kernel(xc_ref, xd_ref, m_ref, o_ref, *, n_ctx):
    i = pl.program_id(0)
    x = jnp.where(i < n_ctx, xc_ref[...], xd_ref[...])
    y = _layer_norm(x) * (1.0 + m_ref[0, 1:2, :]) + m_ref[0, 0:1, :]
    o_ref[...] = y.astype(o_ref.dtype)


def _ln_modulate(x_ctx, x_dec, mods):
    tm = 512
    n_ctx = T_CTX // tm
    return pl.pallas_call(
        functools.partial(_ln_mod_kernel, n_ctx=n_ctx),
        grid=(T_ALL // tm,),
        in_specs=[pl.BlockSpec((tm, D_MODEL), lambda i: (jnp.minimum(i, n_ctx - 1), 0)),
                  pl.BlockSpec((tm, D_MODEL), lambda i: (jnp.maximum(i - n_ctx, 0), 0)),
                  pl.BlockSpec((1, 6, D_MODEL), lambda i: (_mod_row(i, tm), 0, 0))],
        out_specs=pl.BlockSpec((tm, D_MODEL), lambda i: (i, 0)),
        out_shape=jax.ShapeDtypeStruct((T_ALL, D_MODEL), BF16),
        compiler_params=_cparams(("arbitrary",)),
        name="ln_modulate",
    )(x_ctx, x_dec, mods)


def _mm_kernel(x_ref, w_ref, o_ref):
    o_ref[...] = _dot(x_ref[...], w_ref[...]).astype(o_ref.dtype)


def _matmul(x, w, *, rows, row_block0=0, col_block0=0, n_out=None, tm=1024, tn=1024,
            out_dtype=F32, name="matmul"):
    k = x.shape[1]
    n_out = w.shape[1] if n_out is None else n_out
    return pl.pallas_call(
        _mm_kernel,
        grid=(rows // tm, n_out // tn),
        in_specs=[pl.BlockSpec((tm, k), lambda i, j: (i + row_block0, 0)),
                  pl.BlockSpec((k, tn), lambda i, j: (0, j + col_block0))],
        out_specs=pl.BlockSpec((tm, tn), lambda i, j: (i, j)),
        out_shape=jax.ShapeDtypeStruct((rows, n_out), out_dtype),
        compiler_params=_cparams(("arbitrary", "arbitrary")),
        name=name,
    )(x, w)


def _ctx_attn_kernel(q_ref, k_ref, v_ref, o_ref):
    scale = NA_HEAD_DIM ** -0.5
    for h in range(NA_HEADS):
        sl = slice(h * NA_HEAD_DIM, (h + 1) * NA_HEAD_DIM)
        q = q_ref[:, sl].astype(BF16)
        k = k_ref[:, sl].astype(BF16)
        v = v_ref[:, sl].astype(BF16)
        s = _dot_nt(q, k) * scale
        m = jnp.max(s, axis=-1, keepdims=True)
        p = jnp.exp(s - m)
        l = jnp.sum(p, axis=-1, keepdims=True)
        o_ref[:, sl] = _dot(p.astype(BF16), v) / l


def _context_attention(q_all, k_ctx, v_ctx):
    spec = pl.BlockSpec((SEQ, NA_WIDTH), lambda b: (b, 0))
    return pl.pallas_call(
        _ctx_attn_kernel,
        grid=(BATCH,),
        in_specs=[spec, spec, spec],
        out_specs=spec,
        out_shape=jax.ShapeDtypeStruct((T_CTX, NA_WIDTH), F32),
        compiler_params=_cparams(("arbitrary",)),
        name="context_attention",
    )(q_all, k_ctx, v_ctx)


def _nbr_attn_kernel(q_ref, k_ref, v_ref, kc_ref, vc_ref, rc_ref, o_ref):
    scale = NA_HEAD_DIM ** -0.5
    kr = min(NA_WIN_R, ROWS)
    kc = kc_ref[0].astype(BF16)
    vc = vc_ref[0].astype(BF16)
    for r in range(ROWS):
        r0 = min(max(r - kr // 2, 0), ROWS - kr)
        off = (r0 - r + NA_WIN_R - 1) * GRID_W
        q = q_ref[r * GRID_W:(r + 1) * GRID_W, :].astype(BF16)
        kb = k_ref[r0 * GRID_W:(r0 + kr) * GRID_W, :].astype(BF16)
        vb = v_ref[r0 * GRID_W:(r0 + kr) * GRID_W, :].astype(BF16)
        s_loc = _dot_nt(q, kb) * scale + rc_ref[0, :, off:off + kr * GRID_W]
        s_ctx = _dot_nt(q, kc) * scale
        m = jnp.maximum(jnp.max(s_loc, axis=-1, keepdims=True),
                        jnp.max(s_ctx, axis=-1, keepdims=True))
        p_loc = jnp.exp(s_loc - m)
        p_ctx = jnp.exp(s_ctx - m)
        l = jnp.sum(p_loc, axis=-1, keepdims=True) + jnp.sum(p_ctx, axis=-1, keepdims=True)
        o = _dot(p_loc.astype(BF16), vb) + _dot(p_ctx.astype(BF16), vc)
        o_ref[r * GRID_W:(r + 1) * GRID_W, :] = o / l


def _rpb_table(rpb):
    col = jnp.arange(GRID_W)
    c0 = jnp.clip(col - NA_WIN_C // 2, 0, GRID_W - NA_WIN_C)
    col_mask = (col[None, :] >= c0[:, None]) & (col[None, :] < c0[:, None] + NA_WIN_C)
    dc_idx = jnp.clip(col[None, :] - col[:, None] + NA_WIN_C - 1, 0, 2 * NA_WIN_C - 2)
    t = rpb[:, :, dc_idx]
    t = jnp.where(col_mask[None, None], t, -jnp.inf)
    t = jnp.transpose(t, (0, 2, 1, 3))
    t = jnp.pad(t, ((0, 0), (0, 0), (0, RPB_SLOTS - t.shape[2]), (0, 0)))
    return t.reshape(NA_HEADS, GRID_W, RPB_SLOTS * GRID_W)


def _neighbourhood_attention(q_all, k_dec, v_dec, kc, vc, rc):
    q_row0 = T_CTX // DEC_SEQ
    hd = NA_HEAD_DIM
    return pl.pallas_call(
        _nbr_attn_kernel,
        grid=(DEC_BATCH, NA_HEADS),
        in_specs=[pl.BlockSpec((DEC_SEQ, hd), lambda b, h: (b + q_row0, h)),
                  pl.BlockSpec((DEC_SEQ, hd), lambda b, h: (b, h)),
                  pl.BlockSpec((DEC_SEQ, hd), lambda b, h: (b, h)),
                  pl.BlockSpec((1, PAST_LEN, hd), lambda b, h: (b, 0, h)),
                  pl.BlockSpec((1, PAST_LEN, hd), lambda b, h: (b, 0, h)),
                  pl.BlockSpec((1, GRID_W, RPB_SLOTS * GRID_W), lambda b, h: (h, 0, 0))],
        out_specs=pl.BlockSpec((DEC_SEQ, hd), lambda b, h: (b, h)),
        out_shape=jax.ShapeDtypeStruct((T_DEC, NA_WIDTH), F32),
        compiler_params=_cparams(("arbitrary", "arbitrary")),
        name="neighbourhood_attention",
    )(q_all, k_dec, v_dec, kc, vc, rc)


def _conv_silu(u_ref, w_ref, b_ref, length):
    u = u_ref[...]
    t = lax.broadcasted_iota(jnp.int32, (length, 1), 0)
    acc = u * w_ref[SSD_CONV // 2:SSD_CONV // 2 + 1, :] + b_ref[...]
    for k in range(SSD_CONV):
        d = k - SSD_CONV // 2
        if d == 0:
            continue
        shifted = pltpu.roll(u, (-d) % length, 0)
        valid = (t + d >= 0) & (t + d < length)
        acc = acc + jnp.where(valid, shifted, 0.0) * w_ref[k:k + 1, :]
    return _silu(acc)


def _ssd_kernel(*refs, length, has_h0, emit_state):
    (xs_ref, b_ref, c_ref, z_ref, dt_ref, dtt_ref, cwx_ref, cwb_ref, cwc_ref,
     cbx_ref, cbb_ref, cbc_ref, dtb_row_ref, dtb_col_ref, al_row_ref, al_col_ref,
     e_ref, d_ref) = refs[:18]
    pos = 18
    if has_h0:
        h0f_ref, h0b_ref = refs[pos:pos + 2]
        pos += 2
    u_ref = refs[pos]
    pos += 1
    if emit_state:
        hf_ref, hb_ref = refs[pos:pos + 2]
        pos += 2
    xs_s, b_s, c_s, y_s, st_s = refs[pos:]

    q = SSD_CHUNK
    nc = length // q
    xs_s[...] = _conv_silu(xs_ref, cwx_ref, cbx_ref, length)
    b_s[...] = _conv_silu(b_ref, cwb_ref, cbb_ref, length)
    c_s[...] = _conv_silu(c_ref, cwc_ref, cbc_ref, length)

    for dirn in range(2):
        if has_h0:
            h0 = (h0f_ref if dirn == 0 else h0b_ref)[0, 0]
            st_s[dirn] = jnp.transpose(h0.reshape(GROUP_W, SSD_STATE))
        else:
            st_s[dirn] = jnp.zeros((SSD_STATE, GROUP_W), F32)

    ri = lax.broadcasted_iota(jnp.int32, (q, q), 0)
    ci = lax.broadcasted_iota(jnp.int32, (q, q), 1)

    def chunk(c, dirn):
        r0 = pl.multiple_of(c * q, q)
        tri = (ri >= ci) if dirn == 0 else (ri <= ci)
        tri_t = (ri <= ci) if dirn == 0 else (ri >= ci)
        tri_bf = jnp.where(tri, 1.0, 0.0).astype(BF16)
        tri_t_bf = jnp.where(tri_t, 1.0, 0.0).astype(BF16)
        xs_c = xs_s[pl.ds(r0, q), :]
        bc = b_s[pl.ds(r0, q), :]
        cc = c_s[pl.ds(r0, q), :]
        dtp = _softplus(dt_ref[pl.ds(r0, q), :] + dtb_row_ref[...])
        dt_e = _dot_exact_rhs(dtp, e_ref[dirn, 0].astype(BF16))
        a_row = -jnp.exp(al_row_ref[dirn, 0])
        cs_e = _dot_exact_lhs(tri_bf, dt_e * a_row)
        dt_t = _softplus(dtt_ref[c, dirn, 0] + dtb_col_ref[dirn, 0])
        da_t = dt_t * (-jnp.exp(al_col_ref[dirn, 0]))
        cs_t = _dot_exact_rhs(da_t, tri_t_bf)

        cb = _dot_nt(cc.astype(BF16), bc.astype(BF16))
        bt = jnp.transpose(bc).astype(BF16)
        end = q - 1 if dirn == 0 else 0
        cs_end = cs_e[end:end + 1, :]
        xdt = xs_c * dt_e
        st = st_s[dirn]
        y = _dot(cc.astype(BF16), st.astype(BF16)) * jnp.exp(cs_e)
        st_s[dirn] = jnp.exp(cs_end) * st + _dot(bt, (xdt * jnp.exp(cs_end - cs_e)).astype(BF16))
        xdt_bf = xdt.astype(BF16)
        parts = []
        for r in range(HEADS_PER_GROUP):
            sl = slice(r * SSD_HEAD_DIM, (r + 1) * SSD_HEAD_DIM)
            diff = cs_e[:, r * SSD_HEAD_DIM:r * SSD_HEAD_DIM + 1] - cs_t[r:r + 1, :]
            lm = jnp.exp(jnp.where(tri, diff, -jnp.inf))
            parts.append(_dot((cb * lm).astype(BF16), xdt_bf[:, sl]))
        y = y + jnp.concatenate(parts, axis=-1)
        if dirn == 0:
            y_s[pl.ds(r0, q), :] = y + d_ref[0] * xs_c
        else:
            y_s[pl.ds(r0, q), :] = y_s[pl.ds(r0, q), :] + y

    def fwd(i, carry):
        chunk(i, 0)
        return carry

    def bwd(i, carry):
        chunk(nc - 1 - i, 1)
        return carry

    lax.fori_loop(0, nc, fwd, 0)
    lax.fori_loop(0, nc, bwd, 0)

    u_ref[...] = y_s[...] * _silu(z_ref[...])
    if emit_state:
        hf_ref[0, 0] = jnp.transpose(st_s[0]).reshape(HEADS_PER_GROUP, SSD_HEAD_DIM, SSD_STATE)
        hb_ref[0, 0] = jnp.transpose(st_s[1]).reshape(HEADS_PER_GROUP, SSD_HEAD_DIM, SSD_STATE)


def _ssd_mixer(zx, dt_all, dt_t, consts, *, length, n_seq, seq_block0, h0=None, emit_state=False):
    (cw, cb, dtb_row, dtb_col, al_row, al_col, expand, d_row) = consts
    g_w, n_s = GROUP_W, SSD_STATE
    xs_cb0 = D_MODEL // g_w
    b_cb0 = (D_MODEL + SSD_WIDTH) // n_s
    c_cb0 = b_cb0 + SSD_GROUPS
    cw_b0 = SSD_WIDTH // n_s
    nck = length // SSD_CHUNK
    in_specs = [
        pl.BlockSpec((length, g_w), lambda s, g: (s + seq_block0, xs_cb0 + g)),
        pl.BlockSpec((length, n_s), lambda s, g: (s + seq_block0, b_cb0 + g)),
        pl.BlockSpec((length, n_s), lambda s, g: (s + seq_block0, c_cb0 + g)),
        pl.BlockSpec((length, g_w), lambda s, g: (s + seq_block0, g)),
        pl.BlockSpec((length, 128), lambda s, g: (s + seq_block0, 0)),
        pl.BlockSpec((nck, 2, 1, HEADS_PER_GROUP, SSD_CHUNK), lambda s, g: (s + seq_block0, 0, g, 0, 0)),
        pl.BlockSpec((SSD_CONV, g_w), lambda s, g: (0, g)),
        pl.BlockSpec((SSD_CONV, n_s), lambda s, g: (0, cw_b0 + g)),
        pl.BlockSpec((SSD_CONV, n_s), lambda s, g: (0, cw_b0 + SSD_GROUPS + g)),
        pl.BlockSpec((1, g_w), lambda s, g: (0, g)),
        pl.BlockSpec((1, n_s), lambda s, g: (0, cw_b0 + g)),
        pl.BlockSpec((1, n_s), lambda s, g: (0, cw_b0 + SSD_GROUPS + g)),
        pl.BlockSpec((1, 128), lambda s, g: (0, 0)),
        pl.BlockSpec((2, 1, HEADS_PER_GROUP, SSD_CHUNK), lambda s, g: (0, g, 0, 0)),
        pl.BlockSpec((2, 1, 1, g_w), lambda s, g: (0, g, 0, 0)),
        pl.BlockSpec((2, 1, HEADS_PER_GROUP, SSD_CHUNK), lambda s, g: (0, g, 0, 0)),
        pl.BlockSpec((2, 1, 128, g_w), lambda s, g: (0, g, 0, 0)),
        pl.BlockSpec((1, 1, g_w), lambda s, g: (g, 0, 0)),
    ]
    args = [zx, zx, zx, zx, dt_all, dt_t, cw, cw, cw, cb, cb, cb,
            dtb_row, dtb_col, al_row, al_col, expand, d_row]
    st_spec = pl.BlockSpec((1, 1, HEADS_PER_GROUP, SSD_HEAD_DIM, n_s), lambda s, g: (s, g, 0, 0, 0))
    if h0 is not None:
        in_specs += [st_spec, st_spec]
        args += list(h0)
    out_specs = [pl.BlockSpec((length, g_w), lambda s, g: (s, g))]
    out_shape = [jax.ShapeDtypeStruct((n_seq * length, SSD_WIDTH), F32)]
    if emit_state:
        st_shape = jax.ShapeDtypeStruct((n_seq, SSD_GROUPS, HEADS_PER_GROUP, SSD_HEAD_DIM, n_s), F32)
        out_specs += [st_spec, st_spec]
        out_shape += [st_shape, st_shape]
    return pl.pallas_call(
        functools.partial(_ssd_kernel, length=length, has_h0=h0 is not None, emit_state=emit_state),
        grid=(n_seq, SSD_GROUPS),
        in_specs=in_specs,
        out_specs=out_specs,
        out_shape=out_shape,
        scratch_shapes=[pltpu.VMEM((length, g_w), F32), pltpu.VMEM((length, n_s), F32),
                        pltpu.VMEM((length, n_s), F32), pltpu.VMEM((length, g_w), F32),
                        pltpu.VMEM((2, n_s, g_w), F32)],
        compiler_params=_cparams(("arbitrary", "arbitrary")),
        name="ssd_mixer_%d" % length,
    )(*args)


def _ssd_consts(conv_w, conv_b, dt_bias, a_log, d_skip):
    hpg = HEADS_PER_GROUP
    dtb_row = jnp.pad(dt_bias.reshape(1, 2 * SSD_HEADS), ((0, 0), (0, 128 - 2 * SSD_HEADS)))
    col = lambda p: jnp.broadcast_to(p.reshape(2, SSD_GROUPS, hpg, 1), (2, SSD_GROUPS, hpg, SSD_CHUNK))
    al_row = jnp.repeat(a_log.reshape(2, SSD_GROUPS, 1, hpg), SSD_HEAD_DIM, axis=-1)
    src = (jnp.arange(2)[:, None, None] * SSD_HEADS + jnp.arange(SSD_GROUPS)[None, :, None] * hpg
           + jnp.arange(GROUP_W)[None, None, :] // SSD_HEAD_DIM)
    expand = (jnp.arange(128)[None, None, :, None] == src[:, :, None, :]).astype(F32)
    d_row = jnp.repeat(d_skip.reshape(SSD_GROUPS, 1, hpg), SSD_HEAD_DIM, axis=-1)
    return (conv_w, conv_b.reshape(1, SSD_CONV_CH), dtb_row, col(dt_bias), al_row, col(a_log), expand, d_row)


def _merge_kernel(nac_ref, nad_ref, uc_ref, ud_ref, nw_ref, w1_ref, w2_ref, g_na_ref, g_ssd_ref,
                  o_ref, a1_s, a2_s, *, n_ctx):
    i = pl.program_id(0)

    @pl.when(pl.program_id(1) == 0)
    def _():
        a1_s[...] = jnp.where(i < n_ctx, nac_ref[...], nad_ref[...]).astype(BF16)
        u = jnp.where(i < n_ctx, uc_ref[...], ud_ref[...])
        r = lax.rsqrt(jnp.mean(u * u, axis=-1, keepdims=True) + LN_EPS)
        a2_s[...] = (u * r * nw_ref[...]).astype(BF16)

    o = (jax.nn.sigmoid(g_na_ref[...]) * _dot(a1_s[...], w1_ref[...])
         + jax.nn.sigmoid(g_ssd_ref[...]) * _dot(a2_s[...], w2_ref[...]))
    o_ref[...] = o.astype(o_ref.dtype)


def _merge_branches(na_ctx, na_dec, u_ctx, u_dec, norm_w, w_na_out, w_ssd_out, gates):
    tm, tn = 256, 1024
    n_ctx = T_CTX // tm
    nj = D_MODEL // tn
    ctx_rows = lambda i, j: (jnp.minimum(i, n_ctx - 1), 0)
    dec_rows = lambda i, j: (jnp.maximum(i - n_ctx, 0), 0)
    return pl.pallas_call(
        functools.partial(_merge_kernel, n_ctx=n_ctx),
        grid=(T_ALL // tm, nj),
        in_specs=[pl.BlockSpec((tm, D_MODEL), ctx_rows), pl.BlockSpec((tm, D_MODEL), dec_rows),
                  pl.BlockSpec((tm, D_MODEL), ctx_rows), pl.BlockSpec((tm, D_MODEL), dec_rows),
                  pl.BlockSpec((1, D_MODEL), lambda i, j: (0, 0)),
                  pl.BlockSpec((D_MODEL, tn), lambda i, j: (0, j)),
                  pl.BlockSpec((D_MODEL, tn), lambda i, j: (0, j)),
                  pl.BlockSpec((tm, tn), lambda i, j: (i, nj + j)),
                  pl.BlockSpec((tm, tn), lambda i, j: (i, j))],
        out_specs=pl.BlockSpec((tm, tn), lambda i, j: (i, j)),
        out_shape=jax.ShapeDtypeStruct((T_ALL, D_MODEL), BF16),
        scratch_shapes=[pltpu.VMEM((tm, D_MODEL), BF16), pltpu.VMEM((tm, D_MODEL), BF16)],
        compiler_params=_cparams(("arbitrary", "arbitrary")),
        name="merge_branches",
    )(na_ctx, na_dec, u_ctx, u_dec, norm_w, w_na_out, w_ssd_out, gates, gates)


def _post_mix_kernel(y_ref, wo_ref, xc_ref, xd_ref, m_ref, g_ref, b_ref, wr_ref, br_ref,
                     x1_ref, h2_ref, idx_ref, wgt_ref, *, n_ctx):
    i = pl.program_id(0)
    x = jnp.where(i < n_ctx, xc_ref[...], xd_ref[...])
    mix = _dot(y_ref[...], wo_ref[...])
    x1 = _layer_norm(DN_ALPHA * x + m_ref[0, 2:3, :] * mix) * g_ref[...] + b_ref[...]
    x1_ref[...] = x1
    h2 = _layer_norm(x1) * (1.0 + m_ref[0, 4:5, :]) + m_ref[0, 3:4, :]
    h2_ref[...] = h2
    logits = _dot_nt_f32(wr_ref[...], h2) + br_ref[...]
    eidx = lax.broadcasted_iota(jnp.int32, logits.shape, 0)
    vals, idxs = [], []
    for _ in range(TOP_K):
        m = jnp.max(logits, axis=0, keepdims=True)
        sel = jnp.min(jnp.where(logits == m, eidx, N_EXPERTS), axis=0, keepdims=True)
        logits = jnp.where(eidx == sel, -jnp.inf, logits)
        vals.append(m)
        idxs.append(sel)
    ex = [jnp.exp(v - vals[0]) for v in vals]
    tot = ex[0] + ex[1] + ex[2] + ex[3]
    idx_ref[...] = jnp.concatenate(idxs, axis=0)
    wgt_ref[...] = jnp.concatenate([e / tot for e in ex], axis=0)


def _post_mix(y, w_o, x_ctx, x_dec, mods, ln_g, ln_b, w_router_t, b_router):
    tm = 256
    n_ctx = T_CTX // tm
    row = lambda i: (i, 0)
    const = lambda i: (0, 0)
    return pl.pallas_call(
        functools.partial(_post_mix_kernel, n_ctx=n_ctx),
        grid=(T_ALL // tm,),
        in_specs=[pl.BlockSpec((tm, D_MODEL), row),
                  pl.BlockSpec((D_MODEL, D_MODEL), const),
                  pl.BlockSpec((tm, D_MODEL), lambda i: (jnp.minimum(i, n_ctx - 1), 0)),
                  pl.BlockSpec((tm, D_MODEL), lambda i: (jnp.maximum(i - n_ctx, 0), 0)),
                  pl.BlockSpec((1, 6, D_MODEL), lambda i: (_mod_row(i, tm), 0, 0)),
                  pl.BlockSpec((1, D_MODEL), const), pl.BlockSpec((1, D_MODEL), const),
                  pl.BlockSpec((N_EXPERTS, D_MODEL), const), pl.BlockSpec((N_EXPERTS, 1), const)],
        out_specs=[pl.BlockSpec((tm, D_MODEL), row), pl.BlockSpec((tm, D_MODEL), row),
                   pl.BlockSpec((TOP_K, tm), lambda i: (0, i)), pl.BlockSpec((TOP_K, tm), lambda i: (0, i))],
        out_shape=[jax.ShapeDtypeStruct((T_ALL, D_MODEL), F32), jax.ShapeDtypeStruct((T_ALL, D_MODEL), F32),
                   jax.ShapeDtypeStruct((TOP_K, T_ALL), jnp.int32), jax.ShapeDtypeStruct((TOP_K, T_ALL), F32)],
        compiler_params=_cparams(("arbitrary",)),
        name="post_mix_router",
    )(y, w_o, x_ctx, x_dec, mods, ln_g, ln_b, w_router_t, b_router)


DISPATCH_TOK = 128


def _dispatch_kernel(pos_ref, meta_ref, x_hbm, xs_hbm, zero_s, sem_z, sem_r):
    i = pl.program_id(0)

    def zero_copy(tile):
        return pltpu.make_async_copy(zero_s, xs_hbm.at[pl.ds(pl.multiple_of(tile * MOE_TM, MOE_TM), MOE_TM)], sem_z)

    @pl.when(i == 0)
    def _():
        zero_s[...] = jnp.zeros(zero_s.shape, zero_s.dtype)
        n_valid = meta_ref[0]
        for e in range(N_EXPERTS):
            @pl.when(meta_ref[1 + e] >= 0)
            def _():
                zero_copy(meta_ref[1 + e]).start()

        def tail_start(t, c):
            zero_copy(t).start()
            return c
        lax.fori_loop(n_valid, MOE_TILES, tail_start, 0)
        for e in range(N_EXPERTS):
            @pl.when(meta_ref[1 + e] >= 0)
            def _():
                zero_copy(0).wait()

        def tail_wait(t, c):
            zero_copy(0).wait()
            return c
        lax.fori_loop(n_valid, MOE_TILES, tail_wait, 0)

    def row_copy(tok, dst):
        return pltpu.make_async_copy(x_hbm.at[pl.ds(tok, 1)], xs_hbm.at[pl.ds(dst, 1)], sem_r)

    base = i * DISPATCH_TOK
    for k in range(TOP_K):
        def issue(t, c):
            row_copy(base + t, pos_ref[0, k, t]).start()
            return c
        lax.fori_loop(0, DISPATCH_TOK, issue, 0)

    def drain(t, c):
        row_copy(0, 0).wait()
        return c
    lax.fori_loop(0, DISPATCH_TOK * TOP_K, drain, 0)


def _dispatch(h2, pos_blocks, meta):
    return pl.pallas_call(
        _dispatch_kernel,
        grid=(T_ALL // DISPATCH_TOK,),
        in_specs=[pl.BlockSpec((1, TOP_K, DISPATCH_TOK), lambda i: (i, 0, 0), memory_space=pltpu.SMEM),
                  pl.BlockSpec(memory_space=pltpu.SMEM),
                  pl.BlockSpec(memory_space=pl.ANY)],
        out_specs=pl.BlockSpec(memory_space=pl.ANY),
        out_shape=jax.ShapeDtypeStruct((MOE_ROWS, D_MODEL), F32),
        scratch_shapes=[pltpu.VMEM((MOE_TM, D_MODEL), F32),
                        pltpu.SemaphoreType.DMA(()), pltpu.SemaphoreType.DMA(())],
        compiler_params=_cparams(("arbitrary",)),
        name="moe_dispatch",
    )(pos_blocks, meta, h2)


def _gate_up_kernel(te_ref, nv_ref, x_ref, wg_ref, wu_ref, bg_ref, bu_ref, o_ref, wg_s, wu_s):
    t = pl.program_id(1)
    prev = te_ref[jnp.maximum(t - 1, 0)]

    @pl.when((t == 0) | (te_ref[t] != prev))
    def _():
        wg_s[...] = wg_ref[0].astype(BF16)
        wu_s[...] = wu_ref[0].astype(BF16)

    @pl.when(t < nv_ref[0])
    def _():
        x = x_ref[...].astype(BF16)
        gate = jnp.minimum(_dot(x, wg_s[...]) + bg_ref[0], SWIGLU_LIMIT)
        up = jnp.clip(_dot(x, wu_s[...]) + bu_ref[0], -SWIGLU_LIMIT, SWIGLU_LIMIT)
        hid = (up + 1.0) * gate * jax.nn.sigmoid(SWIGLU_ALPHA * gate)
        o_ref[...] = hid.astype(o_ref.dtype)

    @pl.when(t >= nv_ref[0])
    def _():
        o_ref[...] = jnp.zeros(o_ref.shape, o_ref.dtype)


def _moe_gate_up(xs, tile_expert, n_valid, w_gate_up, b_gate_up):
    tn = 512
    nj = D_EXPERT // tn
    row = lambda j, t, te, nv: (jnp.minimum(t, nv[0] - 1), 0)
    grid_spec = pltpu.PrefetchScalarGridSpec(
        num_scalar_prefetch=2,
        grid=(nj, MOE_TILES),
        in_specs=[pl.BlockSpec((MOE_TM, D_MODEL), row),
                  pl.BlockSpec((1, D_MODEL, tn), lambda j, t, te, nv: (te[t], 0, j)),
                  pl.BlockSpec((1, D_MODEL, tn), lambda j, t, te, nv: (te[t], 0, nj + j)),
                  pl.BlockSpec((1, 1, tn), lambda j, t, te, nv: (te[t], 0, j)),
                  pl.BlockSpec((1, 1, tn), lambda j, t, te, nv: (te[t], 0, nj + j))],
        out_specs=pl.BlockSpec((MOE_TM, tn), lambda j, t, te, nv: (t, j)),
        scratch_shapes=[pltpu.VMEM((D_MODEL, tn), BF16), pltpu.VMEM((D_MODEL, tn), BF16)],
    )
    return pl.pallas_call(
        _gate_up_kernel,
        grid_spec=grid_spec,
        out_shape=jax.ShapeDtypeStruct((MOE_ROWS, D_EXPERT), BF16),
        compiler_params=_cparams(("arbitrary", "arbitrary")),
        name="moe_gate_up",
    )(tile_expert, n_valid, xs, w_gate_up, w_gate_up, b_gate_up, b_gate_up)


def _down_kernel(te_ref, nv_ref, h_ref, w_ref, b_ref, o_ref, w_s):
    t = pl.program_id(1)
    prev = te_ref[jnp.maximum(t - 1, 0)]

    @pl.when((t == 0) | (te_ref[t] != prev))
    def _():
        w_s[...] = w_ref[0].astype(BF16)

    @pl.when(t < nv_ref[0])
    def _():
        o_ref[...] = _dot(h_ref[...], w_s[...]) + b_ref[0]

    @pl.when(t >= nv_ref[0])
    def _():
        o_ref[...] = jnp.zeros(o_ref.shape, o_ref.dtype)


def _moe_down(hid, tile_expert, n_valid, w_down, b_down):
    tn = 1024
    grid_spec = pltpu.PrefetchScalarGridSpec(
        num_scalar_prefetch=2,
        grid=(D_MODEL // tn, MOE_TILES),
        in_specs=[pl.BlockSpec((MOE_TM, D_EXPERT), lambda j, t, te, nv: (jnp.minimum(t, nv[0] - 1), 0)),
                  pl.BlockSpec((1, D_EXPERT, tn), lambda j, t, te, nv: (te[t], 0, j)),
                  pl.BlockSpec((1, 1, tn), lambda j, t, te, nv: (te[t], 0, j))],
        out_specs=pl.BlockSpec((MOE_TM, tn), lambda j, t, te, nv: (t, j)),
        scratch_shapes=[pltpu.VMEM((D_EXPERT, tn), BF16)],
    )
    return pl.pallas_call(
        _down_kernel,
        grid_spec=grid_spec,
        out_shape=jax.ShapeDtypeStruct((MOE_ROWS, D_MODEL), F32),
        compiler_params=_cparams(("arbitrary", "arbitrary")),
        name="moe_down",
    )(tile_expert, n_valid, hid, w_down, b_down)


COMBINE_TOK = 64


def _combine_kernel(pos_ref, ys_hbm, wgt_ref, x1_ref, m_ref, g_ref, b_ref, o_ref, buf, sem):
    def row_copy(k, t, src):
        return pltpu.make_async_copy(ys_hbm.at[pl.ds(src, 1)], buf.at[k, pl.ds(t, 1)], sem)

    for k in range(TOP_K):
        def issue(t, c):
            row_copy(k, t, pos_ref[0, k, t]).start()
            return c
        lax.fori_loop(0, COMBINE_TOK, issue, 0)

    def drain(t, c):
        row_copy(0, 0, 0).wait()
        return c
    lax.fori_loop(0, COMBINE_TOK * TOP_K, drain, 0)

    w = wgt_ref[...]
    ffn = w[:, 0:1] * buf[0]
    for k in range(1, TOP_K):
        ffn = ffn + w[:, k:k + 1] * buf[k]
    x2 = _layer_norm(DN_ALPHA * x1_ref[...] + m_ref[0, 5:6, :] * ffn) * g_ref[...] + b_ref[...]
    o_ref[...] = x2


def _combine(ys, pos_blocks, wgt, x1, mods, ln_g, ln_b, *, rows, row0):
    tm = COMBINE_TOK
    b0 = row0 // tm
    const = lambda i: (0, 0)
    return pl.pallas_call(
        _combine_kernel,
        grid=(rows // tm,),
        in_specs=[pl.BlockSpec((1, TOP_K, tm), lambda i: (i + b0, 0, 0), memory_space=pltpu.SMEM),
                  pl.BlockSpec(memory_space=pl.ANY),
                  pl.BlockSpec((tm, TOP_K), lambda i: (i + b0, 0)),
                  pl.BlockSpec((tm, D_MODEL), lambda i: (i + b0, 0)),
                  pl.BlockSpec((1, 6, D_MODEL), lambda i: (_mod_row(i + b0, tm), 0, 0)),
                  pl.BlockSpec((1, D_MODEL), const), pl.BlockSpec((1, D_MODEL), const)],
        out_specs=pl.BlockSpec((tm, D_MODEL), lambda i: (i, 0)),
        out_shape=jax.ShapeDtypeStruct((rows, D_MODEL), F32),
        scratch_shapes=[pltpu.VMEM((TOP_K, tm, D_MODEL), F32), pltpu.SemaphoreType.DMA(())],
        compiler_params=_cparams(("arbitrary",)),
        name="moe_combine_ln2",
    )(pos_blocks, ys, wgt, x1, mods, ln_g, ln_b)


def _moe_plan(idx_t):
    flat = idx_t.reshape(-1)
    onehot = (flat[:, None] == jnp.arange(N_EXPERTS)[None, :]).astype(jnp.int32)
    rank = jnp.take_along_axis(jnp.cumsum(onehot, axis=0), flat[:, None], axis=1)[:, 0] - 1
    counts = jnp.sum(onehot, axis=0)
    tiles = (counts + MOE_TM - 1) // MOE_TM
    tile_end = jnp.cumsum(tiles)
    tile_start = tile_end - tiles
    pos = (tile_start[flat] * MOE_TM + rank).reshape(TOP_K, T_ALL)
    n_valid = tile_end[-1]
    tile_ids = jnp.minimum(jnp.arange(MOE_TILES), n_valid - 1)
    tile_expert = jnp.sum((tile_end[None, :] <= tile_ids[:, None]).astype(jnp.int32), axis=1)
    last_tile = jnp.where(tiles > 0, tile_end - 1, -1)
    meta = jnp.concatenate([n_valid[None], last_tile]).astype(jnp.int32)
    return pos.astype(jnp.int32), tile_expert.astype(jnp.int32), n_valid.reshape(1).astype(jnp.int32), meta


def kernel(x_prompt, x_sample, cache_na_k, cache_na_v, state_ssd_fwd, state_ssd_bwd, c, c_ctx, w_mod, b_mod, w_in, ssd_conv_w, ssd_conv_b, ssd_dt_bias, ssd_a_log, ssd_d, ssd_norm_w, na_rpb, w_ssd_out, w_na_out, w_o, ln1_g, ln1_b, ln2_g, ln2_b, w_router, b_router, w_gate_up, b_gate_up, w_down, b_down):
    assert w_mod.shape[0] == 1, "single-layer trunk"
    x_ctx = x_prompt.reshape(T_CTX, D_MODEL)
    x_dec = x_sample.reshape(T_DEC, D_MODEL)

    cvec = jnp.concatenate([c_ctx[None], c, jnp.zeros((8 - 1 - DEC_BATCH, D_MODEL), F32)], axis=0)
    mods = _modulation(cvec, w_mod[0], b_mod[0])[:1 + DEC_BATCH].reshape(1 + DEC_BATCH, 6, D_MODEL)

    h = _ln_modulate(x_ctx, x_dec, mods)
    w = w_in[0]
    n_main = 3 * NA_WIDTH + SSD_WIDTH + SSD_CONV_CH
    w_main = w[:, :n_main].astype(BF16)
    w_dt = jnp.pad(w[:, n_main:n_main + 2 * SSD_HEADS], ((0, 0), (0, 128 - 2 * SSD_HEADS))).astype(BF16)
    w_gates = w[:, n_main + 2 * SSD_HEADS:].astype(BF16)
    tm = 1024
    ctx_blocks = T_CTX // tm
    q_all = _matmul(h, w_main, rows=T_ALL, col_block0=0, n_out=NA_WIDTH, name="proj_q")
    k_ctx = _matmul(h, w_main, rows=T_CTX, col_block0=2, n_out=NA_WIDTH, name="proj_k_ctx")
    k_dec = _matmul(h, w_main, rows=T_DEC, row_block0=ctx_blocks, col_block0=2, n_out=NA_WIDTH, name="proj_k_dec")
    v_ctx = _matmul(h, w_main, rows=T_CTX, col_block0=4, n_out=NA_WIDTH, name="proj_v_ctx")
    v_dec = _matmul(h, w_main, rows=T_DEC, row_block0=ctx_blocks, col_block0=4, n_out=NA_WIDTH, name="proj_v_dec")
    zx = _matmul(h, w_main, rows=T_ALL, col_block0=6, n_out=SSD_WIDTH + SSD_CONV_CH, name="proj_zxbc")
    gates = _matmul(h, w_gates, rows=T_ALL, name="proj_gates")
    dt_all = _matmul(h, w_dt, rows=T_ALL, tn=128, name="proj_dt")

    na_ctx = _context_attention(q_all, k_ctx, v_ctx)
    kc = cache_na_k[:, 0].reshape(DEC_BATCH, PAST_LEN, NA_WIDTH)
    vc = cache_na_v[:, 0].reshape(DEC_BATCH, PAST_LEN, NA_WIDTH)
    na_dec = _neighbourhood_attention(q_all, k_dec, v_dec, kc, vc, _rpb_table(na_rpb[0]))

    consts = _ssd_consts(ssd_conv_w[0], ssd_conv_b[0], ssd_dt_bias[0], ssd_a_log[0], ssd_d[0])
    dt_t = dt_all[:, :2 * SSD_HEADS].reshape(T_ALL // SSD_CHUNK, SSD_CHUNK, 2, SSD_GROUPS, HEADS_PER_GROUP)
    dt_t = jnp.transpose(dt_t, (0, 2, 3, 4, 1))
    u_ctx, h_f, h_b = _ssd_mixer(zx, dt_all, dt_t, consts, length=SEQ, n_seq=BATCH, seq_block0=0,
                                 emit_state=True)
    h0 = (state_ssd_fwd[:, 0].reshape(DEC_BATCH, SSD_GROUPS, HEADS_PER_GROUP, SSD_HEAD_DIM, SSD_STATE),
          state_ssd_bwd[:, 0].reshape(DEC_BATCH, SSD_GROUPS, HEADS_PER_GROUP, SSD_HEAD_DIM, SSD_STATE))
    (u_dec,) = _ssd_mixer(zx, dt_all, dt_t, consts, length=DEC_SEQ, n_seq=DEC_BATCH,
                          seq_block0=T_CTX // DEC_SEQ, h0=h0)

    y = _merge_branches(na_ctx, na_dec, u_ctx, u_dec, ssd_norm_w[0].reshape(1, SSD_WIDTH),
                        w_na_out[0].astype(BF16), w_ssd_out[0].astype(BF16), gates)
    x1, h2, idx_t, wgt_t = _post_mix(y, w_o[0].astype(BF16), x_ctx, x_dec, mods,
                                     ln1_g[0].reshape(1, D_MODEL), ln1_b[0].reshape(1, D_MODEL),
                                     jnp.transpose(w_router[0]), b_router[0].reshape(N_EXPERTS, 1))

    pos, tile_expert, n_valid, meta = _moe_plan(idx_t)
    pos_d = jnp.transpose(pos.reshape(TOP_K, T_ALL // DISPATCH_TOK, DISPATCH_TOK), (1, 0, 2))
    xs = _dispatch(h2, pos_d, meta)
    hid = _moe_gate_up(xs, tile_expert, n_valid, w_gate_up[0], b_gate_up[0].reshape(N_EXPERTS, 1, 2 * D_EXPERT))
    ys = _moe_down(hid, tile_expert, n_valid, w_down[0], b_down[0].reshape(N_EXPERTS, 1, D_MODEL))
    pos_c = jnp.transpose(pos.reshape(TOP_K, T_ALL // COMBINE_TOK, COMBINE_TOK), (1, 0, 2))
    wgt = jnp.transpose(wgt_t)
    g2, b2 = ln2_g[0].reshape(1, D_MODEL), ln2_b[0].reshape(1, D_MODEL)
    y_ctx = _combine(ys, pos_c, wgt, x1, mods, g2, b2, rows=T_CTX, row0=0)
    y_dec = _combine(ys, pos_c, wgt, x1, mods, g2, b2, rows=T_DEC, row0=T_CTX)

    return (y_ctx.reshape(BATCH, SEQ, D_MODEL),
            y_dec.reshape(DEC_BATCH, DEC_SEQ, D_MODEL),
            k_ctx.reshape(BATCH, 1, SEQ, NA_HEADS, NA_HEAD_DIM),
            v_ctx.reshape(BATCH, 1, SEQ, NA_HEADS, NA_HEAD_DIM),
            h_f.reshape(BATCH, 1, SSD_HEADS, SSD_HEAD_DIM, SSD_STATE),
            h_b.reshape(BATCH, 1, SSD_HEADS, SSD_HEAD_DIM, SSD_STATE))
```

```python
import functools

import jax
import jax.numpy as jnp
from jax import lax
from jax.experimental import pallas as pl
from jax.experimental.pallas import tpu as pltpu

F32 = jnp.float32
BF16 = jnp.bfloat16

D_MODEL = 2048
BATCH = 32
SEQ = 256
DEC_BATCH = 2
DEC_SEQ = 1024
PAST_LEN = 512
GRID_W = 64
NA_HEADS = 16
NA_HEAD_DIM = 128
NA_WIDTH = NA_HEADS * NA_HEAD_DIM
NA_WIN_R = 8
NA_WIN_C = 16
SSD_HEADS = 32
SSD_HEAD_DIM = 64
SSD_WIDTH = SSD_HEADS * SSD_HEAD_DIM
SSD_GROUPS = 4
SSD_STATE = 128
SSD_CONV = 5
SSD_CHUNK = 128
SSD_CONV_CH = SSD_WIDTH + 2 * SSD_GROUPS * SSD_STATE
N_EXPERTS = 32
TOP_K = 4
D_EXPERT = 2048
SWIGLU_LIMIT = 7.0
SWIGLU_ALPHA = 1.702
DN_ALPHA = 2.0 ** 0.25
LN_EPS = 1e-5

T_CTX = BATCH * SEQ
T_DEC = DEC_BATCH * DEC_SEQ
T_ALL = T_CTX + T_DEC
HEADS_PER_GROUP = SSD_HEADS // SSD_GROUPS
GROUP_W = HEADS_PER_GROUP * SSD_HEAD_DIM
ROWS = DEC_SEQ // GRID_W
RPB_SLOTS = 16

MOE_TM = 256
MOE_TILES = T_ALL * TOP_K // MOE_TM + N_EXPERTS
MOE_ROWS = MOE_TILES * MOE_TM

VMEM_LIMIT = 56 * 1024 * 1024


def _cparams(sem):
    return pltpu.CompilerParams(dimension_semantics=sem, vmem_limit_bytes=VMEM_LIMIT)


def _split3(x):
    hi = x.astype(BF16)
    r1 = x - hi.astype(F32)
    mid = r1.astype(BF16)
    lo = (r1 - mid.astype(F32)).astype(BF16)
    return hi, mid, lo


def _dot(a, b):
    return jnp.dot(a, b, preferred_element_type=F32)


def _dot_nt(a, b):
    return lax.dot_general(a, b, (((1,), (1,)), ((), ())), preferred_element_type=F32)


def _dot_exact_rhs(x, m_bf):
    hi, mid, lo = _split3(x)
    return _dot(hi, m_bf) + (_dot(mid, m_bf) + _dot(lo, m_bf))


def _dot_exact_lhs(m_bf, x):
    hi, mid, lo = _split3(x)
    return _dot(m_bf, hi) + (_dot(m_bf, mid) + _dot(m_bf, lo))


def _dot_nt_f32(a, b):
    ah, am, al = _split3(a)
    bh, bm, bl = _split3(b)
    small = _dot_nt(ah, bl) + _dot_nt(am, bm) + _dot_nt(al, bh)
    mid = _dot_nt(ah, bm) + _dot_nt(am, bh)
    return _dot_nt(ah, bh) + (mid + small)


def _dot_f32(a, b):
    ah, am, al = _split3(a)
    bh, bm, bl = _split3(b)
    small = _dot(ah, bl) + _dot(am, bm) + _dot(al, bh)
    mid = _dot(ah, bm) + _dot(am, bh)
    return _dot(ah, bh) + (mid + small)


def _silu(x):
    return x * jax.nn.sigmoid(x)


def _softplus(x):
    return jnp.maximum(x, 0.0) + jnp.log1p(jnp.exp(-jnp.abs(x)))


def _layer_norm(x):
    mu = jnp.mean(x, axis=-1, keepdims=True)
    xc = x - mu
    var = jnp.mean(xc * xc, axis=-1, keepdims=True)
    return xc * lax.rsqrt(var + LN_EPS)


def _mod_row(i, tm):
    n_ctx = T_CTX // tm
    per_b = DEC_SEQ // tm
    return jnp.where(i < n_ctx, 0, 1 + (i - n_ctx) // per_b)


def _mod_kernel(c_ref, w_ref, b_ref, o_ref):
    o_ref[...] = _dot_f32(_silu(c_ref[...]), w_ref[...]) + b_ref[...]


def _modulation(cvec, w_mod, b_mod):
    tn = 1024
    n = w_mod.shape[1]
    return pl.pallas_call(
        _mod_kernel,
        grid=(n // tn,),
        in_specs=[pl.BlockSpec((8, D_MODEL), lambda j: (0, 0)),
                  pl.BlockSpec((D_MODEL, tn), lambda j: (0, j)),
                  pl.BlockSpec((1, tn), lambda j: (0, j))],
        out_specs=pl.BlockSpec((8, tn), lambda j: (0, j)),
        out_shape=jax.ShapeDtypeStruct((8, n), F32),
        compiler_params=_cparams(("arbitrary",)),
        name="modulation",
    )(cvec, w_mod, b_mod.reshape(1, n))


def _ln_mod_kernel(xc_ref, xd_ref, m_ref, o_ref, *, n_ctx):
    i = pl.program_id(0)
    x = jnp.where(i < n_ctx, xc_ref[...], xd_ref[...])
    y = _layer_norm(x) * (1.0 + m_ref[0, 1:2, :]) + m_ref[0, 0:1, :]
    o_ref[...] = y.astype(o_ref.dtype)


def _ln_modulate(x_ctx, x_dec, mods):
    tm = 512
    n_ctx = T_CTX // tm
    return pl.pallas_call(
        functools.partial(_ln_mod_kernel, n_ctx=n_ctx),
        grid=(T_ALL // tm,),
        in_specs=[pl.BlockSpec((tm, D_MODEL), lambda i: (jnp.minimum(i, n_ctx - 1), 0)),
                  pl.BlockSpec((tm, D_MODEL), lambda i: (jnp.maximum(i - n_ctx, 0), 0)),
                  pl.BlockSpec((1, 6, D_MODEL), lambda i: (_mod_row(i, tm), 0, 0))],
        out_specs=pl.BlockSpec((tm, D_MODEL), lambda i: (i, 0)),
        out_shape=jax.ShapeDtypeStruct((T_ALL, D_MODEL), BF16),
        compiler_params=_cparams(("arbitrary",)),
        name="ln_modulate",
    )(x_ctx, x_dec, mods)


def _mm_kernel(x_ref, w_ref, o_ref):
    o_ref[...] = _dot(x_ref[...], w_ref[...]).astype(o_ref.dtype)


def _matmul(x, w, *, rows, row_block0=0, col_block0=0, n_out=None, tm=1024, tn=1024,
            out_dtype=F32, name="matmul"):
    k = x.shape[1]
    n_out = w.shape[1] if n_out is None else n_out
    return pl.pallas_call(
        _mm_kernel,
        grid=(rows // tm, n_out // tn),
        in_specs=[pl.BlockSpec((tm, k), lambda i, j: (i + row_block0, 0)),
                  pl.BlockSpec((k, tn), lambda i, j: (0, j + col_block0))],
        out_specs=pl.BlockSpec((tm, tn), lambda i, j: (i, j)),
        out_shape=jax.ShapeDtypeStruct((rows, n_out), out_dtype),
        compiler_params=_cparams(("arbitrary", "arbitrary")),
        name=name,
    )(x, w)


def _ctx_attn_kernel(q_ref, k_ref, v_ref, o_ref):
    scale = NA_HEAD_DIM ** -0.5
    for h in range(NA_HEADS):
        sl = slice(h * NA_HEAD_DIM, (h + 1) * NA_HEAD_DIM)
        q = q_ref[:, sl].astype(BF16)
        k = k_ref[:, sl].astype(BF16)
        v = v_ref[:, sl].astype(BF16)
        s = _dot_nt(q, k) * scale
        m = jnp.max(s, axis=-1, keepdims=True)
        p = jnp.exp(s - m)
        l = jnp.sum(p, axis=-1, keepdims=True)
        o_ref[:, sl] = _dot(p.astype(BF16), v) / l


def _context_attention(q_all, k_ctx, v_ctx):
    spec = pl.BlockSpec((SEQ, NA_WIDTH), lambda b: (b, 0))
    return pl.pallas_call(
        _ctx_attn_kernel,
        grid=(BATCH,),
        in_specs=[spec, spec, spec],
        out_specs=spec,
        out_shape=jax.ShapeDtypeStruct((T_CTX, NA_WIDTH), F32),
        compiler_params=_cparams(("arbitrary",)),
        name="context_attention",
    )(q_all, k_ctx, v_ctx)


def _nbr_attn_kernel(q_ref, k_ref, v_ref, kc_ref, vc_ref, rc_ref, o_ref):
    scale = NA_HEAD_DIM ** -0.5
    kr = min(NA_WIN_R, ROWS)
    kc = kc_ref[0].astype(BF16)
    vc = vc_ref[0].astype(BF16)
    for r in range(ROWS):
        r0 = min(max(r - kr // 2, 0), ROWS - kr)
        off = (r0 - r + NA_WIN_R - 1) * GRID_W
        q = q_ref[r * GRID_W:(r + 1) * GRID_W, :].astype(BF16)
        kb = k_ref[r0 * GRID_W:(r0 + kr) * GRID_W, :].astype(BF16)
        vb = v_ref[r0 * GRID_W:(r0 + kr) * GRID_W, :].astype(BF16)
        s_loc = _dot_nt(q, kb) * scale + rc_ref[0, :, off:off + kr * GRID_W]
        s_ctx = _dot_nt(q, kc) * scale
        m = jnp.maximum(jnp.max(s_loc, axis=-1, keepdims=True),
                        jnp.max(s_ctx, axis=-1, keepdims=True))
        p_loc = jnp.exp(s_loc - m)
        p_ctx = jnp.exp(s_ctx - m)
        l = jnp.sum(p_loc, axis=-1, keepdims=True) + jnp.sum(p_ctx, axis=-1, keepdims=True)
        o = _dot(p_loc.astype(BF16), vb) + _dot(p_ctx.astype(BF16), vc)
        o_ref[r * GRID_W:(r + 1) * GRID_W, :] = o / l


def _rpb_table(rpb):
    col = jnp.arange(GRID_W)
    c0 = jnp.clip(col - NA_WIN_C // 2, 0, GRID_W - NA_WIN_C)
    col_mask = (col[None, :] >= c0[:, None]) & (col[None, :] < c0[:, None] + NA_WIN_C)
    dc_idx = jnp.clip(col[None, :] - col[:, None] + NA_WIN_C - 1, 0, 2 * NA_WIN_C - 2)
    pick = (dc_idx[:, :, None] == jnp.arange(2 * NA_WIN_C - 1)[None, None, :]).astype(F32)
    t = jnp.einsum('hdj,qkj->hqdk', rpb, pick, precision=lax.Precision.HIGHEST)
    t = jnp.where(col_mask[None, :, None, :], t, -jnp.inf)
    t = jnp.pad(t, ((0, 0), (0, 0), (0, RPB_SLOTS - t.shape[2]), (0, 0)))
    return t.reshape(NA_HEADS, GRID_W, RPB_SLOTS * GRID_W)


def _neighbourhood_attention(q_all, k_dec, v_dec, kc, vc, rc):
    q_row0 = T_CTX // DEC_SEQ
    hd = NA_HEAD_DIM
    return pl.pallas_call(
        _nbr_attn_kernel,
        grid=(DEC_BATCH, NA_HEADS),
        in_specs=[pl.BlockSpec((DEC_SEQ, hd), lambda b, h: (b + q_row0, h)),
                  pl.BlockSpec((DEC_SEQ, hd), lambda b, h: (b, h)),
                  pl.BlockSpec((DEC_SEQ, hd), lambda b, h: (b, h)),
                  pl.BlockSpec((1, PAST_LEN, hd), lambda b, h: (b, 0, h)),
                  pl.BlockSpec((1, PAST_LEN, hd), lambda b, h: (b, 0, h)),
                  pl.BlockSpec((1, GRID_W, RPB_SLOTS * GRID_W), lambda b, h: (h, 0, 0))],
        out_specs=pl.BlockSpec((DEC_SEQ, hd), lambda b, h: (b, h)),
        out_shape=jax.ShapeDtypeStruct((T_DEC, NA_WIDTH), F32),
        compiler_params=_cparams(("arbitrary", "arbitrary")),
        name="neighbourhood_attention",
    )(q_all, k_dec, v_dec, kc, vc, rc)


def _conv_silu(u_ref, w_ref, b_ref, length):
    u = u_ref[...]
    halo = jnp.zeros((8, u.shape[1]), F32)
    padded = jnp.concatenate([halo, u, halo], axis=0)
    acc = u * w_ref[SSD_CONV // 2:SSD_CONV // 2 + 1, :] + b_ref[...]
    for k in range(SSD_CONV):
        d = k - SSD_CONV // 2
        if d == 0:
            continue
        shifted = pltpu.roll(padded, (-d) % (length + 16), 0)[8:8 + length]
        acc = acc + shifted * w_ref[k:k + 1, :]
    return _silu(acc)


def _ssd_kernel(*refs, length, has_h0, emit_state):
    (xs_ref, b_ref, c_ref, z_ref, dt_ref, dtt_ref, cwx_ref, cwb_ref, cwc_ref,
     cbx_ref, cbb_ref, cbc_ref, dtb_row_ref, dtb_col_ref, al_row_ref, al_col_ref,
     e_ref, d_ref) = refs[:18]
    pos = 18
    if has_h0:
        h0f_ref, h0b_ref = refs[pos:pos + 2]
        pos += 2
    u_ref = refs[pos]
    pos += 1
    if emit_state:
        hf_ref, hb_ref = refs[pos:pos + 2]
        pos += 2
    xs_s, b_s, c_s, y_s, st_s = refs[pos:]

    q = SSD_CHUNK
    nc = length // q
    xs_s[...] = _conv_silu(xs_ref, cwx_ref, cbx_ref, length)
    b_s[...] = _conv_silu(b_ref, cwb_ref, cbb_ref, length)
    c_s[...] = _conv_silu(c_ref, cwc_ref, cbc_ref, length)
    y_s[...] = d_ref[0] * xs_s[...]

    for dirn in range(2):
        if has_h0:
            h0 = (h0f_ref if dirn == 0 else h0b_ref)[0, 0]
            st_s[dirn] = jnp.transpose(h0.reshape(GROUP_W, SSD_STATE))
        else:
            st_s[dirn] = jnp.zeros((SSD_STATE, GROUP_W), F32)

    ri = lax.broadcasted_iota(jnp.int32, (q, q), 0)
    ci = lax.broadcasted_iota(jnp.int32, (q, q), 1)
    lower = ri >= ci
    upper = ri <= ci
    lower_bf = jnp.where(lower, 1.0, 0.0).astype(BF16)
    upper_bf = jnp.where(upper, 1.0, 0.0).astype(BF16)

    def chunk(c, dirn):
        r0 = c * q if isinstance(c, int) else pl.multiple_of(c * q, q)
        tri = lower if dirn == 0 else upper
        xs_c = xs_s[pl.ds(r0, q), :]
        bc = b_s[pl.ds(r0, q), :]
        cc = c_s[pl.ds(r0, q), :]
        dtp = _softplus(dt_ref[pl.ds(r0, q), :] + dtb_row_ref[...])
        da = dtp * (-jnp.exp(al_row_ref[...]))
        pre = _dot_exact_lhs(lower_bf, da)
        cs = pre if dirn == 0 else pre[q - 1:q, :] - pre + da
        e_bf = e_ref[dirn, 0]
        dt_e = _dot_exact_rhs(dtp, e_bf)
        cs_e = _dot_exact_rhs(cs, e_bf)
        dt_t = _softplus(dtt_ref[c, dirn, 0] + dtb_col_ref[dirn, 0])
        da_t = dt_t * (-jnp.exp(al_col_ref[dirn, 0]))
        cs_t = _dot_exact_rhs(da_t, upper_bf if dirn == 0 else lower_bf)

        cb = _dot_nt(cc.astype(BF16), bc.astype(BF16))
        bt = jnp.transpose(bc).astype(BF16)
        end = q - 1 if dirn == 0 else 0
        cs_end = cs_e[end:end + 1, :]
        xdt = xs_c * dt_e
        st = st_s[dirn]
        y = _dot(cc.astype(BF16), st.astype(BF16)) * jnp.exp(cs_e)
        st_s[dirn] = jnp.exp(cs_end) * st + _dot(bt, (xdt * jnp.exp(cs_end - cs_e)).astype(BF16))
        xdt_bf = xdt.astype(BF16)
        parts = []
        for r in range(HEADS_PER_GROUP):
            sl = slice(r * SSD_HEAD_DIM, (r + 1) * SSD_HEAD_DIM)
            diff = cs_e[:, r * SSD_HEAD_DIM:r * SSD_HEAD_DIM + 1] - cs_t[r:r + 1, :]
            lm = jnp.exp(jnp.where(tri, diff, -jnp.inf))
            parts.append(_dot((cb * lm).astype(BF16), xdt_bf[:, sl]))
        y = y + jnp.concatenate(parts, axis=-1)
        y_s[pl.ds(r0, q), :] = y_s[pl.ds(r0, q), :] + y

    if nc <= 2:
        for i in range(nc):
            chunk(i, 0)
            chunk(nc - 1 - i, 1)
    else:
        def both(i, carry):
            chunk(i, 0)
            chunk(nc - 1 - i, 1)
            return carry
        lax.fori_loop(0, nc, both, 0)

    u_ref[...] = y_s[...] * _silu(z_ref[...])
    if emit_state:
        hf_ref[0, 0] = jnp.transpose(st_s[0]).reshape(HEADS_PER_GROUP, SSD_HEAD_DIM, SSD_STATE)
        hb_ref[0, 0] = jnp.transpose(st_s[1]).reshape(HEADS_PER_GROUP, SSD_HEAD_DIM, SSD_STATE)


def _ssd_mixer(zx, dt_all, dt_t, consts, *, length, n_seq, seq_block0, h0=None, emit_state=False):
    (cw, cb, dtb_row, dtb_col, al_row, al_col, expand, d_row) = consts
    g_w, n_s = GROUP_W, SSD_STATE
    xs_cb0 = D_MODEL // g_w
    b_cb0 = (D_MODEL + SSD_WIDTH) // n_s
    c_cb0 = b_cb0 + SSD_GROUPS
    cw_b0 = SSD_WIDTH // n_s
    nck = length // SSD_CHUNK
    in_specs = [
        pl.BlockSpec((length, g_w), lambda s, g: (s + seq_block0, xs_cb0 + g)),
        pl.BlockSpec((length, n_s), lambda s, g: (s + seq_block0, b_cb0 + g)),
        pl.BlockSpec((length, n_s), lambda s, g: (s + seq_block0, c_cb0 + g)),
        pl.BlockSpec((length, g_w), lambda s, g: (s + seq_block0, g)),
        pl.BlockSpec((length, 128), lambda s, g: (s + seq_block0, 0)),
        pl.BlockSpec((nck, 2, 1, HEADS_PER_GROUP, SSD_CHUNK), lambda s, g: (s + seq_block0, 0, g, 0, 0)),
        pl.BlockSpec((SSD_CONV, g_w), lambda s, g: (0, g)),
        pl.BlockSpec((SSD_CONV, n_s), lambda s, g: (0, cw_b0 + g)),
        pl.BlockSpec((SSD_CONV, n_s), lambda s, g: (0, cw_b0 + SSD_GROUPS + g)),
        pl.BlockSpec((1, g_w), lambda s, g: (0, g)),
        pl.BlockSpec((1, n_s), lambda s, g: (0, cw_b0 + g)),
        pl.BlockSpec((1, n_s), lambda s, g: (0, cw_b0 + SSD_GROUPS + g)),
        pl.BlockSpec((1, 128), lambda s, g: (0, 0)),
        pl.BlockSpec((2, 1, HEADS_PER_GROUP, SSD_CHUNK), lambda s, g: (0, g, 0, 0)),
        pl.BlockSpec((1, 128), lambda s, g: (0, 0)),
        pl.BlockSpec((2, 1, HEADS_PER_GROUP, SSD_CHUNK), lambda s, g: (0, g, 0, 0)),
        pl.BlockSpec((2, 1, 128, g_w), lambda s, g: (0, g, 0, 0)),
        pl.BlockSpec((1, 1, g_w), lambda s, g: (g, 0, 0)),
    ]
    args = [zx, zx, zx, zx, dt_all, dt_t, cw, cw, cw, cb, cb, cb,
            dtb_row, dtb_col, al_row, al_col, expand, d_row]
    st_spec = pl.BlockSpec((1, 1, HEADS_PER_GROUP, SSD_HEAD_DIM, n_s), lambda s, g: (s, g, 0, 0, 0))
    if h0 is not None:
        in_specs += [st_spec, st_spec]
        args += list(h0)
    out_specs = [pl.BlockSpec((length, g_w), lambda s, g: (s, g))]
    out_shape = [jax.ShapeDtypeStruct((n_seq * length, SSD_WIDTH), F32)]
    if emit_state:
        st_shape = jax.ShapeDtypeStruct((n_seq, SSD_GROUPS, HEADS_PER_GROUP, SSD_HEAD_DIM, n_s), F32)
        out_specs += [st_spec, st_spec]
        out_shape += [st_shape, st_shape]
    return pl.pallas_call(
        functools.partial(_ssd_kernel, length=length, has_h0=h0 is not None, emit_state=emit_state),
        grid=(n_seq, SSD_GROUPS),
        in_specs=in_specs,
        out_specs=out_specs,
        out_shape=out_shape,
        scratch_shapes=[pltpu.VMEM((length, g_w), F32), pltpu.VMEM((length, n_s), F32),
                        pltpu.VMEM((length, n_s), F32), pltpu.VMEM((length, g_w), F32),
                        pltpu.VMEM((2, n_s, g_w), F32)],
        compiler_params=_cparams(("arbitrary", "arbitrary")),
        name="ssd_mixer_%d" % length,
    )(*args)


def _ssd_consts(conv_w, conv_b, dt_bias, a_log, d_skip):
    hpg = HEADS_PER_GROUP
    dtb_row = jnp.pad(dt_bias.reshape(1, 2 * SSD_HEADS), ((0, 0), (0, 128 - 2 * SSD_HEADS)))
    col = lambda p: jnp.broadcast_to(p.reshape(2, SSD_GROUPS, hpg, 1), (2, SSD_GROUPS, hpg, SSD_CHUNK))
    al_row = jnp.pad(a_log.reshape(1, 2 * SSD_HEADS), ((0, 0), (0, 128 - 2 * SSD_HEADS)))
    src = (jnp.arange(2)[:, None, None] * SSD_HEADS + jnp.arange(SSD_GROUPS)[None, :, None] * hpg
           + jnp.arange(GROUP_W)[None, None, :] // SSD_HEAD_DIM)
    expand = (jnp.arange(128)[None, None, :, None] == src[:, :, None, :]).astype(BF16)
    d_row = jnp.repeat(d_skip.reshape(SSD_GROUPS, 1, hpg), SSD_HEAD_DIM, axis=-1)
    return (conv_w, conv_b.reshape(1, SSD_CONV_CH), dtb_row, col(dt_bias), al_row, col(a_log), expand, d_row)


def _merge_kernel(nac_ref, nad_ref, uc_ref, ud_ref, nw_ref, w1_ref, w2_ref, g_na_ref, g_ssd_ref,
                  o_ref, *, n_ctx):
    is_ctx = pl.program_id(0) < n_ctx
    a1 = jnp.where(is_ctx, nac_ref[...], nad_ref[...]).astype(BF16)
    u = jnp.where(is_ctx, uc_ref[...], ud_ref[...])
    r = lax.rsqrt(jnp.mean(u * u, axis=-1, keepdims=True) + LN_EPS)
    a2 = (u * r * nw_ref[...]).astype(BF16)
    o = (jax.nn.sigmoid(g_na_ref[...]) * _dot(a1, w1_ref[...])
         + jax.nn.sigmoid(g_ssd_ref[...]) * _dot(a2, w2_ref[...]))
    o_ref[...] = o.astype(o_ref.dtype)


def _merge_branches(na_ctx, na_dec, u_ctx, u_dec, norm_w, w_na_out, w_ssd_out, gates):
    tm = 256
    n_ctx = T_CTX // tm
    ctx_rows = lambda i: (jnp.minimum(i, n_ctx - 1), 0)
    dec_rows = lambda i: (jnp.maximum(i - n_ctx, 0), 0)
    const = lambda i: (0, 0)
    resident = pl.Buffered(1)
    return pl.pallas_call(
        functools.partial(_merge_kernel, n_ctx=n_ctx),
        grid=(T_ALL // tm,),
        in_specs=[pl.BlockSpec((tm, D_MODEL), ctx_rows), pl.BlockSpec((tm, D_MODEL), dec_rows),
                  pl.BlockSpec((tm, D_MODEL), ctx_rows), pl.BlockSpec((tm, D_MODEL), dec_rows),
                  pl.BlockSpec((1, D_MODEL), const),
                  pl.BlockSpec((D_MODEL, D_MODEL), const, pipeline_mode=resident),
                  pl.BlockSpec((D_MODEL, D_MODEL), const, pipeline_mode=resident),
                  pl.BlockSpec((tm, D_MODEL), lambda i: (i, 1)),
                  pl.BlockSpec((tm, D_MODEL), lambda i: (i, 0))],
        out_specs=pl.BlockSpec((tm, D_MODEL), lambda i: (i, 0)),
        out_shape=jax.ShapeDtypeStruct((T_ALL, D_MODEL), BF16),
        compiler_params=_cparams(("arbitrary",)),
        name="merge_branches",
    )(na_ctx, na_dec, u_ctx, u_dec, norm_w, w_na_out, w_ssd_out, gates, gates)


def _post_mix_kernel(y_ref, wo_ref, xc_ref, xd_ref, m_ref, g_ref, b_ref, wr_ref, br_ref,
                     x1_ref, h2_ref, idx_ref, wgt_ref, *, n_ctx):
    i = pl.program_id(0)
    x = jnp.where(i < n_ctx, xc_ref[...], xd_ref[...])
    mix = _dot(y_ref[...], wo_ref[...])
    x1 = _layer_norm(DN_ALPHA * x + m_ref[0, 2:3, :] * mix) * g_ref[...] + b_ref[...]
    x1_ref[...] = x1
    h2 = _layer_norm(x1) * (1.0 + m_ref[0, 4:5, :]) + m_ref[0, 3:4, :]
    h2_ref[...] = h2
    logits = _dot_nt_f32(wr_ref[...], h2) + br_ref[...]
    eidx = lax.broadcasted_iota(jnp.int32, logits.shape, 0)
    vals, idxs = [], []
    for _ in range(TOP_K):
        m = jnp.max(logits, axis=0, keepdims=True)
        sel = jnp.min(jnp.where(logits == m, eidx, N_EXPERTS), axis=0, keepdims=True)
        logits = jnp.where(eidx == sel, -jnp.inf, logits)
        vals.append(m)
        idxs.append(sel)
    ex = [jnp.exp(v - vals[0]) for v in vals]
    tot = ex[0] + ex[1] + ex[2] + ex[3]
    idx_ref[...] = jnp.concatenate(idxs, axis=0)
    wgt_ref[...] = jnp.concatenate([e / tot for e in ex], axis=0)


def _post_mix(y, w_o, x_ctx, x_dec, mods, ln_g, ln_b, w_router_t, b_router):
    tm = 512
    n_ctx = T_CTX // tm
    row = lambda i: (i, 0)
    const = lambda i: (0, 0)
    return pl.pallas_call(
        functools.partial(_post_mix_kernel, n_ctx=n_ctx),
        grid=(T_ALL // tm,),
        in_specs=[pl.BlockSpec((tm, D_MODEL), row),
                  pl.BlockSpec((D_MODEL, D_MODEL), const, pipeline_mode=pl.Buffered(1)),
                  pl.BlockSpec((tm, D_MODEL), lambda i: (jnp.minimum(i, n_ctx - 1), 0)),
                  pl.BlockSpec((tm, D_MODEL), lambda i: (jnp.maximum(i - n_ctx, 0), 0)),
                  pl.BlockSpec((1, 6, D_MODEL), lambda i: (_mod_row(i, tm), 0, 0)),
                  pl.BlockSpec((1, D_MODEL), const), pl.BlockSpec((1, D_MODEL), const),
                  pl.BlockSpec((N_EXPERTS, D_MODEL), const), pl.BlockSpec((N_EXPERTS, 1), const)],
        out_specs=[pl.BlockSpec((tm, D_MODEL), row), pl.BlockSpec((tm, D_MODEL), row),
                   pl.BlockSpec((TOP_K, tm), lambda i: (0, i)), pl.BlockSpec((TOP_K, tm), lambda i: (0, i))],
        out_shape=[jax.ShapeDtypeStruct((T_ALL, D_MODEL), F32), jax.ShapeDtypeStruct((T_ALL, D_MODEL), F32),
                   jax.ShapeDtypeStruct((TOP_K, T_ALL), jnp.int32), jax.ShapeDtypeStruct((TOP_K, T_ALL), F32)],
        compiler_params=_cparams(("arbitrary",)),
        name="post_mix_router",
    )(y, w_o, x_ctx, x_dec, mods, ln_g, ln_b, w_router_t, b_router)


def _start_rows(src_hbm, idx_ref, idx_lead, dst, sem, count):
    for r in range(count):
        pltpu.make_async_copy(src_hbm.at[pl.ds(idx_ref[idx_lead + (r,)], 1)], dst.at[pl.ds(r, 1)], sem).start()


def _wait_rows(src_hbm, dst, sem, count):
    for _ in range(count):
        pltpu.make_async_copy(src_hbm.at[pl.ds(0, 1)], dst.at[pl.ds(0, 1)], sem).wait()


def _dispatch_kernel(nv_ref, tok_ref, tok_next_ref, x_hbm, o_ref, buf, sem):
    t = pl.program_id(0)
    nv = nv_ref[0]
    slot = t % 2

    @pl.when(t == 0)
    def _():
        _start_rows(x_hbm, tok_ref, (0, 0), buf.at[0], sem.at[0], MOE_TM)

    @pl.when(t + 1 < nv)
    def _():
        _start_rows(x_hbm, tok_next_ref, (0, 0), buf.at[1 - slot], sem.at[1 - slot], MOE_TM)

    @pl.when(t < nv)
    def _():
        _wait_rows(x_hbm, buf.at[slot], sem.at[slot], MOE_TM)
        o_ref[...] = buf[slot].astype(o_ref.dtype)

    @pl.when(t >= nv)
    def _():
        o_ref[...] = jnp.zeros(o_ref.shape, o_ref.dtype)


def _dispatch(h2, row_token, n_valid):
    tok_spec = lambda f: pl.BlockSpec((1, 1, MOE_TM), f, memory_space=pltpu.SMEM)
    grid_spec = pltpu.PrefetchScalarGridSpec(
        num_scalar_prefetch=1,
        grid=(MOE_TILES,),
        in_specs=[tok_spec(lambda t, nv: (t, 0, 0)),
                  tok_spec(lambda t, nv: (jnp.minimum(t + 1, MOE_TILES - 1), 0, 0)),
                  pl.BlockSpec(memory_space=pl.ANY)],
        out_specs=pl.BlockSpec((MOE_TM, D_MODEL), lambda t, nv: (t, 0)),
        scratch_shapes=[pltpu.VMEM((2, MOE_TM, D_MODEL), F32), pltpu.SemaphoreType.DMA((2,))],
    )
    return pl.pallas_call(
        _dispatch_kernel,
        grid_spec=grid_spec,
        out_shape=jax.ShapeDtypeStruct((MOE_ROWS, D_MODEL), BF16),
        compiler_params=_cparams(("arbitrary",)),
        name="moe_dispatch",
    )(n_valid, row_token, row_token, h2)


def _gate_up_kernel(te_ref, nv_ref, x_ref, wg_ref, wu_ref, bg_ref, bu_ref, o_ref, wg_s, wu_s):
    t = pl.program_id(1)
    prev = te_ref[jnp.maximum(t - 1, 0)]

    @pl.when((t == 0) | (te_ref[t] != prev))
    def _():
        wg_s[...] = wg_ref[0].astype(BF16)
        wu_s[...] = wu_ref[0].astype(BF16)

    @pl.when(t < nv_ref[0])
    def _():
        x = x_ref[...]
        gate = jnp.minimum(_dot(x, wg_s[...]) + bg_ref[0], SWIGLU_LIMIT)
        up = jnp.clip(_dot(x, wu_s[...]) + bu_ref[0], -SWIGLU_LIMIT, SWIGLU_LIMIT)
        hid = (up + 1.0) * gate * jax.nn.sigmoid(SWIGLU_ALPHA * gate)
        o_ref[...] = hid.astype(o_ref.dtype)

    @pl.when(t >= nv_ref[0])
    def _():
        o_ref[...] = jnp.zeros(o_ref.shape, o_ref.dtype)


def _moe_gate_up(xs, tile_expert, n_valid, w_gate_up, b_gate_up):
    tn = 1024
    nj = D_EXPERT // tn
    row = lambda j, t, te, nv: (jnp.minimum(t, nv[0] - 1), 0)
    grid_spec = pltpu.PrefetchScalarGridSpec(
        num_scalar_prefetch=2,
        grid=(nj, MOE_TILES),
        in_specs=[pl.BlockSpec((MOE_TM, D_MODEL), row),
                  pl.BlockSpec((1, D_MODEL, tn), lambda j, t, te, nv: (te[t], 0, j)),
                  pl.BlockSpec((1, D_MODEL, tn), lambda j, t, te, nv: (te[t], 0, nj + j)),
                  pl.BlockSpec((1, 1, tn), lambda j, t, te, nv: (te[t], 0, j)),
                  pl.BlockSpec((1, 1, tn), lambda j, t, te, nv: (te[t], 0, nj + j))],
        out_specs=pl.BlockSpec((MOE_TM, tn), lambda j, t, te, nv: (t, j)),
        scratch_shapes=[pltpu.VMEM((D_MODEL, tn), BF16), pltpu.VMEM((D_MODEL, tn), BF16)],
    )
    return pl.pallas_call(
        _gate_up_kernel,
        grid_spec=grid_spec,
        out_shape=jax.ShapeDtypeStruct((MOE_ROWS, D_EXPERT), BF16),
        compiler_params=_cparams(("arbitrary", "arbitrary")),
        name="moe_gate_up",
    )(tile_expert, n_valid, xs, w_gate_up, w_gate_up, b_gate_up, b_gate_up)


def _down_kernel(te_ref, nv_ref, h_ref, w_ref, b_ref, o_ref, w_s):
    t = pl.program_id(1)
    prev = te_ref[jnp.maximum(t - 1, 0)]

    @pl.when((t == 0) | (te_ref[t] != prev))
    def _():
        w_s[...] = w_ref[0].astype(BF16)

    @pl.when(t < nv_ref[0])
    def _():
        o_ref[...] = _dot(h_ref[...], w_s[...]) + b_ref[0]

    @pl.when(t >= nv_ref[0])
    def _():
        o_ref[...] = jnp.zeros(o_ref.shape, o_ref.dtype)


def _moe_down(hid, tile_expert, n_valid, w_down, b_down):
    tn = 1024
    grid_spec = pltpu.PrefetchScalarGridSpec(
        num_scalar_prefetch=2,
        grid=(D_MODEL // tn, MOE_TILES),
        in_specs=[pl.BlockSpec((MOE_TM, D_EXPERT), lambda j, t, te, nv: (jnp.minimum(t, nv[0] - 1), 0)),
                  pl.BlockSpec((1, D_EXPERT, tn), lambda j, t, te, nv: (te[t], 0, j)),
                  pl.BlockSpec((1, 1, tn), lambda j, t, te, nv: (te[t], 0, j))],
        out_specs=pl.BlockSpec((MOE_TM, tn), lambda j, t, te, nv: (t, j)),
        scratch_shapes=[pltpu.VMEM((D_EXPERT, tn), BF16)],
    )
    return pl.pallas_call(
        _down_kernel,
        grid_spec=grid_spec,
        out_shape=jax.ShapeDtypeStruct((MOE_ROWS, D_MODEL), F32),
        compiler_params=_cparams(("arbitrary", "arbitrary")),
        name="moe_down",
    )(tile_expert, n_valid, hid, w_down, b_down)


COMBINE_TOK = 64


def _combine_kernel(pos_ref, pos_next_ref, ys_hbm, wgt_ref, x1_ref, m_ref, g_ref, b_ref, o_ref, buf, sem,
                    *, n_steps):
    i = pl.program_id(0)
    slot = i % 2

    def start(p_ref, s):
        for k in range(TOP_K):
            _start_rows(ys_hbm, p_ref, (0, k), buf.at[s, k], sem.at[s], COMBINE_TOK)

    @pl.when(i == 0)
    def _():
        start(pos_ref, 0)

    @pl.when(i + 1 < n_steps)
    def _():
        start(pos_next_ref, 1 - slot)

    _wait_rows(ys_hbm, buf.at[slot, 0], sem.at[slot], TOP_K * COMBINE_TOK)
    w = wgt_ref[...]
    ffn = w[:, 0:1] * buf[slot, 0]
    for k in range(1, TOP_K):
        ffn = ffn + w[:, k:k + 1] * buf[slot, k]
    x2 = _layer_norm(DN_ALPHA * x1_ref[...] + m_ref[0, 5:6, :] * ffn) * g_ref[...] + b_ref[...]
    o_ref[...] = x2


def _combine(ys, pos_blocks, wgt, x1, mods, ln_g, ln_b, *, rows, row0):
    tm = COMBINE_TOK
    b0 = row0 // tm
    n = rows // tm
    const = lambda i: (0, 0)
    pos_spec = lambda f: pl.BlockSpec((1, TOP_K, tm), f, memory_space=pltpu.SMEM)
    return pl.pallas_call(
        functools.partial(_combine_kernel, n_steps=n),
        grid=(n,),
        in_specs=[pos_spec(lambda i: (i + b0, 0, 0)),
                  pos_spec(lambda i: (jnp.minimum(i + 1, n - 1) + b0, 0, 0)),
                  pl.BlockSpec(memory_space=pl.ANY),
                  pl.BlockSpec((tm, TOP_K), lambda i: (i + b0, 0)),
                  pl.BlockSpec((tm, D_MODEL), lambda i: (i + b0, 0)),
                  pl.BlockSpec((1, 6, D_MODEL), lambda i: (_mod_row(i + b0, tm), 0, 0)),
                  pl.BlockSpec((1, D_MODEL), const), pl.BlockSpec((1, D_MODEL), const)],
        out_specs=pl.BlockSpec((tm, D_MODEL), lambda i: (i, 0)),
        out_shape=jax.ShapeDtypeStruct((rows, D_MODEL), F32),
        scratch_shapes=[pltpu.VMEM((2, TOP_K, tm, D_MODEL), F32), pltpu.SemaphoreType.DMA((2,))],
        compiler_params=_cparams(("arbitrary",)),
        name="moe_combine_ln2",
    )(pos_blocks, pos_blocks, ys, wgt, x1, mods, ln_g, ln_b)


def _moe_plan(idx_t):
    flat = idx_t.reshape(-1)
    onehot = (flat[:, None] == jnp.arange(N_EXPERTS)[None, :]).astype(jnp.int32)
    rank = jnp.take_along_axis(jnp.cumsum(onehot, axis=0), flat[:, None], axis=1)[:, 0] - 1
    counts = jnp.sum(onehot, axis=0)
    tiles = (counts + MOE_TM - 1) // MOE_TM
    tile_end = jnp.cumsum(tiles)
    tile_start = tile_end - tiles
    pos = tile_start[flat] * MOE_TM + rank
    token = jnp.tile(jnp.arange(T_ALL, dtype=jnp.int32), TOP_K)
    row_token = jnp.zeros((MOE_ROWS,), jnp.int32).at[pos].set(token, unique_indices=True)
    n_valid = tile_end[-1]
    tile_ids = jnp.minimum(jnp.arange(MOE_TILES), n_valid - 1)
    tile_expert = jnp.sum((tile_end[None, :] <= tile_ids[:, None]).astype(jnp.int32), axis=1)
    return (pos.reshape(TOP_K, T_ALL).astype(jnp.int32), row_token.reshape(MOE_TILES, 1, MOE_TM),
            tile_expert.astype(jnp.int32), n_valid.reshape(1).astype(jnp.int32))


def kernel(x_prompt, x_sample, cache_na_k, cache_na_v, state_ssd_fwd, state_ssd_bwd, c, c_ctx, w_mod, b_mod, w_in, ssd_conv_w, ssd_conv_b, ssd_dt_bias, ssd_a_log, ssd_d, ssd_norm_w, na_rpb, w_ssd_out, w_na_out, w_o, ln1_g, ln1_b, ln2_g, ln2_b, w_router, b_router, w_gate_up, b_gate_up, w_down, b_down):
    assert w_mod.shape[0] == 1, "single-layer trunk"
    x_ctx = x_prompt.reshape(T_CTX, D_MODEL)
    x_dec = x_sample.reshape(T_DEC, D_MODEL)

    cvec = jnp.concatenate([c_ctx[None], c, jnp.zeros((8 - 1 - DEC_BATCH, D_MODEL), F32)], axis=0)
    mods = _modulation(cvec, w_mod[0], b_mod[0])[:1 + DEC_BATCH].reshape(1 + DEC_BATCH, 6, D_MODEL)

    h = _ln_modulate(x_ctx, x_dec, mods)
    w = w_in[0]
    n_main = 3 * NA_WIDTH + SSD_WIDTH + SSD_CONV_CH
    w_bf = w.astype(BF16)
    w_dt = jnp.pad(w[:, n_main:n_main + 2 * SSD_HEADS], ((0, 0), (0, 128 - 2 * SSD_HEADS))).astype(BF16)
    w_gates = w_bf[:, n_main + 2 * SSD_HEADS:]
    tm = 1024
    ctx_blocks = T_CTX // tm
    q_all = _matmul(h, w_bf, rows=T_ALL, col_block0=0, n_out=NA_WIDTH, name="proj_q")
    k_ctx = _matmul(h, w_bf, rows=T_CTX, col_block0=2, n_out=NA_WIDTH, name="proj_k_ctx")
    k_dec = _matmul(h, w_bf, rows=T_DEC, row_block0=ctx_blocks, col_block0=2, n_out=NA_WIDTH, name="proj_k_dec")
    v_ctx = _matmul(h, w_bf, rows=T_CTX, col_block0=4, n_out=NA_WIDTH, name="proj_v_ctx")
    v_dec = _matmul(h, w_bf, rows=T_DEC, row_block0=ctx_blocks, col_block0=4, n_out=NA_WIDTH, name="proj_v_dec")
    zx = _matmul(h, w_bf, rows=T_ALL, col_block0=6, n_out=SSD_WIDTH + SSD_CONV_CH, name="proj_zxbc")
    gates = _matmul(h, w_gates, rows=T_ALL, name="proj_gates")
    dt_all = _matmul(h, w_dt, rows=T_ALL, tn=128, name="proj_dt")

    na_ctx = _context_attention(q_all, k_ctx, v_ctx)
    kc = cache_na_k[:, 0].reshape(DEC_BATCH, PAST_LEN, NA_WIDTH)
    vc = cache_na_v[:, 0].reshape(DEC_BATCH, PAST_LEN, NA_WIDTH)
    na_dec = _neighbourhood_attention(q_all, k_dec, v_dec, kc, vc, _rpb_table(na_rpb[0]))

    consts = _ssd_consts(ssd_conv_w[0], ssd_conv_b[0], ssd_dt_bias[0], ssd_a_log[0], ssd_d[0])
    dt_t = dt_all[:, :2 * SSD_HEADS].reshape(T_ALL // SSD_CHUNK, SSD_CHUNK, 2, SSD_GROUPS, HEADS_PER_GROUP)
    dt_t = jnp.transpose(dt_t, (0, 2, 3, 4, 1))
    u_ctx, h_f, h_b = _ssd_mixer(zx, dt_all, dt_t, consts, length=SEQ, n_seq=BATCH, seq_block0=0,
                                 emit_state=True)
    h0 = (state_ssd_fwd[:, 0].reshape(DEC_BATCH, SSD_GROUPS, HEADS_PER_GROUP, SSD_HEAD_DIM, SSD_STATE),
          state_ssd_bwd[:, 0].reshape(DEC_BATCH, SSD_GROUPS, HEADS_PER_GROUP, SSD_HEAD_DIM, SSD_STATE))
    (u_dec,) = _ssd_mixer(zx, dt_all, dt_t, consts, length=DEC_SEQ, n_seq=DEC_BATCH,
                          seq_block0=T_CTX // DEC_SEQ, h0=h0)

    y = _merge_branches(na_ctx, na_dec, u_ctx, u_dec, ssd_norm_w[0].reshape(1, SSD_WIDTH),
                        w_na_out[0].astype(BF16), w_ssd_out[0].astype(BF16), gates)
    x1, h2, idx_t, wgt_t = _post_mix(y, w_o[0].astype(BF16), x_ctx, x_dec, mods,
                                     ln1_g[0].reshape(1, D_MODEL), ln1_b[0].reshape(1, D_MODEL),
                                     jnp.transpose(w_router[0]), b_router[0].reshape(N_EXPERTS, 1))

    pos, row_token, tile_expert, n_valid = _moe_plan(idx_t)
    xs = _dispatch(h2, row_token, n_valid)
    hid = _moe_gate_up(xs, tile_expert, n_valid, w_gate_up[0], b_gate_up[0].reshape(N_EXPERTS, 1, 2 * D_EXPERT))
    ys = _moe_down(hid, tile_expert, n_valid, w_down[0], b_down[0].reshape(N_EXPERTS, 1, D_MODEL))
    pos_c = jnp.transpose(pos.reshape(TOP_K, T_ALL // COMBINE_TOK, COMBINE_TOK), (1, 0, 2))
    wgt = jnp.transpose(wgt_t)
    g2, b2 = ln2_g[0].reshape(1, D_MODEL), ln2_b[0].reshape(1, D_MODEL)
    y_ctx = _combine(ys, pos_c, wgt, x1, mods, g2, b2, rows=T_CTX, row0=0)
    y_dec = _combine(ys, pos_c, wgt, x1, mods, g2, b2, rows=T_DEC, row0=T_CTX)

    return (y_ctx.reshape(BATCH, SEQ, D_MODEL),
            y_dec.reshape(DEC_BATCH, DEC_SEQ, D_MODEL),
            k_ctx.reshape(BATCH, 1, SEQ, NA_HEADS, NA_HEAD_DIM),
            v_ctx.reshape(BATCH, 1, SEQ, NA_HEADS, NA_HEAD_DIM),
            h_f.reshape(BATCH, 1, SSD_HEADS, SSD_HEAD_DIM, SSD_STATE),
            h_b.reshape(BATCH, 1, SSD_HEADS, SSD_HEAD_DIM, SSD_STATE))
```

```python
import functools

import jax
import jax.numpy as jnp
from jax import lax
from jax.experimental import pallas as pl
from jax.experimental.pallas import tpu as pltpu

F32 = jnp.float32
BF16 = jnp.bfloat16

D_MODEL = 2048
BATCH = 32
SEQ = 256
DEC_BATCH = 2
DEC_SEQ = 1024
PAST_LEN = 512
GRID_W = 64
NA_HEADS = 16
NA_HEAD_DIM = 128
NA_WIDTH = NA_HEADS * NA_HEAD_DIM
NA_WIN_R = 8
NA_WIN_C = 16
SSD_HEADS = 32
SSD_HEAD_DIM = 64
SSD_WIDTH = SSD_HEADS * SSD_HEAD_DIM
SSD_GROUPS = 4
SSD_STATE = 128
SSD_CONV = 5
SSD_CHUNK = 128
SSD_CONV_CH = SSD_WIDTH + 2 * SSD_GROUPS * SSD_STATE
N_EXPERTS = 32
TOP_K = 4
D_EXPERT = 2048
SWIGLU_LIMIT = 7.0
SWIGLU_ALPHA = 1.702
DN_ALPHA = 2.0 ** 0.25
LN_EPS = 1e-5

T_CTX = BATCH * SEQ
T_DEC = DEC_BATCH * DEC_SEQ
T_ALL = T_CTX + T_DEC
HEADS_PER_GROUP = SSD_HEADS // SSD_GROUPS
GROUP_W = HEADS_PER_GROUP * SSD_HEAD_DIM
ROWS = DEC_SEQ // GRID_W
RPB_SLOTS = 16

MOE_TM = 256
MOE_TILES = T_ALL * TOP_K // MOE_TM + N_EXPERTS
MOE_ROWS = MOE_TILES * MOE_TM
W_SPLIT = 8

VMEM_LIMIT = 56 * 1024 * 1024


def _cparams(sem):
    return pltpu.CompilerParams(dimension_semantics=sem, vmem_limit_bytes=VMEM_LIMIT)


def _split3(x):
    hi = x.astype(BF16)
    r1 = x - hi.astype(F32)
    mid = r1.astype(BF16)
    lo = (r1 - mid.astype(F32)).astype(BF16)
    return hi, mid, lo


def _dot(a, b):
    return jnp.dot(a, b, preferred_element_type=F32)


def _dot_nt(a, b):
    return lax.dot_general(a, b, (((1,), (1,)), ((), ())), preferred_element_type=F32)


def _dot_exact_rhs(x, m_bf):
    hi, mid, lo = _split3(x)
    return _dot(hi, m_bf) + (_dot(mid, m_bf) + _dot(lo, m_bf))


def _dot_exact_lhs(m_bf, x):
    hi, mid, lo = _split3(x)
    return _dot(m_bf, hi) + (_dot(m_bf, mid) + _dot(m_bf, lo))


def _dot_nt_f32(a, b):
    ah, am, al = _split3(a)
    bh, bm, bl = _split3(b)
    small = _dot_nt(ah, bl) + _dot_nt(am, bm) + _dot_nt(al, bh)
    mid = _dot_nt(ah, bm) + _dot_nt(am, bh)
    return _dot_nt(ah, bh) + (mid + small)


def _dot_f32(a, b):
    ah, am, al = _split3(a)
    bh, bm, bl = _split3(b)
    small = _dot(ah, bl) + _dot(am, bm) + _dot(al, bh)
    mid = _dot(ah, bm) + _dot(am, bh)
    return _dot(ah, bh) + (mid + small)


def _silu(x):
    return x * jax.nn.sigmoid(x)


def _softplus(x):
    return jnp.maximum(x, 0.0) + jnp.log1p(jnp.exp(-jnp.abs(x)))


def _layer_norm(x):
    mu = jnp.mean(x, axis=-1, keepdims=True)
    xc = x - mu
    var = jnp.mean(xc * xc, axis=-1, keepdims=True)
    return xc * lax.rsqrt(var + LN_EPS)


def _mod_row(i, tm):
    n_ctx = T_CTX // tm
    per_b = DEC_SEQ // tm
    return jnp.where(i < n_ctx, 0, 1 + (i - n_ctx) // per_b)


def _mod_kernel(c_ref, w_ref, b_ref, o_ref):
    o_ref[...] = _dot_f32(_silu(c_ref[...]), w_ref[...]) + b_ref[...]


def _modulation(cvec, w_mod, b_mod):
    tn = 1024
    n = w_mod.shape[1]
    return pl.pallas_call(
        _mod_kernel,
        grid=(n // tn,),
        in_specs=[pl.BlockSpec((8, D_MODEL), lambda j: (0, 0)),
                  pl.BlockSpec((D_MODEL, tn), lambda j: (0, j)),
                  pl.BlockSpec((1, tn), lambda j: (0, j))],
        out_specs=pl.BlockSpec((8, tn), lambda j: (0, j)),
        out_shape=jax.ShapeDtypeStruct((8, n), F32),
        compiler_params=_cparams(("arbitrary",)),
        name="modulation",
    )(cvec, w_mod, b_mod.reshape(1, n))


def _ln_mod_kernel(xc_ref, xd_ref, m_ref, o_ref, *, n_ctx):
    i = pl.program_id(0)
    x = jnp.where(i < n_ctx, xc_ref[...], xd_ref[...])
    y = _layer_norm(x) * (1.0 + m_ref[0, 1:2, :]) + m_ref[0, 0:1, :]
    o_ref[...] = y.astype(o_ref.dtype)


def _ln_modulate(x_ctx, x_dec, mods):
    tm = 512
    n_ctx = T_CTX // tm
    return pl.pallas_call(
        functools.partial(_ln_mod_kernel, n_ctx=n_ctx),
        grid=(T_ALL // tm,),
        in_specs=[pl.BlockSpec((tm, D_MODEL), lambda i: (jnp.minimum(i, n_ctx - 1), 0)),
                  pl.BlockSpec((tm, D_MODEL), lambda i: (jnp.maximum(i - n_ctx, 0), 0)),
                  pl.BlockSpec((1, 6, D_MODEL), lambda i: (_mod_row(i, tm), 0, 0))],
        out_specs=pl.BlockSpec((tm, D_MODEL), lambda i: (i, 0)),
        out_shape=jax.ShapeDtypeStruct((T_ALL, D_MODEL), BF16),
        compiler_params=_cparams(("arbitrary",)),
        name="ln_modulate",
    )(x_ctx, x_dec, mods)


def _mm_kernel(x_ref, w_ref, o_ref):
    o_ref[...] = _dot(x_ref[...], w_ref[...]).astype(o_ref.dtype)


def _matmul(x, w, *, rows, row_block0=0, col_block0=0, n_out=None, tm=1024, tn=1024,
            out_dtype=F32, name="matmul"):
    k = x.shape[1]
    n_out = w.shape[1] if n_out is None else n_out
    return pl.pallas_call(
        _mm_kernel,
        grid=(rows // tm, n_out // tn),
        in_specs=[pl.BlockSpec((tm, k), lambda i, j: (i + row_block0, 0)),
                  pl.BlockSpec((k, tn), lambda i, j: (0, j + col_block0))],
        out_specs=pl.BlockSpec((tm, tn), lambda i, j: (i, j)),
        out_shape=jax.ShapeDtypeStruct((rows, n_out), out_dtype),
        compiler_params=_cparams(("arbitrary", "arbitrary")),
        name=name,
    )(x, w)


def _ctx_attn_kernel(q_ref, k_ref, v_ref, o_ref):
    scale = NA_HEAD_DIM ** -0.5
    for h in range(NA_HEADS):
        sl = slice(h * NA_HEAD_DIM, (h + 1) * NA_HEAD_DIM)
        q = q_ref[:, sl].astype(BF16)
        k = k_ref[:, sl].astype(BF16)
        v = v_ref[:, sl].astype(BF16)
        s = _dot_nt(q, k) * scale
        m = jnp.max(s, axis=-1, keepdims=True)
        p = jnp.exp(s - m)
        l = jnp.sum(p, axis=-1, keepdims=True)
        o_ref[:, sl] = _dot(p.astype(BF16), v) / l


def _context_attention(q_all, k_ctx, v_ctx):
    spec = pl.BlockSpec((SEQ, NA_WIDTH), lambda b: (b, 0))
    return pl.pallas_call(
        _ctx_attn_kernel,
        grid=(BATCH,),
        in_specs=[spec, spec, spec],
        out_specs=spec,
        out_shape=jax.ShapeDtypeStruct((T_CTX, NA_WIDTH), F32),
        compiler_params=_cparams(("arbitrary",)),
        name="context_attention",
    )(q_all, k_ctx, v_ctx)


def _nbr_attn_kernel(q_ref, k_ref, v_ref, kc_ref, vc_ref, rc_ref, o_ref):
    scale = NA_HEAD_DIM ** -0.5
    kr = min(NA_WIN_R, ROWS)
    kc = kc_ref[0].astype(BF16)
    vc = vc_ref[0].astype(BF16)
    for r in range(ROWS):
        r0 = min(max(r - kr // 2, 0), ROWS - kr)
        off = (r0 - r + NA_WIN_R - 1) * GRID_W
        q = q_ref[r * GRID_W:(r + 1) * GRID_W, :].astype(BF16)
        kb = k_ref[r0 * GRID_W:(r0 + kr) * GRID_W, :].astype(BF16)
        vb = v_ref[r0 * GRID_W:(r0 + kr) * GRID_W, :].astype(BF16)
        s_loc = _dot_nt(q, kb) * scale + rc_ref[0, :, off:off + kr * GRID_W]
        s_ctx = _dot_nt(q, kc) * scale
        m = jnp.maximum(jnp.max(s_loc, axis=-1, keepdims=True),
                        jnp.max(s_ctx, axis=-1, keepdims=True))
        p_loc = jnp.exp(s_loc - m)
        p_ctx = jnp.exp(s_ctx - m)
        l = jnp.sum(p_loc, axis=-1, keepdims=True) + jnp.sum(p_ctx, axis=-1, keepdims=True)
        o = _dot(p_loc.astype(BF16), vb) + _dot(p_ctx.astype(BF16), vc)
        o_ref[r * GRID_W:(r + 1) * GRID_W, :] = o / l


def _rpb_table(rpb):
    col = jnp.arange(GRID_W)
    c0 = jnp.clip(col - NA_WIN_C // 2, 0, GRID_W - NA_WIN_C)
    col_mask = (col[None, :] >= c0[:, None]) & (col[None, :] < c0[:, None] + NA_WIN_C)
    dc_idx = jnp.clip(col[None, :] - col[:, None] + NA_WIN_C - 1, 0, 2 * NA_WIN_C - 2)
    pick = (dc_idx[:, :, None] == jnp.arange(2 * NA_WIN_C - 1)[None, None, :]).astype(F32)
    t = jnp.einsum('hdj,qkj->hqdk', rpb, pick, precision=lax.Precision.HIGHEST)
    t = jnp.where(col_mask[None, :, None, :], t, -jnp.inf)
    t = jnp.pad(t, ((0, 0), (0, 0), (0, RPB_SLOTS - t.shape[2]), (0, 0)))
    return t.reshape(NA_HEADS, GRID_W, RPB_SLOTS * GRID_W)


def _neighbourhood_attention(q_all, k_dec, v_dec, kc, vc, rc):
    q_row0 = T_CTX // DEC_SEQ
    hd = NA_HEAD_DIM
    return pl.pallas_call(
        _nbr_attn_kernel,
        grid=(DEC_BATCH, NA_HEADS),
        in_specs=[pl.BlockSpec((DEC_SEQ, hd), lambda b, h: (b + q_row0, h)),
                  pl.BlockSpec((DEC_SEQ, hd), lambda b, h: (b, h)),
                  pl.BlockSpec((DEC_SEQ, hd), lambda b, h: (b, h)),
                  pl.BlockSpec((1, PAST_LEN, hd), lambda b, h: (b, 0, h)),
                  pl.BlockSpec((1, PAST_LEN, hd), lambda b, h: (b, 0, h)),
                  pl.BlockSpec((1, GRID_W, RPB_SLOTS * GRID_W), lambda b, h: (h, 0, 0))],
        out_specs=pl.BlockSpec((DEC_SEQ, hd), lambda b, h: (b, h)),
        out_shape=jax.ShapeDtypeStruct((T_DEC, NA_WIDTH), F32),
        compiler_params=_cparams(("arbitrary", "arbitrary")),
        name="neighbourhood_attention",
    )(q_all, k_dec, v_dec, kc, vc, rc)


def _conv_silu(u_ref, w_ref, b_ref, length):
    u = u_ref[...]
    halo = jnp.zeros((8, u.shape[1]), F32)
    padded = jnp.concatenate([halo, u, halo], axis=0)
    acc = u * w_ref[SSD_CONV // 2:SSD_CONV // 2 + 1, :] + b_ref[...]
    for k in range(SSD_CONV):
        d = k - SSD_CONV // 2
        if d == 0:
            continue
        shifted = pltpu.roll(padded, (-d) % (length + 16), 0)[8:8 + length]
        acc = acc + shifted * w_ref[k:k + 1, :]
    return _silu(acc)


def _ssd_kernel(*refs, length, has_h0, emit_state):
    (xs_ref, b_ref, c_ref, z_ref, dt_ref, dtt_ref, cwx_ref, cwb_ref, cwc_ref,
     cbx_ref, cbb_ref, cbc_ref, dtb_row_ref, dtb_col_ref, al_row_ref, al_col_ref,
     e_ref, d_ref) = refs[:18]
    pos = 18
    if has_h0:
        h0f_ref, h0b_ref = refs[pos:pos + 2]
        pos += 2
    u_ref = refs[pos]
    pos += 1
    if emit_state:
        hf_ref, hb_ref = refs[pos:pos + 2]
        pos += 2
    xs_s, b_s, c_s, y_s, st_s = refs[pos:]

    q = SSD_CHUNK
    nc = length // q
    xs_s[...] = _conv_silu(xs_ref, cwx_ref, cbx_ref, length)
    b_s[...] = _conv_silu(b_ref, cwb_ref, cbb_ref, length)
    c_s[...] = _conv_silu(c_ref, cwc_ref, cbc_ref, length)
    y_s[...] = d_ref[0] * xs_s[...]

    for dirn in range(2):
        if has_h0:
            h0 = (h0f_ref if dirn == 0 else h0b_ref)[0, 0]
            st_s[dirn] = jnp.transpose(h0.reshape(GROUP_W, SSD_STATE))
        else:
            st_s[dirn] = jnp.zeros((SSD_STATE, GROUP_W), F32)

    ri = lax.broadcasted_iota(jnp.int32, (q, q), 0)
    ci = lax.broadcasted_iota(jnp.int32, (q, q), 1)
    lower = ri >= ci
    upper = ri <= ci
    lower_bf = jnp.where(lower, 1.0, 0.0).astype(BF16)
    upper_bf = jnp.where(upper, 1.0, 0.0).astype(BF16)

    def chunk(c, dirn):
        r0 = c * q if isinstance(c, int) else pl.multiple_of(c * q, q)
        tri = lower if dirn == 0 else upper
        xs_c = xs_s[pl.ds(r0, q), :]
        bc = b_s[pl.ds(r0, q), :]
        cc = c_s[pl.ds(r0, q), :]
        dtp = _softplus(dt_ref[pl.ds(r0, q), :] + dtb_row_ref[...])
        da = dtp * (-jnp.exp(al_row_ref[...]))
        pre = _dot_exact_lhs(lower_bf, da)
        cs = pre if dirn == 0 else pre[q - 1:q, :] - pre + da
        e_bf = e_ref[dirn, 0]
        dt_e = _dot_exact_rhs(dtp, e_bf)
        cs_e = _dot_exact_rhs(cs, e_bf)
        dt_t = _softplus(dtt_ref[c, dirn, 0] + dtb_col_ref[dirn, 0])
        da_t = dt_t * (-jnp.exp(al_col_ref[dirn, 0]))
        cs_t = _dot_exact_rhs(da_t, upper_bf if dirn == 0 else lower_bf)

        cb = _dot_nt(cc.astype(BF16), bc.astype(BF16))
        bt = jnp.transpose(bc).astype(BF16)
        end = q - 1 if dirn == 0 else 0
        cs_end = cs_e[end:end + 1, :]
        xdt = xs_c * dt_e
        st = st_s[dirn]
        y = _dot(cc.astype(BF16), st.astype(BF16)) * jnp.exp(cs_e)
        st_s[dirn] = jnp.exp(cs_end) * st + _dot(bt, (xdt * jnp.exp(cs_end - cs_e)).astype(BF16))
        xdt_bf = xdt.astype(BF16)
        parts = []
        for r in range(HEADS_PER_GROUP):
            sl = slice(r * SSD_HEAD_DIM, (r + 1) * SSD_HEAD_DIM)
            diff = cs_e[:, r * SSD_HEAD_DIM:r * SSD_HEAD_DIM + 1] - cs_t[r:r + 1, :]
            lm = jnp.exp(jnp.where(tri, diff, -jnp.inf))
            parts.append(_dot((cb * lm).astype(BF16), xdt_bf[:, sl]))
        y = y + jnp.concatenate(parts, axis=-1)
        y_s[pl.ds(r0, q), :] = y_s[pl.ds(r0, q), :] + y

    if nc <= 2:
        for i in range(nc):
            chunk(i, 0)
            chunk(nc - 1 - i, 1)
    else:
        def both(i, carry):
            chunk(i, 0)
            chunk(nc - 1 - i, 1)
            return carry
        lax.fori_loop(0, nc, both, 0)

    u_ref[...] = y_s[...] * _silu(z_ref[...])
    if emit_state:
        hf_ref[0, 0] = jnp.transpose(st_s[0]).reshape(HEADS_PER_GROUP, SSD_HEAD_DIM, SSD_STATE)
        hb_ref[0, 0] = jnp.transpose(st_s[1]).reshape(HEADS_PER_GROUP, SSD_HEAD_DIM, SSD_STATE)


def _ssd_mixer(zx, dt_all, dt_t, consts, *, length, n_seq, seq_block0, h0=None, emit_state=False):
    (cw, cb, dtb_row, dtb_col, al_row, al_col, expand, d_row) = consts
    g_w, n_s = GROUP_W, SSD_STATE
    xs_cb0 = D_MODEL // g_w
    b_cb0 = (D_MODEL + SSD_WIDTH) // n_s
    c_cb0 = b_cb0 + SSD_GROUPS
    cw_b0 = SSD_WIDTH // n_s
    nck = length // SSD_CHUNK
    in_specs = [
        pl.BlockSpec((length, g_w), lambda s, g: (s + seq_block0, xs_cb0 + g)),
        pl.BlockSpec((length, n_s), lambda s, g: (s + seq_block0, b_cb0 + g)),
        pl.BlockSpec((length, n_s), lambda s, g: (s + seq_block0, c_cb0 + g)),
        pl.BlockSpec((length, g_w), lambda s, g: (s + seq_block0, g)),
        pl.BlockSpec((length, 128), lambda s, g: (s + seq_block0, 0)),
        pl.BlockSpec((nck, 2, 1, HEADS_PER_GROUP, SSD_CHUNK), lambda s, g: (s + seq_block0, 0, g, 0, 0)),
        pl.BlockSpec((SSD_CONV, g_w), lambda s, g: (0, g)),
        pl.BlockSpec((SSD_CONV, n_s), lambda s, g: (0, cw_b0 + g)),
        pl.BlockSpec((SSD_CONV, n_s), lambda s, g: (0, cw_b0 + SSD_GROUPS + g)),
        pl.BlockSpec((1, g_w), lambda s, g: (0, g)),
        pl.BlockSpec((1, n_s), lambda s, g: (0, cw_b0 + g)),
        pl.BlockSpec((1, n_s), lambda s, g: (0, cw_b0 + SSD_GROUPS + g)),
        pl.BlockSpec((1, 128), lambda s, g: (0, 0)),
        pl.BlockSpec((2, 1, HEADS_PER_GROUP, SSD_CHUNK), lambda s, g: (0, g, 0, 0)),
        pl.BlockSpec((1, 128), lambda s, g: (0, 0)),
        pl.BlockSpec((2, 1, HEADS_PER_GROUP, SSD_CHUNK), lambda s, g: (0, g, 0, 0)),
        pl.BlockSpec((2, 1, 128, g_w), lambda s, g: (0, g, 0, 0)),
        pl.BlockSpec((1, 1, g_w), lambda s, g: (g, 0, 0)),
    ]
    args = [zx, zx, zx, zx, dt_all, dt_t, cw, cw, cw, cb, cb, cb,
            dtb_row, dtb_col, al_row, al_col, expand, d_row]
    st_spec = pl.BlockSpec((1, 1, HEADS_PER_GROUP, SSD_HEAD_DIM, n_s), lambda s, g: (s, g, 0, 0, 0))
    if h0 is not None:
        in_specs += [st_spec, st_spec]
        args += list(h0)
    out_specs = [pl.BlockSpec((length, g_w), lambda s, g: (s, g))]
    out_shape = [jax.ShapeDtypeStruct((n_seq * length, SSD_WIDTH), F32)]
    if emit_state:
        st_shape = jax.ShapeDtypeStruct((n_seq, SSD_GROUPS, HEADS_PER_GROUP, SSD_HEAD_DIM, n_s), F32)
        out_specs += [st_spec, st_spec]
        out_shape += [st_shape, st_shape]
    return pl.pallas_call(
        functools.partial(_ssd_kernel, length=length, has_h0=h0 is not None, emit_state=emit_state),
        grid=(n_seq, SSD_GROUPS),
        in_specs=in_specs,
        out_specs=out_specs,
        out_shape=out_shape,
        scratch_shapes=[pltpu.VMEM((length, g_w), F32), pltpu.VMEM((length, n_s), F32),
                        pltpu.VMEM((length, n_s), F32), pltpu.VMEM((length, g_w), F32),
                        pltpu.VMEM((2, n_s, g_w), F32)],
        compiler_params=_cparams(("arbitrary", "arbitrary")),
        name="ssd_mixer_%d" % length,
    )(*args)


def _ssd_consts(conv_w, conv_b, dt_bias, a_log, d_skip):
    hpg = HEADS_PER_GROUP
    dtb_row = jnp.pad(dt_bias.reshape(1, 2 * SSD_HEADS), ((0, 0), (0, 128 - 2 * SSD_HEADS)))
    col = lambda p: jnp.broadcast_to(p.reshape(2, SSD_GROUPS, hpg, 1), (2, SSD_GROUPS, hpg, SSD_CHUNK))
    al_row = jnp.pad(a_log.reshape(1, 2 * SSD_HEADS), ((0, 0), (0, 128 - 2 * SSD_HEADS)))
    src = (jnp.arange(2)[:, None, None] * SSD_HEADS + jnp.arange(SSD_GROUPS)[None, :, None] * hpg
           + jnp.arange(GROUP_W)[None, None, :] // SSD_HEAD_DIM)
    expand = (jnp.arange(128)[None, None, :, None] == src[:, :, None, :]).astype(BF16)
    d_row = jnp.repeat(d_skip.reshape(SSD_GROUPS, 1, hpg), SSD_HEAD_DIM, axis=-1)
    return (conv_w, conv_b.reshape(1, SSD_CONV_CH), dtb_row, col(dt_bias), al_row, col(a_log), expand, d_row)


def _merge_kernel(nac_ref, nad_ref, uc_ref, ud_ref, nw_ref, w1_ref, w2_ref, g_na_ref, g_ssd_ref,
                  o_ref, *, n_ctx):
    is_ctx = pl.program_id(0) < n_ctx
    a1 = jnp.where(is_ctx, nac_ref[...], nad_ref[...]).astype(BF16)
    u = jnp.where(is_ctx, uc_ref[...], ud_ref[...])
    r = lax.rsqrt(jnp.mean(u * u, axis=-1, keepdims=True) + LN_EPS)
    a2 = (u * r * nw_ref[...]).astype(BF16)
    o = (jax.nn.sigmoid(g_na_ref[...]) * _dot(a1, w1_ref[...])
         + jax.nn.sigmoid(g_ssd_ref[...]) * _dot(a2, w2_ref[...]))
    o_ref[...] = o.astype(o_ref.dtype)


def _merge_branches(na_ctx, na_dec, u_ctx, u_dec, norm_w, w_na_out, w_ssd_out, gates):
    tm = 256
    n_ctx = T_CTX // tm
    ctx_rows = lambda i: (jnp.minimum(i, n_ctx - 1), 0)
    dec_rows = lambda i: (jnp.maximum(i - n_ctx, 0), 0)
    const = lambda i: (0, 0)
    resident = pl.Buffered(1)
    return pl.pallas_call(
        functools.partial(_merge_kernel, n_ctx=n_ctx),
        grid=(T_ALL // tm,),
        in_specs=[pl.BlockSpec((tm, D_MODEL), ctx_rows), pl.BlockSpec((tm, D_MODEL), dec_rows),
                  pl.BlockSpec((tm, D_MODEL), ctx_rows), pl.BlockSpec((tm, D_MODEL), dec_rows),
                  pl.BlockSpec((1, D_MODEL), const),
                  pl.BlockSpec((D_MODEL, D_MODEL), const, pipeline_mode=resident),
                  pl.BlockSpec((D_MODEL, D_MODEL), const, pipeline_mode=resident),
                  pl.BlockSpec((tm, D_MODEL), lambda i: (i, 1)),
                  pl.BlockSpec((tm, D_MODEL), lambda i: (i, 0))],
        out_specs=pl.BlockSpec((tm, D_MODEL), lambda i: (i, 0)),
        out_shape=jax.ShapeDtypeStruct((T_ALL, D_MODEL), BF16),
        compiler_params=_cparams(("arbitrary",)),
        name="merge_branches",
    )(na_ctx, na_dec, u_ctx, u_dec, norm_w, w_na_out, w_ssd_out, gates, gates)


def _post_mix_kernel(y_ref, wo_ref, xc_ref, xd_ref, m_ref, g_ref, b_ref, wr_ref, br_ref,
                     x1_ref, h2_ref, idx_ref, wgt_ref, *, n_ctx):
    i = pl.program_id(0)
    x = jnp.where(i < n_ctx, xc_ref[...], xd_ref[...])
    mix = _dot(y_ref[...], wo_ref[...])
    x1 = _layer_norm(DN_ALPHA * x + m_ref[0, 2:3, :] * mix) * g_ref[...] + b_ref[...]
    x1_ref[...] = x1
    h2 = _layer_norm(x1) * (1.0 + m_ref[0, 4:5, :]) + m_ref[0, 3:4, :]
    h2_ref[...] = h2
    logits = _dot_nt_f32(wr_ref[...], h2) + br_ref[...]
    eidx = lax.broadcasted_iota(jnp.int32, logits.shape, 0)
    vals, idxs = [], []
    for _ in range(TOP_K):
        m = jnp.max(logits, axis=0, keepdims=True)
        sel = jnp.min(jnp.where(logits == m, eidx, N_EXPERTS), axis=0, keepdims=True)
        logits = jnp.where(eidx == sel, -jnp.inf, logits)
        vals.append(m)
        idxs.append(sel)
    ex = [jnp.exp(v - vals[0]) for v in vals]
    tot = ex[0] + ex[1] + ex[2] + ex[3]
    idx_ref[...] = jnp.concatenate(idxs, axis=0)
    wgt_ref[...] = jnp.concatenate([e / tot for e in ex], axis=0)


def _post_mix(y, w_o, x_ctx, x_dec, mods, ln_g, ln_b, w_router_t, b_router):
    tm = 512
    n_ctx = T_CTX // tm
    row = lambda i: (i, 0)
    const = lambda i: (0, 0)
    return pl.pallas_call(
        functools.partial(_post_mix_kernel, n_ctx=n_ctx),
        grid=(T_ALL // tm,),
        in_specs=[pl.BlockSpec((tm, D_MODEL), row),
                  pl.BlockSpec((D_MODEL, D_MODEL), const, pipeline_mode=pl.Buffered(1)),
                  pl.BlockSpec((tm, D_MODEL), lambda i: (jnp.minimum(i, n_ctx - 1), 0)),
                  pl.BlockSpec((tm, D_MODEL), lambda i: (jnp.maximum(i - n_ctx, 0), 0)),
                  pl.BlockSpec((1, 6, D_MODEL), lambda i: (_mod_row(i, tm), 0, 0)),
                  pl.BlockSpec((1, D_MODEL), const), pl.BlockSpec((1, D_MODEL), const),
                  pl.BlockSpec((N_EXPERTS, D_MODEL), const), pl.BlockSpec((N_EXPERTS, 1), const)],
        out_specs=[pl.BlockSpec((tm, D_MODEL), row), pl.BlockSpec((tm, D_MODEL), row),
                   pl.BlockSpec((TOP_K, tm), lambda i: (0, i)), pl.BlockSpec((TOP_K, tm), lambda i: (0, i))],
        out_shape=[jax.ShapeDtypeStruct((T_ALL, D_MODEL), F32), jax.ShapeDtypeStruct((T_ALL, D_MODEL), F32),
                   jax.ShapeDtypeStruct((TOP_K, T_ALL), jnp.int32), jax.ShapeDtypeStruct((TOP_K, T_ALL), F32)],
        compiler_params=_cparams(("arbitrary",)),
        name="post_mix_router",
    )(y, w_o, x_ctx, x_dec, mods, ln_g, ln_b, w_router_t, b_router)


DISPATCH_TOK = 128


def _tile_rows(t):
    return pl.ds(pl.multiple_of(t * MOE_TM, MOE_TM), MOE_TM)


def _dispatch_kernel(t0_ref, nt_ref, nv_ref, pos_ref, x_hbm, xs_hbm, buf, zero_s, sem_in, sem_out, sem_z,
                     *, n_steps):
    i = pl.program_id(0)
    slot = i % 3

    def zero_copy(tile):
        return pltpu.make_async_copy(zero_s, xs_hbm.at[_tile_rows(tile)], sem_z)

    def in_copy(step, s):
        rows = pl.ds(pl.multiple_of(step * DISPATCH_TOK, DISPATCH_TOK), DISPATCH_TOK)
        return pltpu.make_async_copy(x_hbm.at[rows], buf.at[s], sem_in.at[s])

    def row_copy(s, t, dst):
        return pltpu.make_async_copy(buf.at[s, pl.ds(t, 1)], xs_hbm.at[pl.ds(dst, 1)], sem_out.at[s])

    def wait_rows(s):
        for _ in range(TOP_K * DISPATCH_TOK):
            row_copy(s, 0, 0).wait()

    @pl.when(i == 0)
    def _():
        in_copy(0, 0).start()
        zero_s[...] = jnp.zeros(zero_s.shape, zero_s.dtype)
        for e in range(N_EXPERTS):
            @pl.when(nt_ref[e] > 0)
            def _():
                zero_copy(t0_ref[e] + nt_ref[e] - 1).start()

        def tail_start(t, c):
            zero_copy(t).start()
            return c
        lax.fori_loop(nv_ref[0], MOE_TILES, tail_start, 0)
        for e in range(N_EXPERTS):
            @pl.when(nt_ref[e] > 0)
            def _():
                zero_copy(0).wait()

        def tail_wait(t, c):
            zero_copy(0).wait()
            return c
        lax.fori_loop(nv_ref[0], MOE_TILES, tail_wait, 0)

    in_copy(i, slot).wait()

    @pl.when(i + 1 < n_steps)
    def _():
        in_copy(i + 1, (i + 1) % 3).start()

    for k in range(TOP_K):
        for t in range(DISPATCH_TOK):
            row_copy(slot, t, pos_ref[0, k, t]).start()

    @pl.when(i > 0)
    def _():
        wait_rows((i + 2) % 3)

    @pl.when(i == n_steps - 1)
    def _():
        wait_rows(slot)


def _dispatch(h2, pos_blocks, tile_start, tiles, n_valid):
    n_steps = T_ALL // DISPATCH_TOK
    grid_spec = pltpu.PrefetchScalarGridSpec(
        num_scalar_prefetch=3,
        grid=(n_steps,),
        in_specs=[pl.BlockSpec((1, TOP_K, DISPATCH_TOK), lambda i, *_: (i, 0, 0), memory_space=pltpu.SMEM),
                  pl.BlockSpec(memory_space=pl.ANY)],
        out_specs=pl.BlockSpec(memory_space=pl.ANY),
        scratch_shapes=[pltpu.VMEM((3, DISPATCH_TOK, D_MODEL), F32),
                        pltpu.VMEM((MOE_TM, D_MODEL), F32),
                        pltpu.SemaphoreType.DMA((3,)), pltpu.SemaphoreType.DMA((3,)),
                        pltpu.SemaphoreType.DMA(())],
    )
    return pl.pallas_call(
        functools.partial(_dispatch_kernel, n_steps=n_steps),
        grid_spec=grid_spec,
        out_shape=jax.ShapeDtypeStruct((MOE_ROWS, D_MODEL), F32),
        compiler_params=_cparams(("arbitrary",)),
        name="moe_dispatch",
    )(tile_start, tiles, n_valid, pos_blocks, h2)


def _expert_tile_loop(t0, nt, in_copy, out_copy, compute):
    @pl.when(nt > 0)
    def _():
        in_copy(t0, 0).start()

    def body(i, carry):
        slot = i % 2
        in_copy(t0 + i, slot).wait()

        @pl.when(i + 1 < nt)
        def _():
            in_copy(t0 + i + 1, 1 - slot).start()

        @pl.when(i >= 2)
        def _():
            out_copy(t0 + i - 2, slot).wait()

        compute(slot)
        out_copy(t0 + i, slot).start()
        return carry

    lax.fori_loop(0, nt, body, 0)

    @pl.when(nt >= 2)
    def _():
        out_copy(t0, nt % 2).wait()

    @pl.when(nt >= 1)
    def _():
        out_copy(t0, (nt + 1) % 2).wait()


def _zero_tail_tiles(is_last, n_valid, obuf, out_copy):
    @pl.when(is_last)
    def _():
        obuf[0] = jnp.zeros(obuf.shape[1:], obuf.dtype)

        def body(t, carry):
            out_copy(t, 0).start()
            out_copy(t, 0).wait()
            return carry
        lax.fori_loop(n_valid, MOE_TILES, body, 0)


def _cast_weight_parts(parts, w_s):
    rows = w_s.shape[0] // len(parts)
    for r, part in enumerate(parts):
        w_s[r * rows:(r + 1) * rows, :] = part[0].astype(BF16)


def _gate_up_kernel(t0_ref, nt_ref, nv_ref, xs_hbm, *refs, tn):
    wg_parts, wu_parts = refs[:W_SPLIT], refs[W_SPLIT:2 * W_SPLIT]
    bg_ref, bu_ref, hid_hbm, wg_s, wu_s, xbuf, obuf, sem_in, sem_out = refs[2 * W_SPLIT:]
    j = pl.program_id(0)
    e = pl.program_id(1)
    _cast_weight_parts(wg_parts, wg_s)
    _cast_weight_parts(wu_parts, wu_s)

    def in_copy(t, slot):
        return pltpu.make_async_copy(xs_hbm.at[_tile_rows(t)], xbuf.at[slot], sem_in.at[slot])

    def out_copy(t, slot):
        cols = pl.ds(pl.multiple_of(j * tn, tn), tn)
        return pltpu.make_async_copy(obuf.at[slot], hid_hbm.at[_tile_rows(t), cols], sem_out.at[slot])

    def compute(slot):
        x = xbuf[slot].astype(BF16)
        gate = jnp.minimum(_dot(x, wg_s[...]) + bg_ref[0], SWIGLU_LIMIT)
        up = jnp.clip(_dot(x, wu_s[...]) + bu_ref[0], -SWIGLU_LIMIT, SWIGLU_LIMIT)
        hid = (up + 1.0) * gate * jax.nn.sigmoid(SWIGLU_ALPHA * gate)
        obuf[slot] = hid.astype(obuf.dtype)

    _expert_tile_loop(t0_ref[e], nt_ref[e], in_copy, out_copy, compute)
    _zero_tail_tiles(e == N_EXPERTS - 1, nv_ref[0], obuf, out_copy)


def _moe_gate_up(xs, tile_start, tiles, n_valid, w_gate_up, b_gate_up):
    tn = 1024
    nj = D_EXPERT // tn
    rows = D_MODEL // W_SPLIT
    grid_spec = pltpu.PrefetchScalarGridSpec(
        num_scalar_prefetch=3,
        grid=(nj, N_EXPERTS),
        in_specs=[pl.BlockSpec(memory_space=pl.ANY)]
        + [pl.BlockSpec((1, rows, tn), lambda j, e, *_, r=r: (e, r, j)) for r in range(W_SPLIT)]
        + [pl.BlockSpec((1, rows, tn), lambda j, e, *_, r=r: (e, r, nj + j)) for r in range(W_SPLIT)]
        + [pl.BlockSpec((1, 1, tn), lambda j, e, *_: (e, 0, j)),
           pl.BlockSpec((1, 1, tn), lambda j, e, *_: (e, 0, nj + j))],
        out_specs=pl.BlockSpec(memory_space=pl.ANY),
        scratch_shapes=[pltpu.VMEM((D_MODEL, tn), BF16), pltpu.VMEM((D_MODEL, tn), BF16),
                        pltpu.VMEM((2, MOE_TM, D_MODEL), F32), pltpu.VMEM((2, MOE_TM, tn), BF16),
                        pltpu.SemaphoreType.DMA((2,)), pltpu.SemaphoreType.DMA((2,))],
    )
    return pl.pallas_call(
        functools.partial(_gate_up_kernel, tn=tn),
        grid_spec=grid_spec,
        out_shape=jax.ShapeDtypeStruct((MOE_ROWS, D_EXPERT), BF16),
        compiler_params=_cparams(("arbitrary", "arbitrary")),
        name="moe_gate_up",
    )(tile_start, tiles, n_valid, xs, *([w_gate_up] * (2 * W_SPLIT)), b_gate_up, b_gate_up)


def _down_kernel(t0_ref, nt_ref, nv_ref, hid_hbm, *refs):
    w_parts = refs[:W_SPLIT]
    b_ref, ys_hbm, w_s, hbuf, obuf, sem_in, sem_out = refs[W_SPLIT:]
    e = pl.program_id(0)
    _cast_weight_parts(w_parts, w_s)

    def in_copy(t, slot):
        return pltpu.make_async_copy(hid_hbm.at[_tile_rows(t)], hbuf.at[slot], sem_in.at[slot])

    def out_copy(t, slot):
        return pltpu.make_async_copy(obuf.at[slot], ys_hbm.at[_tile_rows(t)], sem_out.at[slot])

    def compute(slot):
        obuf[slot] = _dot(hbuf[slot], w_s[...]) + b_ref[0]

    _expert_tile_loop(t0_ref[e], nt_ref[e], in_copy, out_copy, compute)
    _zero_tail_tiles(e == N_EXPERTS - 1, nv_ref[0], obuf, out_copy)


def _moe_down(hid, tile_start, tiles, n_valid, w_down, b_down):
    rows = D_EXPERT // W_SPLIT
    grid_spec = pltpu.PrefetchScalarGridSpec(
        num_scalar_prefetch=3,
        grid=(N_EXPERTS,),
        in_specs=[pl.BlockSpec(memory_space=pl.ANY)]
        + [pl.BlockSpec((1, rows, D_MODEL), lambda e, *_, r=r: (e, r, 0)) for r in range(W_SPLIT)]
        + [pl.BlockSpec((1, 1, D_MODEL), lambda e, *_: (e, 0, 0))],
        out_specs=pl.BlockSpec(memory_space=pl.ANY),
        scratch_shapes=[pltpu.VMEM((D_EXPERT, D_MODEL), BF16),
                        pltpu.VMEM((2, MOE_TM, D_EXPERT), BF16), pltpu.VMEM((2, MOE_TM, D_MODEL), F32),
                        pltpu.SemaphoreType.DMA((2,)), pltpu.SemaphoreType.DMA((2,))],
    )
    return pl.pallas_call(
        _down_kernel,
        grid_spec=grid_spec,
        out_shape=jax.ShapeDtypeStruct((MOE_ROWS, D_MODEL), F32),
        compiler_params=_cparams(("arbitrary",)),
        name="moe_down",
    )(tile_start, tiles, n_valid, hid, *([w_down] * W_SPLIT), b_down)


COMBINE_TOK = 64


def _start_rows(src_hbm, idx_ref, idx_lead, dst, sem, count):
    for r in range(count):
        pltpu.make_async_copy(src_hbm.at[pl.ds(idx_ref[idx_lead + (r,)], 1)], dst.at[pl.ds(r, 1)], sem).start()


def _wait_rows(src_hbm, dst, sem, count):
    for _ in range(count):
        pltpu.make_async_copy(src_hbm.at[pl.ds(0, 1)], dst.at[pl.ds(0, 1)], sem).wait()


def _combine_kernel(pos_ref, pos_next_ref, ys_hbm, wgt_ref, x1_ref, m_ref, g_ref, b_ref, o_ref, buf, sem,
                    *, n_steps):
    i = pl.program_id(0)
    slot = i % 2

    def start(p_ref, s):
        for k in range(TOP_K):
            _start_rows(ys_hbm, p_ref, (0, k), buf.at[s, k], sem.at[s], COMBINE_TOK)

    @pl.when(i == 0)
    def _():
        start(pos_ref, 0)

    @pl.when(i + 1 < n_steps)
    def _():
        start(pos_next_ref, 1 - slot)

    _wait_rows(ys_hbm, buf.at[slot, 0], sem.at[slot], TOP_K * COMBINE_TOK)
    w = wgt_ref[...]
    ffn = w[:, 0:1] * buf[slot, 0]
    for k in range(1, TOP_K):
        ffn = ffn + w[:, k:k + 1] * buf[slot, k]
    x2 = _layer_norm(DN_ALPHA * x1_ref[...] + m_ref[0, 5:6, :] * ffn) * g_ref[...] + b_ref[...]
    o_ref[...] = x2


def _combine(ys, pos_blocks, wgt, x1, mods, ln_g, ln_b, *, rows, row0):
    tm = COMBINE_TOK
    b0 = row0 // tm
    n = rows // tm
    const = lambda i: (0, 0)
    pos_spec = lambda f: pl.BlockSpec((1, TOP_K, tm), f, memory_space=pltpu.SMEM)
    return pl.pallas_call(
        functools.partial(_combine_kernel, n_steps=n),
        grid=(n,),
        in_specs=[pos_spec(lambda i: (i + b0, 0, 0)),
                  pos_spec(lambda i: (jnp.minimum(i + 1, n - 1) + b0, 0, 0)),
                  pl.BlockSpec(memory_space=pl.ANY),
                  pl.BlockSpec((tm, TOP_K), lambda i: (i + b0, 0)),
                  pl.BlockSpec((tm, D_MODEL), lambda i: (i + b0, 0)),
                  pl.BlockSpec((1, 6, D_MODEL), lambda i: (_mod_row(i + b0, tm), 0, 0)),
                  pl.BlockSpec((1, D_MODEL), const), pl.BlockSpec((1, D_MODEL), const)],
        out_specs=pl.BlockSpec((tm, D_MODEL), lambda i: (i, 0)),
        out_shape=jax.ShapeDtypeStruct((rows, D_MODEL), F32),
        scratch_shapes=[pltpu.VMEM((2, TOP_K, tm, D_MODEL), F32), pltpu.SemaphoreType.DMA((2,))],
        compiler_params=_cparams(("arbitrary",)),
        name="moe_combine_ln2",
    )(pos_blocks, pos_blocks, ys, wgt, x1, mods, ln_g, ln_b)


def _moe_plan(idx_t):
    flat = idx_t.reshape(-1)
    onehot = (flat[:, None] == jnp.arange(N_EXPERTS)[None, :]).astype(jnp.int32)
    rank = jnp.take_along_axis(jnp.cumsum(onehot, axis=0), flat[:, None], axis=1)[:, 0] - 1
    counts = jnp.sum(onehot, axis=0)
    tiles = (counts + MOE_TM - 1) // MOE_TM
    tile_end = jnp.cumsum(tiles)
    tile_start = tile_end - tiles
    pos = tile_start[flat] * MOE_TM + rank
    n_valid = tile_end[-1]
    i32 = lambda a: a.astype(jnp.int32)
    return i32(pos.reshape(TOP_K, T_ALL)), i32(tile_start), i32(tiles), i32(n_valid.reshape(1))


def kernel(x_prompt, x_sample, cache_na_k, cache_na_v, state_ssd_fwd, state_ssd_bwd, c, c_ctx, w_mod, b_mod, w_in, ssd_conv_w, ssd_conv_b, ssd_dt_bias, ssd_a_log, ssd_d, ssd_norm_w, na_rpb, w_ssd_out, w_na_out, w_o, ln1_g, ln1_b, ln2_g, ln2_b, w_router, b_router, w_gate_up, b_gate_up, w_down, b_down):
    assert w_mod.shape[0] == 1, "single-layer trunk"
    x_ctx = x_prompt.reshape(T_CTX, D_MODEL)
    x_dec = x_sample.reshape(T_DEC, D_MODEL)

    cvec = jnp.concatenate([c_ctx[None], c, jnp.zeros((8 - 1 - DEC_BATCH, D_MODEL), F32)], axis=0)
    mods = _modulation(cvec, w_mod[0], b_mod[0])[:1 + DEC_BATCH].reshape(1 + DEC_BATCH, 6, D_MODEL)

    h = _ln_modulate(x_ctx, x_dec, mods)
    w = w_in[0]
    n_main = 3 * NA_WIDTH + SSD_WIDTH + SSD_CONV_CH
    w_bf = w.astype(BF16)
    w_dt = jnp.pad(w[:, n_main:n_main + 2 * SSD_HEADS], ((0, 0), (0, 128 - 2 * SSD_HEADS))).astype(BF16)
    w_gates = w_bf[:, n_main + 2 * SSD_HEADS:]
    tm = 1024
    ctx_blocks = T_CTX // tm
    q_all = _matmul(h, w_bf, rows=T_ALL, col_block0=0, n_out=NA_WIDTH, name="proj_q")
    k_ctx = _matmul(h, w_bf, rows=T_CTX, col_block0=2, n_out=NA_WIDTH, name="proj_k_ctx")
    k_dec = _matmul(h, w_bf, rows=T_DEC, row_block0=ctx_blocks, col_block0=2, n_out=NA_WIDTH, name="proj_k_dec")
    v_ctx = _matmul(h, w_bf, rows=T_CTX, col_block0=4, n_out=NA_WIDTH, name="proj_v_ctx")
    v_dec = _matmul(h, w_bf, rows=T_DEC, row_block0=ctx_blocks, col_block0=4, n_out=NA_WIDTH, name="proj_v_dec")
    zx = _matmul(h, w_bf, rows=T_ALL, col_block0=6, n_out=SSD_WIDTH + SSD_CONV_CH, name="proj_zxbc")
    gates = _matmul(h, w_gates, rows=T_ALL, name="proj_gates")
    dt_all = _matmul(h, w_dt, rows=T_ALL, tn=128, name="proj_dt")

    na_ctx = _context_attention(q_all, k_ctx, v_ctx)
    kc = cache_na_k[:, 0].reshape(DEC_BATCH, PAST_LEN, NA_WIDTH)
    vc = cache_na_v[:, 0].reshape(DEC_BATCH, PAST_LEN, NA_WIDTH)
    na_dec = _neighbourhood_attention(q_all, k_dec, v_dec, kc, vc, _rpb_table(na_rpb[0]))

    consts = _ssd_consts(ssd_conv_w[0], ssd_conv_b[0], ssd_dt_bias[0], ssd_a_log[0], ssd_d[0])
    dt_t = dt_all[:, :2 * SSD_HEADS].reshape(T_ALL // SSD_CHUNK, SSD_CHUNK, 2, SSD_GROUPS, HEADS_PER_GROUP)
    dt_t = jnp.transpose(dt_t, (0, 2, 3, 4, 1))
    u_ctx, h_f, h_b = _ssd_mixer(zx, dt_all, dt_t, consts, length=SEQ, n_seq=BATCH, seq_block0=0,
                                 emit_state=True)
    h0 = (state_ssd_fwd[:, 0].reshape(DEC_BATCH, SSD_GROUPS, HEADS_PER_GROUP, SSD_HEAD_DIM, SSD_STATE),
          state_ssd_bwd[:, 0].reshape(DEC_BATCH, SSD_GROUPS, HEADS_PER_GROUP, SSD_HEAD_DIM, SSD_STATE))
    (u_dec,) = _ssd_mixer(zx, dt_all, dt_t, consts, length=DEC_SEQ, n_seq=DEC_BATCH,
                          seq_block0=T_CTX // DEC_SEQ, h0=h0)

    y = _merge_branches(na_ctx, na_dec, u_ctx, u_dec, ssd_norm_w[0].reshape(1, SSD_WIDTH),
                        w_na_out[0].astype(BF16), w_ssd_out[0].astype(BF16), gates)
    x1, h2, idx_t, wgt_t = _post_mix(y, w_o[0].astype(BF16), x_ctx, x_dec, mods,
                                     ln1_g[0].reshape(1, D_MODEL), ln1_b[0].reshape(1, D_MODEL),
                                     jnp.transpose(w_router[0]), b_router[0].reshape(N_EXPERTS, 1))

    pos, tile_start, tiles, n_valid = _moe_plan(idx_t)
    pos_d = jnp.transpose(pos.reshape(TOP_K, T_ALL // DISPATCH_TOK, DISPATCH_TOK), (1, 0, 2))
    xs = _dispatch(h2, pos_d, tile_start, tiles, n_valid)
    hid = _moe_gate_up(xs, tile_start, tiles, n_valid, w_gate_up[0],
                       b_gate_up[0].reshape(N_EXPERTS, 1, 2 * D_EXPERT))
    ys = _moe_down(hid, tile_start, tiles, n_valid, w_down[0], b_down[0].reshape(N_EXPERTS, 1, D_MODEL))
    pos_c = jnp.transpose(pos.reshape(TOP_K, T_ALL // COMBINE_TOK, COMBINE_TOK), (1, 0, 2))
    wgt = jnp.transpose(wgt_t)
    g2, b2 = ln2_g[0].reshape(1, D_MODEL), ln2_b[0].reshape(1, D_MODEL)
    y_ctx = _combine(ys, pos_c, wgt, x1, mods, g2, b2, rows=T_CTX, row0=0)
    y_dec = _combine(ys, pos_c, wgt, x1, mods, g2, b2, rows=T_DEC, row0=T_CTX)

    return (y_ctx.reshape(BATCH, SEQ, D_MODEL),
            y_dec.reshape(DEC_BATCH, DEC_SEQ, D_MODEL),
            k_ctx.reshape(BATCH, 1, SEQ, NA_HEADS, NA_HEAD_DIM),
            v_ctx.reshape(BATCH, 1, SEQ, NA_HEADS, NA_HEAD_DIM),
            h_f.reshape(BATCH, 1, SSD_HEADS, SSD_HEAD_DIM, SSD_STATE),
            h_b.reshape(BATCH, 1, SSD_HEADS, SSD_HEAD_DIM, SSD_STATE))
```

```python
import functools

import jax
import jax.numpy as jnp
from jax import lax
from jax.experimental import pallas as pl
from jax.experimental.pallas import tpu as pltpu

F32 = jnp.float32
BF16 = jnp.bfloat16

D_MODEL = 2048
BATCH = 32
SEQ = 256
DEC_BATCH = 2
DEC_SEQ = 1024
PAST_LEN = 512
GRID_W = 64
NA_HEADS = 16
NA_HEAD_DIM = 128
NA_WIDTH = NA_HEADS * NA_HEAD_DIM
NA_WIN_R = 8
NA_WIN_C = 16
SSD_HEADS = 32
SSD_HEAD_DIM = 64
SSD_WIDTH = SSD_HEADS * SSD_HEAD_DIM
SSD_GROUPS = 4
SSD_STATE = 128
SSD_CONV = 5
SSD_CHUNK = 128
SSD_CONV_CH = SSD_WIDTH + 2 * SSD_GROUPS * SSD_STATE
N_EXPERTS = 32
TOP_K = 4
D_EXPERT = 2048
SWIGLU_LIMIT = 7.0
SWIGLU_ALPHA = 1.702
DN_ALPHA = 2.0 ** 0.25
LN_EPS = 1e-5

T_CTX = BATCH * SEQ
T_DEC = DEC_BATCH * DEC_SEQ
T_ALL = T_CTX + T_DEC
HEADS_PER_GROUP = SSD_HEADS // SSD_GROUPS
GROUP_W = HEADS_PER_GROUP * SSD_HEAD_DIM
ROWS = DEC_SEQ // GRID_W
RPB_SLOTS = 16

MOE_TM = 256
MOE_TILES = T_ALL * TOP_K // MOE_TM + N_EXPERTS
MOE_ROWS = MOE_TILES * MOE_TM
W_SPLIT = 8

VMEM_LIMIT = 56 * 1024 * 1024


def _cparams(sem):
    return pltpu.CompilerParams(dimension_semantics=sem, vmem_limit_bytes=VMEM_LIMIT)


def _split3(x):
    hi = x.astype(BF16)
    r1 = x - hi.astype(F32)
    mid = r1.astype(BF16)
    lo = (r1 - mid.astype(F32)).astype(BF16)
    return hi, mid, lo


def _dot(a, b):
    return jnp.dot(a, b, preferred_element_type=F32)


def _dot_nt(a, b):
    return lax.dot_general(a, b, (((1,), (1,)), ((), ())), preferred_element_type=F32)


def _dot_exact_rhs(x, m_bf):
    hi, mid, lo = _split3(x)
    return _dot(hi, m_bf) + (_dot(mid, m_bf) + _dot(lo, m_bf))


def _dot_hi_mid_rhs(x, m_bf):
    hi = x.astype(BF16)
    mid = (x - hi.astype(F32)).astype(BF16)
    return _dot(hi, m_bf) + _dot(mid, m_bf)


def _dot_exact_lhs(m_bf, x):
    hi, mid, lo = _split3(x)
    return _dot(m_bf, hi) + (_dot(m_bf, mid) + _dot(m_bf, lo))


def _dot_nt_f32(a, b):
    ah, am, al = _split3(a)
    bh, bm, bl = _split3(b)
    small = _dot_nt(ah, bl) + _dot_nt(am, bm) + _dot_nt(al, bh)
    mid = _dot_nt(ah, bm) + _dot_nt(am, bh)
    return _dot_nt(ah, bh) + (mid + small)


def _dot_f32(a, b):
    ah, am, al = _split3(a)
    bh, bm, bl = _split3(b)
    small = _dot(ah, bl) + _dot(am, bm) + _dot(al, bh)
    mid = _dot(ah, bm) + _dot(am, bh)
    return _dot(ah, bh) + (mid + small)


def _silu(x):
    return x * jax.nn.sigmoid(x)


def _softplus(x):
    return jnp.maximum(x, 0.0) + jnp.log1p(jnp.exp(-jnp.abs(x)))


def _layer_norm(x):
    mu = jnp.mean(x, axis=-1, keepdims=True)
    xc = x - mu
    var = jnp.mean(xc * xc, axis=-1, keepdims=True)
    return xc * lax.rsqrt(var + LN_EPS)


def _mod_row(i, tm):
    n_ctx = T_CTX // tm
    per_b = DEC_SEQ // tm
    return jnp.where(i < n_ctx, 0, 1 + (i - n_ctx) // per_b)


def _mod_kernel(c_ref, w_ref, b_ref, o_ref):
    o_ref[...] = _dot_f32(_silu(c_ref[...]), w_ref[...]) + b_ref[...]


def _modulation(cvec, w_mod, b_mod):
    tn = 1024
    n = w_mod.shape[1]
    return pl.pallas_call(
        _mod_kernel,
        grid=(n // tn,),
        in_specs=[pl.BlockSpec((8, D_MODEL), lambda j: (0, 0)),
                  pl.BlockSpec((D_MODEL, tn), lambda j: (0, j)),
                  pl.BlockSpec((1, tn), lambda j: (0, j))],
        out_specs=pl.BlockSpec((8, tn), lambda j: (0, j)),
        out_shape=jax.ShapeDtypeStruct((8, n), F32),
        compiler_params=_cparams(("arbitrary",)),
        name="modulation",
    )(cvec, w_mod, b_mod.reshape(1, n))


def _ln_mod_kernel(xc_ref, xd_ref, m_ref, o_ref, *, n_ctx):
    i = pl.program_id(0)
    x = jnp.where(i < n_ctx, xc_ref[...], xd_ref[...])
    y = _layer_norm(x) * (1.0 + m_ref[0, 1:2, :]) + m_ref[0, 0:1, :]
    o_ref[...] = y.astype(o_ref.dtype)


def _ln_modulate(x_ctx, x_dec, mods):
    tm = 512
    n_ctx = T_CTX // tm
    return pl.pallas_call(
        functools.partial(_ln_mod_kernel, n_ctx=n_ctx),
        grid=(T_ALL // tm,),
        in_specs=[pl.BlockSpec((tm, D_MODEL), lambda i: (jnp.minimum(i, n_ctx - 1), 0)),
                  pl.BlockSpec((tm, D_MODEL), lambda i: (jnp.maximum(i - n_ctx, 0), 0)),
                  pl.BlockSpec((1, 6, D_MODEL), lambda i: (_mod_row(i, tm), 0, 0))],
        out_specs=pl.BlockSpec((tm, D_MODEL), lambda i: (i, 0)),
        out_shape=jax.ShapeDtypeStruct((T_ALL, D_MODEL), BF16),
        compiler_params=_cparams(("arbitrary",)),
        name="ln_modulate",
    )(x_ctx, x_dec, mods)


def _mm_kernel(x_ref, w_ref, o_ref):
    o_ref[...] = _dot(x_ref[...], w_ref[...]).astype(o_ref.dtype)


def _matmul(x, w, *, rows, row_block0=0, col_block0=0, n_out=None, tm=1024, tn=1024,
            out_dtype=F32, name="matmul"):
    k = x.shape[1]
    n_out = w.shape[1] if n_out is None else n_out
    return pl.pallas_call(
        _mm_kernel,
        grid=(rows // tm, n_out // tn),
        in_specs=[pl.BlockSpec((tm, k), lambda i, j: (i + row_block0, 0)),
                  pl.BlockSpec((k, tn), lambda i, j: (0, j + col_block0))],
        out_specs=pl.BlockSpec((tm, tn), lambda i, j: (i, j)),
        out_shape=jax.ShapeDtypeStruct((rows, n_out), out_dtype),
        compiler_params=_cparams(("arbitrary", "arbitrary")),
        name=name,
    )(x, w)


def _ctx_attn_kernel(q_ref, k_ref, v_ref, o_ref):
    scale = NA_HEAD_DIM ** -0.5
    for h in range(NA_HEADS):
        sl = slice(h * NA_HEAD_DIM, (h + 1) * NA_HEAD_DIM)
        q = q_ref[:, sl].astype(BF16)
        k = k_ref[:, sl].astype(BF16)
        v = v_ref[:, sl].astype(BF16)
        s = _dot_nt(q, k) * scale
        m = jnp.max(s, axis=-1, keepdims=True)
        p = jnp.exp(s - m)
        l = jnp.sum(p, axis=-1, keepdims=True)
        o_ref[:, sl] = _dot(p.astype(BF16), v) / l


def _context_attention(q_all, k_ctx, v_ctx):
    spec = pl.BlockSpec((SEQ, NA_WIDTH), lambda b: (b, 0))
    return pl.pallas_call(
        _ctx_attn_kernel,
        grid=(BATCH,),
        in_specs=[spec, spec, spec],
        out_specs=spec,
        out_shape=jax.ShapeDtypeStruct((T_CTX, NA_WIDTH), F32),
        compiler_params=_cparams(("arbitrary",)),
        name="context_attention",
    )(q_all, k_ctx, v_ctx)


def _nbr_attn_kernel(q_ref, k_ref, v_ref, kc_ref, vc_ref, rc_ref, o_ref):
    scale = NA_HEAD_DIM ** -0.5
    kr = min(NA_WIN_R, ROWS)
    kc = kc_ref[0].astype(BF16)
    vc = vc_ref[0].astype(BF16)
    for r in range(ROWS):
        r0 = min(max(r - kr // 2, 0), ROWS - kr)
        off = (r0 - r + NA_WIN_R - 1) * GRID_W
        q = q_ref[r * GRID_W:(r + 1) * GRID_W, :].astype(BF16)
        kb = k_ref[r0 * GRID_W:(r0 + kr) * GRID_W, :].astype(BF16)
        vb = v_ref[r0 * GRID_W:(r0 + kr) * GRID_W, :].astype(BF16)
        s_loc = _dot_nt(q, kb) * scale + rc_ref[0, :, off:off + kr * GRID_W]
        s_ctx = _dot_nt(q, kc) * scale
        m = jnp.maximum(jnp.max(s_loc, axis=-1, keepdims=True),
                        jnp.max(s_ctx, axis=-1, keepdims=True))
        p_loc = jnp.exp(s_loc - m)
        p_ctx = jnp.exp(s_ctx - m)
        l = jnp.sum(p_loc, axis=-1, keepdims=True) + jnp.sum(p_ctx, axis=-1, keepdims=True)
        o = _dot(p_loc.astype(BF16), vb) + _dot(p_ctx.astype(BF16), vc)
        o_ref[r * GRID_W:(r + 1) * GRID_W, :] = o / l


def _rpb_table(rpb):
    col = jnp.arange(GRID_W)
    c0 = jnp.clip(col - NA_WIN_C // 2, 0, GRID_W - NA_WIN_C)
    col_mask = (col[None, :] >= c0[:, None]) & (col[None, :] < c0[:, None] + NA_WIN_C)
    dc_idx = jnp.clip(col[None, :] - col[:, None] + NA_WIN_C - 1, 0, 2 * NA_WIN_C - 2)
    pick = (dc_idx[:, :, None] == jnp.arange(2 * NA_WIN_C - 1)[None, None, :]).astype(F32)
    t = jnp.einsum('hdj,qkj->hqdk', rpb, pick, precision=lax.Precision.HIGHEST)
    t = jnp.where(col_mask[None, :, None, :], t, -jnp.inf)
    t = jnp.pad(t, ((0, 0), (0, 0), (0, RPB_SLOTS - t.shape[2]), (0, 0)))
    return t.reshape(NA_HEADS, GRID_W, RPB_SLOTS * GRID_W)


def _neighbourhood_attention(q_all, k_dec, v_dec, kc, vc, rc):
    q_row0 = T_CTX // DEC_SEQ
    hd = NA_HEAD_DIM
    return pl.pallas_call(
        _nbr_attn_kernel,
        grid=(DEC_BATCH, NA_HEADS),
        in_specs=[pl.BlockSpec((DEC_SEQ, hd), lambda b, h: (b + q_row0, h)),
                  pl.BlockSpec((DEC_SEQ, hd), lambda b, h: (b, h)),
                  pl.BlockSpec((DEC_SEQ, hd), lambda b, h: (b, h)),
                  pl.BlockSpec((1, PAST_LEN, hd), lambda b, h: (b, 0, h)),
                  pl.BlockSpec((1, PAST_LEN, hd), lambda b, h: (b, 0, h)),
                  pl.BlockSpec((1, GRID_W, RPB_SLOTS * GRID_W), lambda b, h: (h, 0, 0))],
        out_specs=pl.BlockSpec((DEC_SEQ, hd), lambda b, h: (b, h)),
        out_shape=jax.ShapeDtypeStruct((T_DEC, NA_WIDTH), F32),
        compiler_params=_cparams(("arbitrary", "arbitrary")),
        name="neighbourhood_attention",
    )(q_all, k_dec, v_dec, kc, vc, rc)


CONV_HALO = 8


def _conv_silu(u_ref, w_ref, b_ref, pad_s, length):
    width = u_ref.shape[1]
    halo = jnp.zeros((CONV_HALO, width), F32)
    pad_s[0:CONV_HALO, 0:width] = halo
    pad_s[CONV_HALO + length:2 * CONV_HALO + length, 0:width] = halo
    pad_s[CONV_HALO:CONV_HALO + length, 0:width] = u_ref[...]
    acc = b_ref[...]
    for k in range(SSD_CONV):
        d = k - SSD_CONV // 2
        acc = acc + pad_s[CONV_HALO + d:CONV_HALO + d + length, 0:width] * w_ref[k:k + 1, :]
    return _silu(acc)


def _ssd_kernel(*refs, length, has_h0, emit_state):
    (xs_ref, b_ref, c_ref, z_ref, dt_ref, dtt_ref, cwx_ref, cwb_ref, cwc_ref,
     cbx_ref, cbb_ref, cbc_ref, dtb_row_ref, dtb_col_ref, al_row_ref, al_col_ref,
     e_ref, d_ref) = refs[:18]
    pos = 18
    if has_h0:
        h0f_ref, h0b_ref = refs[pos:pos + 2]
        pos += 2
    u_ref = refs[pos]
    pos += 1
    if emit_state:
        hf_ref, hb_ref = refs[pos:pos + 2]
        pos += 2
    xs_s, b_s, c_s, y_s, st_s, pad_s = refs[pos:]

    q = SSD_CHUNK
    nc = length // q
    xs_s[...] = _conv_silu(xs_ref, cwx_ref, cbx_ref, pad_s, length)
    b_s[...] = _conv_silu(b_ref, cwb_ref, cbb_ref, pad_s, length)
    c_s[...] = _conv_silu(c_ref, cwc_ref, cbc_ref, pad_s, length)
    y_s[...] = d_ref[0] * xs_s[...]

    for dirn in range(2):
        if has_h0:
            h0 = (h0f_ref if dirn == 0 else h0b_ref)[0, 0]
            st_s[dirn] = jnp.transpose(h0.reshape(GROUP_W, SSD_STATE))
        else:
            st_s[dirn] = jnp.zeros((SSD_STATE, GROUP_W), F32)

    ri = lax.broadcasted_iota(jnp.int32, (q, q), 0)
    ci = lax.broadcasted_iota(jnp.int32, (q, q), 1)
    lower = ri >= ci
    upper = ri <= ci
    lower_bf = jnp.where(lower, 1.0, 0.0).astype(BF16)
    upper_bf = jnp.where(upper, 1.0, 0.0).astype(BF16)

    def chunk(c, dirn):
        r0 = c * q if isinstance(c, int) else pl.multiple_of(c * q, q)
        tri = lower if dirn == 0 else upper
        xs_c = xs_s[pl.ds(r0, q), :]
        bc = b_s[pl.ds(r0, q), :]
        cc = c_s[pl.ds(r0, q), :]
        dtp = _softplus(dt_ref[pl.ds(r0, q), :] + dtb_row_ref[...])
        da = dtp * (-jnp.exp(al_row_ref[...]))
        pre = _dot_exact_lhs(lower_bf, da)
        cs = pre if dirn == 0 else pre[q - 1:q, :] - pre + da
        e_bf = e_ref[dirn, 0]
        dt_e = _dot_hi_mid_rhs(dtp, e_bf)
        cs_e = _dot_exact_rhs(cs, e_bf)
        dt_t = _softplus(dtt_ref[c, dirn, 0] + dtb_col_ref[dirn, 0])
        da_t = dt_t * (-jnp.exp(al_col_ref[dirn, 0]))
        cs_t = _dot_exact_rhs(da_t, upper_bf if dirn == 0 else lower_bf)

        cb = _dot_nt(cc.astype(BF16), bc.astype(BF16))
        bt = jnp.transpose(bc).astype(BF16)
        end = q - 1 if dirn == 0 else 0
        cs_end = cs_e[end:end + 1, :]
        xdt = xs_c * dt_e
        st = st_s[dirn]
        y = _dot(cc.astype(BF16), st.astype(BF16)) * jnp.exp(cs_e)
        st_s[dirn] = jnp.exp(cs_end) * st + _dot(bt, (xdt * jnp.exp(cs_end - cs_e)).astype(BF16))
        xdt_bf = xdt.astype(BF16)
        parts = []
        for r in range(HEADS_PER_GROUP):
            sl = slice(r * SSD_HEAD_DIM, (r + 1) * SSD_HEAD_DIM)
            diff = cs_e[:, r * SSD_HEAD_DIM:r * SSD_HEAD_DIM + 1] - cs_t[r:r + 1, :]
            lm = jnp.exp(jnp.where(tri, diff, -jnp.inf))
            parts.append(_dot((cb * lm).astype(BF16), xdt_bf[:, sl]))
        y = y + jnp.concatenate(parts, axis=-1)
        y_s[pl.ds(r0, q), :] = y_s[pl.ds(r0, q), :] + y

    if nc <= 2:
        for i in range(nc):
            chunk(i, 0)
            chunk(nc - 1 - i, 1)
    else:
        def both(i, carry):
            chunk(i, 0)
            chunk(nc - 1 - i, 1)
            return carry
        lax.fori_loop(0, nc, both, 0)

    u_ref[...] = y_s[...] * _silu(z_ref[...])
    if emit_state:
        hf_ref[0, 0] = jnp.transpose(st_s[0]).reshape(HEADS_PER_GROUP, SSD_HEAD_DIM, SSD_STATE)
        hb_ref[0, 0] = jnp.transpose(st_s[1]).reshape(HEADS_PER_GROUP, SSD_HEAD_DIM, SSD_STATE)


def _ssd_mixer(zx, dt_all, dt_t, consts, *, length, n_seq, seq_block0, h0=None, emit_state=False):
    (cw, cb, dtb_row, dtb_col, al_row, al_col, expand, d_row) = consts
    g_w, n_s = GROUP_W, SSD_STATE
    xs_cb0 = D_MODEL // g_w
    b_cb0 = (D_MODEL + SSD_WIDTH) // n_s
    c_cb0 = b_cb0 + SSD_GROUPS
    cw_b0 = SSD_WIDTH // n_s
    nck = length // SSD_CHUNK
    in_specs = [
        pl.BlockSpec((length, g_w), lambda s, g: (s + seq_block0, xs_cb0 + g)),
        pl.BlockSpec((length, n_s), lambda s, g: (s + seq_block0, b_cb0 + g)),
        pl.BlockSpec((length, n_s), lambda s, g: (s + seq_block0, c_cb0 + g)),
        pl.BlockSpec((length, g_w), lambda s, g: (s + seq_block0, g)),
        pl.BlockSpec((length, 128), lambda s, g: (s + seq_block0, 0)),
        pl.BlockSpec((nck, 2, 1, HEADS_PER_GROUP, SSD_CHUNK), lambda s, g: (s + seq_block0, 0, g, 0, 0)),
        pl.BlockSpec((SSD_CONV, g_w), lambda s, g: (0, g)),
        pl.BlockSpec((SSD_CONV, n_s), lambda s, g: (0, cw_b0 + g)),
        pl.BlockSpec((SSD_CONV, n_s), lambda s, g: (0, cw_b0 + SSD_GROUPS + g)),
        pl.BlockSpec((1, g_w), lambda s, g: (0, g)),
        pl.BlockSpec((1, n_s), lambda s, g: (0, cw_b0 + g)),
        pl.BlockSpec((1, n_s), lambda s, g: (0, cw_b0 + SSD_GROUPS + g)),
        pl.BlockSpec((1, 128), lambda s, g: (0, 0)),
        pl.BlockSpec((2, 1, HEADS_PER_GROUP, SSD_CHUNK), lambda s, g: (0, g, 0, 0)),
        pl.BlockSpec((1, 128), lambda s, g: (0, 0)),
        pl.BlockSpec((2, 1, HEADS_PER_GROUP, SSD_CHUNK), lambda s, g: (0, g, 0, 0)),
        pl.BlockSpec((2, 1, 128, g_w), lambda s, g: (0, g, 0, 0)),
        pl.BlockSpec((1, 1, g_w), lambda s, g: (g, 0, 0)),
    ]
    args = [zx, zx, zx, zx, dt_all, dt_t, cw, cw, cw, cb, cb, cb,
            dtb_row, dtb_col, al_row, al_col, expand, d_row]
    st_spec = pl.BlockSpec((1, 1, HEADS_PER_GROUP, SSD_HEAD_DIM, n_s), lambda s, g: (s, g, 0, 0, 0))
    if h0 is not None:
        in_specs += [st_spec, st_spec]
        args += list(h0)
    out_specs = [pl.BlockSpec((length, g_w), lambda s, g: (s, g))]
    out_shape = [jax.ShapeDtypeStruct((n_seq * length, SSD_WIDTH), F32)]
    if emit_state:
        st_shape = jax.ShapeDtypeStruct((n_seq, SSD_GROUPS, HEADS_PER_GROUP, SSD_HEAD_DIM, n_s), F32)
        out_specs += [st_spec, st_spec]
        out_shape += [st_shape, st_shape]
    return pl.pallas_call(
        functools.partial(_ssd_kernel, length=length, has_h0=h0 is not None, emit_state=emit_state),
        grid=(n_seq, SSD_GROUPS),
        in_specs=in_specs,
        out_specs=out_specs,
        out_shape=out_shape,
        scratch_shapes=[pltpu.VMEM((length, g_w), F32), pltpu.VMEM((length, n_s), F32),
                        pltpu.VMEM((length, n_s), F32), pltpu.VMEM((length, g_w), F32),
                        pltpu.VMEM((2, n_s, g_w), F32),
                        pltpu.VMEM((length + 2 * CONV_HALO, g_w), F32)],
        compiler_params=_cparams(("arbitrary", "arbitrary")),
        name="ssd_mixer_%d" % length,
    )(*args)


def _ssd_consts(conv_w, conv_b, dt_bias, a_log, d_skip):
    hpg = HEADS_PER_GROUP
    dtb_row = jnp.pad(dt_bias.reshape(1, 2 * SSD_HEADS), ((0, 0), (0, 128 - 2 * SSD_HEADS)))
    col = lambda p: jnp.broadcast_to(p.reshape(2, SSD_GROUPS, hpg, 1), (2, SSD_GROUPS, hpg, SSD_CHUNK))
    al_row = jnp.pad(a_log.reshape(1, 2 * SSD_HEADS), ((0, 0), (0, 128 - 2 * SSD_HEADS)))
    src = (jnp.arange(2)[:, None, None] * SSD_HEADS + jnp.arange(SSD_GROUPS)[None, :, None] * hpg
           + jnp.arange(GROUP_W)[None, None, :] // SSD_HEAD_DIM)
    expand = (jnp.arange(128)[None, None, :, None] == src[:, :, None, :]).astype(BF16)
    d_row = jnp.repeat(d_skip.reshape(SSD_GROUPS, 1, hpg), SSD_HEAD_DIM, axis=-1)
    return (conv_w, conv_b.reshape(1, SSD_CONV_CH), dtb_row, col(dt_bias), al_row, col(a_log), expand, d_row)


def _merge_kernel(nac_ref, nad_ref, uc_ref, ud_ref, nw_ref, w1_ref, w2_ref, g_na_ref, g_ssd_ref,
                  o_ref, *, n_ctx):
    is_ctx = pl.program_id(0) < n_ctx
    a1 = jnp.where(is_ctx, nac_ref[...], nad_ref[...]).astype(BF16)
    u = jnp.where(is_ctx, uc_ref[...], ud_ref[...])
    r = lax.rsqrt(jnp.mean(u * u, axis=-1, keepdims=True) + LN_EPS)
    a2 = (u * r * nw_ref[...]).astype(BF16)
    o = (jax.nn.sigmoid(g_na_ref[...]) * _dot(a1, w1_ref[...])
         + jax.nn.sigmoid(g_ssd_ref[...]) * _dot(a2, w2_ref[...]))
    o_ref[...] = o.astype(o_ref.dtype)


def _merge_branches(na_ctx, na_dec, u_ctx, u_dec, norm_w, w_na_out, w_ssd_out, gates):
    tm = 256
    n_ctx = T_CTX // tm
    ctx_rows = lambda i: (jnp.minimum(i, n_ctx - 1), 0)
    dec_rows = lambda i: (jnp.maximum(i - n_ctx, 0), 0)
    const = lambda i: (0, 0)
    resident = pl.Buffered(1)
    return pl.pallas_call(
        functools.partial(_merge_kernel, n_ctx=n_ctx),
        grid=(T_ALL // tm,),
        in_specs=[pl.BlockSpec((tm, D_MODEL), ctx_rows), pl.BlockSpec((tm, D_MODEL), dec_rows),
                  pl.BlockSpec((tm, D_MODEL), ctx_rows), pl.BlockSpec((tm, D_MODEL), dec_rows),
                  pl.BlockSpec((1, D_MODEL), const),
                  pl.BlockSpec((D_MODEL, D_MODEL), const, pipeline_mode=resident),
                  pl.BlockSpec((D_MODEL, D_MODEL), const, pipeline_mode=resident),
                  pl.BlockSpec((tm, D_MODEL), lambda i: (i, 1)),
                  pl.BlockSpec((tm, D_MODEL), lambda i: (i, 0))],
        out_specs=pl.BlockSpec((tm, D_MODEL), lambda i: (i, 0)),
        out_shape=jax.ShapeDtypeStruct((T_ALL, D_MODEL), BF16),
        compiler_params=_cparams(("arbitrary",)),
        name="merge_branches",
    )(na_ctx, na_dec, u_ctx, u_dec, norm_w, w_na_out, w_ssd_out, gates, gates)


def _post_mix_kernel(y_ref, wo_ref, xc_ref, xd_ref, m_ref, g_ref, b_ref, wr_ref, br_ref,
                     x1_ref, h2_ref, idx_ref, wgt_ref, *, n_ctx):
    i = pl.program_id(0)
    x = jnp.where(i < n_ctx, xc_ref[...], xd_ref[...])
    mix = _dot(y_ref[...], wo_ref[...])
    x1 = _layer_norm(DN_ALPHA * x + m_ref[0, 2:3, :] * mix) * g_ref[...] + b_ref[...]
    x1_ref[...] = x1
    h2 = _layer_norm(x1) * (1.0 + m_ref[0, 4:5, :]) + m_ref[0, 3:4, :]
    h2_ref[...] = h2
    logits = _dot_nt_f32(wr_ref[...], h2) + br_ref[...]
    eidx = lax.broadcasted_iota(jnp.int32, logits.shape, 0)
    vals, idxs = [], []
    for _ in range(TOP_K):
        m = jnp.max(logits, axis=0, keepdims=True)
        sel = jnp.min(jnp.where(logits == m, eidx, N_EXPERTS), axis=0, keepdims=True)
        logits = jnp.where(eidx == sel, -jnp.inf, logits)
        vals.append(m)
        idxs.append(sel)
    ex = [jnp.exp(v - vals[0]) for v in vals]
    tot = ex[0] + ex[1] + ex[2] + ex[3]
    idx_ref[...] = jnp.concatenate(idxs, axis=0)
    wgt_ref[...] = jnp.concatenate([e / tot for e in ex], axis=0)


def _post_mix(y, w_o, x_ctx, x_dec, mods, ln_g, ln_b, w_router_t, b_router):
    tm = 512
    n_ctx = T_CTX // tm
    row = lambda i: (i, 0)
    const = lambda i: (0, 0)
    return pl.pallas_call(
        functools.partial(_post_mix_kernel, n_ctx=n_ctx),
        grid=(T_ALL // tm,),
        in_specs=[pl.BlockSpec((tm, D_MODEL), row),
                  pl.BlockSpec((D_MODEL, D_MODEL), const, pipeline_mode=pl.Buffered(1)),
                  pl.BlockSpec((tm, D_MODEL), lambda i: (jnp.minimum(i, n_ctx - 1), 0)),
                  pl.BlockSpec((tm, D_MODEL), lambda i: (jnp.maximum(i - n_ctx, 0), 0)),
                  pl.BlockSpec((1, 6, D_MODEL), lambda i: (_mod_row(i, tm), 0, 0)),
                  pl.BlockSpec((1, D_MODEL), const), pl.BlockSpec((1, D_MODEL), const),
                  pl.BlockSpec((N_EXPERTS, D_MODEL), const), pl.BlockSpec((N_EXPERTS, 1), const)],
        out_specs=[pl.BlockSpec((tm, D_MODEL), row), pl.BlockSpec((tm, D_MODEL), row),
                   pl.BlockSpec((TOP_K, tm), lambda i: (0, i)), pl.BlockSpec((TOP_K, tm), lambda i: (0, i))],
        out_shape=[jax.ShapeDtypeStruct((T_ALL, D_MODEL), F32), jax.ShapeDtypeStruct((T_ALL, D_MODEL), F32),
                   jax.ShapeDtypeStruct((TOP_K, T_ALL), jnp.int32), jax.ShapeDtypeStruct((TOP_K, T_ALL), F32)],
        compiler_params=_cparams(("arbitrary",)),
        name="post_mix_router",
    )(y, w_o, x_ctx, x_dec, mods, ln_g, ln_b, w_router_t, b_router)


DISPATCH_TOK = 128


def _tile_rows(t):
    return pl.ds(pl.multiple_of(t * MOE_TM, MOE_TM), MOE_TM)


def _dispatch_kernel(t0_ref, nt_ref, nv_ref, pos_ref, x_hbm, xs_hbm, buf, zero_s, sem_in, sem_out, sem_z,
                     *, n_steps):
    i = pl.program_id(0)
    slot = i % 3

    def zero_copy(tile):
        return pltpu.make_async_copy(zero_s, xs_hbm.at[_tile_rows(tile)], sem_z)

    def in_copy(step, s):
        rows = pl.ds(pl.multiple_of(step * DISPATCH_TOK, DISPATCH_TOK), DISPATCH_TOK)
        return pltpu.make_async_copy(x_hbm.at[rows], buf.at[s], sem_in.at[s])

    def row_copy(s, t, dst):
        return pltpu.make_async_copy(buf.at[s, pl.ds(t, 1)], xs_hbm.at[pl.ds(dst, 1)], sem_out.at[s])

    def wait_rows(s):
        for _ in range(TOP_K * DISPATCH_TOK):
            row_copy(s, 0, 0).wait()

    @pl.when(i == 0)
    def _():
        in_copy(0, 0).start()
        zero_s[...] = jnp.zeros(zero_s.shape, zero_s.dtype)
        for e in range(N_EXPERTS):
            @pl.when(nt_ref[e] > 0)
            def _():
                zero_copy(t0_ref[e] + nt_ref[e] - 1).start()

        def tail_start(t, c):
            zero_copy(t).start()
            return c
        lax.fori_loop(nv_ref[0], MOE_TILES, tail_start, 0)
        for e in range(N_EXPERTS):
            @pl.when(nt_ref[e] > 0)
            def _():
                zero_copy(0).wait()

        def tail_wait(t, c):
            zero_copy(0).wait()
            return c
        lax.fori_loop(nv_ref[0], MOE_TILES, tail_wait, 0)

    in_copy(i, slot).wait()

    @pl.when(i + 1 < n_steps)
    def _():
        in_copy(i + 1, (i + 1) % 3).start()

    for k in range(TOP_K):
        for t in range(DISPATCH_TOK):
            row_copy(slot, t, pos_ref[0, k, t]).start(priority=t % 2)

    @pl.when(i > 0)
    def _():
        wait_rows((i + 2) % 3)

    @pl.when(i == n_steps - 1)
    def _():
        wait_rows(slot)


def _dispatch(h2, pos_blocks, tile_start, tiles, n_valid):
    n_steps = T_ALL // DISPATCH_TOK
    grid_spec = pltpu.PrefetchScalarGridSpec(
        num_scalar_prefetch=3,
        grid=(n_steps,),
        in_specs=[pl.BlockSpec((1, TOP_K, DISPATCH_TOK), lambda i, *_: (i, 0, 0), memory_space=pltpu.SMEM),
                  pl.BlockSpec(memory_space=pl.ANY)],
        out_specs=pl.BlockSpec(memory_space=pl.ANY),
        scratch_shapes=[pltpu.VMEM((3, DISPATCH_TOK, D_MODEL), F32),
                        pltpu.VMEM((MOE_TM, D_MODEL), F32),
                        pltpu.SemaphoreType.DMA((3,)), pltpu.SemaphoreType.DMA((3,)),
                        pltpu.SemaphoreType.DMA(())],
    )
    return pl.pallas_call(
        functools.partial(_dispatch_kernel, n_steps=n_steps),
        grid_spec=grid_spec,
        out_shape=jax.ShapeDtypeStruct((MOE_ROWS, D_MODEL), F32),
        compiler_params=_cparams(("arbitrary",)),
        name="moe_dispatch",
    )(tile_start, tiles, n_valid, pos_blocks, h2)


def _expert_tile_loop(t0, nt, in_copy, out_copy, compute):
    @pl.when(nt > 0)
    def _():
        in_copy(t0, 0).start(priority=1)

    def body(i, carry):
        slot = i % 2
        in_copy(t0 + i, slot).wait()

        @pl.when(i + 1 < nt)
        def _():
            in_copy(t0 + i + 1, 1 - slot).start(priority=1)

        @pl.when(i >= 2)
        def _():
            out_copy(t0 + i - 2, slot).wait()

        compute(slot)
        out_copy(t0 + i, slot).start(priority=1)
        return carry

    lax.fori_loop(0, nt, body, 0)

    @pl.when(nt >= 2)
    def _():
        out_copy(t0, nt % 2).wait()

    @pl.when(nt >= 1)
    def _():
        out_copy(t0, (nt + 1) % 2).wait()


def _zero_tail_tiles(is_last, n_valid, obuf, out_copy):
    @pl.when(is_last)
    def _():
        obuf[0] = jnp.zeros(obuf.shape[1:], obuf.dtype)

        def body(t, carry):
            out_copy(t, 0).start()
            out_copy(t, 0).wait()
            return carry
        lax.fori_loop(n_valid, MOE_TILES, body, 0)


def _cast_weight_parts(parts, w_s):
    rows = w_s.shape[0] // len(parts)
    for r, part in enumerate(parts):
        w_s[r * rows:(r + 1) * rows, :] = part[0].astype(BF16)


def _gate_up_kernel(t0_ref, nt_ref, nv_ref, xs_hbm, *refs, tn):
    wg_parts, wu_parts = refs[:W_SPLIT], refs[W_SPLIT:2 * W_SPLIT]
    bg_ref, bu_ref, hid_hbm, wg_s, wu_s, xbuf, obuf, sem_in, sem_out = refs[2 * W_SPLIT:]
    j = pl.program_id(0)
    e = pl.program_id(1)
    _cast_weight_parts(wg_parts, wg_s)
    _cast_weight_parts(wu_parts, wu_s)

    def in_copy(t, slot):
        return pltpu.make_async_copy(xs_hbm.at[_tile_rows(t)], xbuf.at[slot], sem_in.at[slot])

    def out_copy(t, slot):
        cols = pl.ds(pl.multiple_of(j * tn, tn), tn)
        return pltpu.make_async_copy(obuf.at[slot], hid_hbm.at[_tile_rows(t), cols], sem_out.at[slot])

    def compute(slot):
        x = xbuf[slot].astype(BF16)
        gate = jnp.minimum(_dot(x, wg_s[...]) + bg_ref[0], SWIGLU_LIMIT)
        up = jnp.clip(_dot(x, wu_s[...]) + bu_ref[0], -SWIGLU_LIMIT, SWIGLU_LIMIT)
        hid = (up + 1.0) * gate * jax.nn.sigmoid(SWIGLU_ALPHA * gate)
        obuf[slot] = hid.astype(obuf.dtype)

    _expert_tile_loop(t0_ref[e], nt_ref[e], in_copy, out_copy, compute)
    _zero_tail_tiles(e == N_EXPERTS - 1, nv_ref[0], obuf, out_copy)


def _moe_gate_up(xs, tile_start, tiles, n_valid, w_gate_up, b_gate_up):
    tn = 1024
    nj = D_EXPERT // tn
    rows = D_MODEL // W_SPLIT
    grid_spec = pltpu.PrefetchScalarGridSpec(
        num_scalar_prefetch=3,
        grid=(nj, N_EXPERTS),
        in_specs=[pl.BlockSpec(memory_space=pl.ANY)]
        + [pl.BlockSpec((1, rows, tn), lambda j, e, *_, r=r: (e, r, j)) for r in range(W_SPLIT)]
        + [pl.BlockSpec((1, rows, tn), lambda j, e, *_, r=r: (e, r, nj + j)) for r in range(W_SPLIT)]
        + [pl.BlockSpec((1, 1, tn), lambda j, e, *_: (e, 0, j)),
           pl.BlockSpec((1, 1, tn), lambda j, e, *_: (e, 0, nj + j))],
        out_specs=pl.BlockSpec(memory_space=pl.ANY),
        scratch_shapes=[pltpu.VMEM((D_MODEL, tn), BF16), pltpu.VMEM((D_MODEL, tn), BF16),
                        pltpu.VMEM((2, MOE_TM, D_MODEL), F32), pltpu.VMEM((2, MOE_TM, tn), BF16),
                        pltpu.SemaphoreType.DMA((2,)), pltpu.SemaphoreType.DMA((2,))],
    )
    return pl.pallas_call(
        functools.partial(_gate_up_kernel, tn=tn),
        grid_spec=grid_spec,
        out_shape=jax.ShapeDtypeStruct((MOE_ROWS, D_EXPERT), BF16),
        compiler_params=_cparams(("arbitrary", "arbitrary")),
        name="moe_gate_up",
    )(tile_start, tiles, n_valid, xs, *([w_gate_up] * (2 * W_SPLIT)), b_gate_up, b_gate_up)


def _down_kernel(t0_ref, nt_ref, nv_ref, hid_hbm, *refs):
    w_parts = refs[:W_SPLIT]
    b_ref, ys_hbm, w_s, hbuf, obuf, sem_in, sem_out = refs[W_SPLIT:]
    e = pl.program_id(0)
    _cast_weight_parts(w_parts, w_s)

    def in_copy(t, slot):
        return pltpu.make_async_copy(hid_hbm.at[_tile_rows(t)], hbuf.at[slot], sem_in.at[slot])

    def out_copy(t, slot):
        return pltpu.make_async_copy(obuf.at[slot], ys_hbm.at[_tile_rows(t)], sem_out.at[slot])

    def compute(slot):
        obuf[slot] = _dot(hbuf[slot], w_s[...]) + b_ref[0]

    _expert_tile_loop(t0_ref[e], nt_ref[e], in_copy, out_copy, compute)
    _zero_tail_tiles(e == N_EXPERTS - 1, nv_ref[0], obuf, out_copy)


def _moe_down(hid, tile_start, tiles, n_valid, w_down, b_down):
    rows = D_EXPERT // W_SPLIT
    grid_spec = pltpu.PrefetchScalarGridSpec(
        num_scalar_prefetch=3,
        grid=(N_EXPERTS,),
        in_specs=[pl.BlockSpec(memory_space=pl.ANY)]
        + [pl.BlockSpec((1, rows, D_MODEL), lambda e, *_, r=r: (e, r, 0)) for r in range(W_SPLIT)]
        + [pl.BlockSpec((1, 1, D_MODEL), lambda e, *_: (e, 0, 0))],
        out_specs=pl.BlockSpec(memory_space=pl.ANY),
        scratch_shapes=[pltpu.VMEM((D_EXPERT, D_MODEL), BF16),
                        pltpu.VMEM((2, MOE_TM, D_EXPERT), BF16), pltpu.VMEM((2, MOE_TM, D_MODEL), F32),
                        pltpu.SemaphoreType.DMA((2,)), pltpu.SemaphoreType.DMA((2,))],
    )
    return pl.pallas_call(
        _down_kernel,
        grid_spec=grid_spec,
        out_shape=jax.ShapeDtypeStruct((MOE_ROWS, D_MODEL), F32),
        compiler_params=_cparams(("arbitrary",)),
        name="moe_down",
    )(tile_start, tiles, n_valid, hid, *([w_down] * W_SPLIT), b_down)


COMBINE_TOK = 64


def _start_rows(src_hbm, idx_ref, idx_lead, dst, sem, count):
    for r in range(count):
        pltpu.make_async_copy(src_hbm.at[pl.ds(idx_ref[idx_lead + (r,)], 1)], dst.at[pl.ds(r, 1)],
                              sem).start(priority=r % 2)


def _wait_rows(src_hbm, dst, sem, count):
    for _ in range(count):
        pltpu.make_async_copy(src_hbm.at[pl.ds(0, 1)], dst.at[pl.ds(0, 1)], sem).wait()


def _combine_kernel(pos_ref, pos_next_ref, ys_hbm, wgt_ref, x1_ref, m_ref, g_ref, b_ref, o_ref, buf, sem,
                    *, n_steps):
    i = pl.program_id(0)
    slot = i % 2

    def start(p_ref, s):
        for k in range(TOP_K):
            _start_rows(ys_hbm, p_ref, (0, k), buf.at[s, k], sem.at[s], COMBINE_TOK)

    @pl.when(i == 0)
    def _():
        start(pos_ref, 0)

    @pl.when(i + 1 < n_steps)
    def _():
        start(pos_next_ref, 1 - slot)

    _wait_rows(ys_hbm, buf.at[slot, 0], sem.at[slot], TOP_K * COMBINE_TOK)
    w = wgt_ref[...]
    ffn = w[:, 0:1] * buf[slot, 0]
    for k in range(1, TOP_K):
        ffn = ffn + w[:, k:k + 1] * buf[slot, k]
    x2 = _layer_norm(DN_ALPHA * x1_ref[...] + m_ref[0, 5:6, :] * ffn) * g_ref[...] + b_ref[...]
    o_ref[...] = x2


def _combine(ys, pos_blocks, wgt, x1, mods, ln_g, ln_b, *, rows, row0):
    tm = COMBINE_TOK
    b0 = row0 // tm
    n = rows // tm
    const = lambda i: (0, 0)
    pos_spec = lambda f: pl.BlockSpec((1, TOP_K, tm), f, memory_space=pltpu.SMEM)
    return pl.pallas_call(
        functools.partial(_combine_kernel, n_steps=n),
        grid=(n,),
        in_specs=[pos_spec(lambda i: (i + b0, 0, 0)),
                  pos_spec(lambda i: (jnp.minimum(i + 1, n - 1) + b0, 0, 0)),
                  pl.BlockSpec(memory_space=pl.ANY),
                  pl.BlockSpec((tm, TOP_K), lambda i: (i + b0, 0)),
                  pl.BlockSpec((tm, D_MODEL), lambda i: (i + b0, 0)),
                  pl.BlockSpec((1, 6, D_MODEL), lambda i: (_mod_row(i + b0, tm), 0, 0)),
                  pl.BlockSpec((1, D_MODEL), const), pl.BlockSpec((1, D_MODEL), const)],
        out_specs=pl.BlockSpec((tm, D_MODEL), lambda i: (i, 0)),
        out_shape=jax.ShapeDtypeStruct((rows, D_MODEL), F32),
        scratch_shapes=[pltpu.VMEM((2, TOP_K, tm, D_MODEL), F32), pltpu.SemaphoreType.DMA((2,))],
        compiler_params=_cparams(("arbitrary",)),
        name="moe_combine_ln2",
    )(pos_blocks, pos_blocks, ys, wgt, x1, mods, ln_g, ln_b)


def _moe_plan(idx_t):
    flat = idx_t.reshape(-1)
    onehot = (flat[:, None] == jnp.arange(N_EXPERTS)[None, :]).astype(jnp.int32)
    rank = jnp.take_along_axis(jnp.cumsum(onehot, axis=0), flat[:, None], axis=1)[:, 0] - 1
    counts = jnp.sum(onehot, axis=0)
    tiles = (counts + MOE_TM - 1) // MOE_TM
    tile_end = jnp.cumsum(tiles)
    tile_start = tile_end - tiles
    pos = tile_start[flat] * MOE_TM + rank
    n_valid = tile_end[-1]
    i32 = lambda a: a.astype(jnp.int32)
    return i32(pos.reshape(TOP_K, T_ALL)), i32(tile_start), i32(tiles), i32(n_valid.reshape(1))


def kernel(x_prompt, x_sample, cache_na_k, cache_na_v, state_ssd_fwd, state_ssd_bwd, c, c_ctx, w_mod, b_mod, w_in, ssd_conv_w, ssd_conv_b, ssd_dt_bias, ssd_a_log, ssd_d, ssd_norm_w, na_rpb, w_ssd_out, w_na_out, w_o, ln1_g, ln1_b, ln2_g, ln2_b, w_router, b_router, w_gate_up, b_gate_up, w_down, b_down):
    assert w_mod.shape[0] == 1, "single-layer trunk"
    x_ctx = x_prompt.reshape(T_CTX, D_MODEL)
    x_dec = x_sample.reshape(T_DEC, D_MODEL)

    cvec = jnp.concatenate([c_ctx[None], c, jnp.zeros((8 - 1 - DEC_BATCH, D_MODEL), F32)], axis=0)
    mods = _modulation(cvec, w_mod[0], b_mod[0])[:1 + DEC_BATCH].reshape(1 + DEC_BATCH, 6, D_MODEL)

    h = _ln_modulate(x_ctx, x_dec, mods)
    w = w_in[0]
    n_main = 3 * NA_WIDTH + SSD_WIDTH + SSD_CONV_CH
    w_bf = w.astype(BF16)
    w_dt = jnp.pad(w[:, n_main:n_main + 2 * SSD_HEADS], ((0, 0), (0, 128 - 2 * SSD_HEADS))).astype(BF16)
    w_gates = w_bf[:, n_main + 2 * SSD_HEADS:]
    tm = 1024
    ctx_blocks = T_CTX // tm
    q_all = _matmul(h, w_bf, rows=T_ALL, col_block0=0, n_out=NA_WIDTH, name="proj_q")
    k_ctx = _matmul(h, w_bf, rows=T_CTX, col_block0=2, n_out=NA_WIDTH, name="proj_k_ctx")
    k_dec = _matmul(h, w_bf, rows=T_DEC, row_block0=ctx_blocks, col_block0=2, n_out=NA_WIDTH, name="proj_k_dec")
    v_ctx = _matmul(h, w_bf, rows=T_CTX, col_block0=4, n_out=NA_WIDTH, name="proj_v_ctx")
    v_dec = _matmul(h, w_bf, rows=T_DEC, row_block0=ctx_blocks, col_block0=4, n_out=NA_WIDTH, name="proj_v_dec")
    zx = _matmul(h, w_bf, rows=T_ALL, col_block0=6, n_out=SSD_WIDTH + SSD_CONV_CH, name="proj_zxbc")
    gates = _matmul(h, w_gates, rows=T_ALL, name="proj_gates")
    dt_all = _matmul(h, w_dt, rows=T_ALL, tn=128, name="proj_dt")

    na_ctx = _context_attention(q_all, k_ctx, v_ctx)
    kc = cache_na_k[:, 0].reshape(DEC_BATCH, PAST_LEN, NA_WIDTH)
    vc = cache_na_v[:, 0].reshape(DEC_BATCH, PAST_LEN, NA_WIDTH)
    na_dec = _neighbourhood_attention(q_all, k_dec, v_dec, kc, vc, _rpb_table(na_rpb[0]))

    consts = _ssd_consts(ssd_conv_w[0], ssd_conv_b[0], ssd_dt_bias[0], ssd_a_log[0], ssd_d[0])
    dt_t = dt_all[:, :2 * SSD_HEADS].reshape(T_ALL // SSD_CHUNK, SSD_CHUNK, 2, SSD_GROUPS, HEADS_PER_GROUP)
    dt_t = jnp.transpose(dt_t, (0, 2, 3, 4, 1))
    u_ctx, h_f, h_b = _ssd_mixer(zx, dt_all, dt_t, consts, length=SEQ, n_seq=BATCH, seq_block0=0,
                                 emit_state=True)
    h0 = (state_ssd_fwd[:, 0].reshape(DEC_BATCH, SSD_GROUPS, HEADS_PER_GROUP, SSD_HEAD_DIM, SSD_STATE),
          state_ssd_bwd[:, 0].reshape(DEC_BATCH, SSD_GROUPS, HEADS_PER_GROUP, SSD_HEAD_DIM, SSD_STATE))
    (u_dec,) = _ssd_mixer(zx, dt_all, dt_t, consts, length=DEC_SEQ, n_seq=DEC_BATCH,
                          seq_block0=T_CTX // DEC_SEQ, h0=h0)

    y = _merge_branches(na_ctx, na_dec, u_ctx, u_dec, ssd_norm_w[0].reshape(1, SSD_WIDTH),
                        w_na_out[0].astype(BF16), w_ssd_out[0].astype(BF16), gates)
    x1, h2, idx_t, wgt_t = _post_mix(y, w_o[0].astype(BF16), x_ctx, x_dec, mods,
                                     ln1_g[0].reshape(1, D_MODEL), ln1_b[0].reshape(1, D_MODEL),
                                     jnp.transpose(w_router[0]), b_router[0].reshape(N_EXPERTS, 1))

    pos, tile_start, tiles, n_valid = _moe_plan(idx_t)
    pos_d = jnp.transpose(pos.reshape(TOP_K, T_ALL // DISPATCH_TOK, DISPATCH_TOK), (1, 0, 2))
    xs = _dispatch(h2, pos_d, tile_start, tiles, n_valid)
    hid = _moe_gate_up(xs, tile_start, tiles, n_valid, w_gate_up[0],
                       b_gate_up[0].reshape(N_EXPERTS, 1, 2 * D_EXPERT))
    ys = _moe_down(hid, tile_start, tiles, n_valid, w_down[0], b_down[0].reshape(N_EXPERTS, 1, D_MODEL))
    pos_c = jnp.transpose(pos.reshape(TOP_K, T_ALL // COMBINE_TOK, COMBINE_TOK), (1, 0, 2))
    wgt = jnp.transpose(wgt_t)
    g2, b2 = ln2_g[0].reshape(1, D_MODEL), ln2_b[0].reshape(1, D_MODEL)
    y_ctx = _combine(ys, pos_c, wgt, x1, mods, g2, b2, rows=T_CTX, row0=0)
    y_dec = _combine(ys, pos_c, wgt, x1, mods, g2, b2, rows=T_DEC, row0=T_CTX)

    return (y_ctx.reshape(BATCH, SEQ, D_MODEL),
            y_dec.reshape(DEC_BATCH, DEC_SEQ, D_MODEL),
            k_ctx.reshape(BATCH, 1, SEQ, NA_HEADS, NA_HEAD_DIM),
            v_ctx.reshape(BATCH, 1, SEQ, NA_HEADS, NA_HEAD_DIM),
            h_f.reshape(BATCH, 1, SSD_HEADS, SSD_HEAD_DIM, SSD_STATE),
            h_b.reshape(BATCH, 1, SSD_HEADS, SSD_HEAD_DIM, SSD_STATE))
```

```python
import functools

import jax
import jax.numpy as jnp
from jax import lax
from jax.experimental import pallas as pl
from jax.experimental.pallas import tpu as pltpu

F32 = jnp.float32
BF16 = jnp.bfloat16

D_MODEL = 2048
BATCH = 32
SEQ = 256
DEC_BATCH = 2
DEC_SEQ = 1024
PAST_LEN = 512
GRID_W = 64
NA_HEADS = 16
NA_HEAD_DIM = 128
NA_WIDTH = NA_HEADS * NA_HEAD_DIM
NA_WIN_R = 8
NA_WIN_C = 16
SSD_HEADS = 32
SSD_HEAD_DIM = 64
SSD_WIDTH = SSD_HEADS * SSD_HEAD_DIM
SSD_GROUPS = 4
SSD_STATE = 128
SSD_CONV = 5
SSD_CHUNK = 128
SSD_CONV_CH = SSD_WIDTH + 2 * SSD_GROUPS * SSD_STATE
N_EXPERTS = 32
TOP_K = 4
D_EXPERT = 2048
SWIGLU_LIMIT = 7.0
SWIGLU_ALPHA = 1.702
DN_ALPHA = 2.0 ** 0.25
LN_EPS = 1e-5

T_CTX = BATCH * SEQ
T_DEC = DEC_BATCH * DEC_SEQ
T_ALL = T_CTX + T_DEC
HEADS_PER_GROUP = SSD_HEADS // SSD_GROUPS
GROUP_W = HEADS_PER_GROUP * SSD_HEAD_DIM
ROWS = DEC_SEQ // GRID_W
RPB_SLOTS = 16

MOE_TM = 256
MOE_TILES = T_ALL * TOP_K // MOE_TM + N_EXPERTS
MOE_ROWS = MOE_TILES * MOE_TM

VMEM_LIMIT = 56 * 1024 * 1024


def _cparams(sem):
    return pltpu.CompilerParams(dimension_semantics=sem, vmem_limit_bytes=VMEM_LIMIT)


def _split3(x):
    hi = x.astype(BF16)
    r1 = x - hi.astype(F32)
    mid = r1.astype(BF16)
    lo = (r1 - mid.astype(F32)).astype(BF16)
    return hi, mid, lo


def _dot(a, b):
    return jnp.dot(a, b, preferred_element_type=F32)


def _dot_nt(a, b):
    return lax.dot_general(a, b, (((1,), (1,)), ((), ())), preferred_element_type=F32)


def _dot_exact_rhs(x, m_bf):
    hi, mid, lo = _split3(x)
    return _dot(hi, m_bf) + (_dot(mid, m_bf) + _dot(lo, m_bf))


def _dot_hi_mid_rhs(x, m_bf):
    hi = x.astype(BF16)
    mid = (x - hi.astype(F32)).astype(BF16)
    return _dot(hi, m_bf) + _dot(mid, m_bf)


def _dot_exact_lhs(m_bf, x):
    hi, mid, lo = _split3(x)
    return _dot(m_bf, hi) + (_dot(m_bf, mid) + _dot(m_bf, lo))


def _dot_nt_f32(a, b):
    ah, am, al = _split3(a)
    bh, bm, bl = _split3(b)
    small = _dot_nt(ah, bl) + _dot_nt(am, bm) + _dot_nt(al, bh)
    mid = _dot_nt(ah, bm) + _dot_nt(am, bh)
    return _dot_nt(ah, bh) + (mid + small)


def _dot_f32(a, b):
    ah, am, al = _split3(a)
    bh, bm, bl = _split3(b)
    small = _dot(ah, bl) + _dot(am, bm) + _dot(al, bh)
    mid = _dot(ah, bm) + _dot(am, bh)
    return _dot(ah, bh) + (mid + small)


def _silu(x):
    return x * jax.nn.sigmoid(x)


def _softplus(x):
    return jnp.maximum(x, 0.0) + jnp.log1p(jnp.exp(-jnp.abs(x)))


def _layer_norm(x):
    mu = jnp.mean(x, axis=-1, keepdims=True)
    xc = x - mu
    var = jnp.mean(xc * xc, axis=-1, keepdims=True)
    return xc * lax.rsqrt(var + LN_EPS)


def _mod_row(i, tm):
    n_ctx = T_CTX // tm
    per_b = DEC_SEQ // tm
    return jnp.where(i < n_ctx, 0, 1 + (i - n_ctx) // per_b)


def _mod_kernel(c_ref, w_ref, b_ref, o_ref):
    o_ref[...] = _dot_f32(_silu(c_ref[...]), w_ref[...]) + b_ref[...]


def _modulation(cvec, w_mod, b_mod):
    tn = 1024
    n = w_mod.shape[1]
    return pl.pallas_call(
        _mod_kernel,
        grid=(n // tn,),
        in_specs=[pl.BlockSpec((8, D_MODEL), lambda j: (0, 0)),
                  pl.BlockSpec((D_MODEL, tn), lambda j: (0, j)),
                  pl.BlockSpec((1, tn), lambda j: (0, j))],
        out_specs=pl.BlockSpec((8, tn), lambda j: (0, j)),
        out_shape=jax.ShapeDtypeStruct((8, n), F32),
        compiler_params=_cparams(("arbitrary",)),
        name="modulation",
    )(cvec, w_mod, b_mod.reshape(1, n))


def _ln_mod_kernel(xc_ref, xd_ref, m_ref, o_ref, *, n_ctx):
    i = pl.program_id(0)
    x = jnp.where(i < n_ctx, xc_ref[...], xd_ref[...])
    y = _layer_norm(x) * (1.0 + m_ref[0, 1:2, :]) + m_ref[0, 0:1, :]
    o_ref[...] = y.astype(o_ref.dtype)


def _ln_modulate(x_ctx, x_dec, mods):
    tm = 512
    n_ctx = T_CTX // tm
    return pl.pallas_call(
        functools.partial(_ln_mod_kernel, n_ctx=n_ctx),
        grid=(T_ALL // tm,),
        in_specs=[pl.BlockSpec((tm, D_MODEL), lambda i: (jnp.minimum(i, n_ctx - 1), 0)),
                  pl.BlockSpec((tm, D_MODEL), lambda i: (jnp.maximum(i - n_ctx, 0), 0)),
                  pl.BlockSpec((1, 6, D_MODEL), lambda i: (_mod_row(i, tm), 0, 0))],
        out_specs=pl.BlockSpec((tm, D_MODEL), lambda i: (i, 0)),
        out_shape=jax.ShapeDtypeStruct((T_ALL, D_MODEL), BF16),
        compiler_params=_cparams(("arbitrary",)),
        name="ln_modulate",
    )(x_ctx, x_dec, mods)


def _mm_kernel(x_ref, w_ref, o_ref):
    o_ref[...] = _dot(x_ref[...], w_ref[...]).astype(o_ref.dtype)


def _matmul(x, w, *, rows, row_block0=0, col_block0=0, n_out=None, tm=1024, tn=1024,
            out_dtype=F32, name="matmul"):
    k = x.shape[1]
    n_out = w.shape[1] if n_out is None else n_out
    return pl.pallas_call(
        _mm_kernel,
        grid=(rows // tm, n_out // tn),
        in_specs=[pl.BlockSpec((tm, k), lambda i, j: (i + row_block0, 0)),
                  pl.BlockSpec((k, tn), lambda i, j: (0, j + col_block0))],
        out_specs=pl.BlockSpec((tm, tn), lambda i, j: (i, j)),
        out_shape=jax.ShapeDtypeStruct((rows, n_out), out_dtype),
        compiler_params=_cparams(("arbitrary", "arbitrary")),
        name=name,
    )(x, w)


def _ctx_attn_kernel(q_ref, k_ref, v_ref, o_ref):
    scale = NA_HEAD_DIM ** -0.5
    for h in range(NA_HEADS):
        sl = slice(h * NA_HEAD_DIM, (h + 1) * NA_HEAD_DIM)
        q = q_ref[:, sl].astype(BF16)
        k = k_ref[:, sl].astype(BF16)
        v = v_ref[:, sl].astype(BF16)
        s = _dot_nt(q, k) * scale
        m = jnp.max(s, axis=-1, keepdims=True)
        p = jnp.exp(s - m)
        l = jnp.sum(p, axis=-1, keepdims=True)
        o_ref[:, sl] = _dot(p.astype(BF16), v) / l


def _context_attention(q_all, k_ctx, v_ctx):
    spec = pl.BlockSpec((SEQ, NA_WIDTH), lambda b: (b, 0))
    return pl.pallas_call(
        _ctx_attn_kernel,
        grid=(BATCH,),
        in_specs=[spec, spec, spec],
        out_specs=spec,
        out_shape=jax.ShapeDtypeStruct((T_CTX, NA_WIDTH), F32),
        compiler_params=_cparams(("arbitrary",)),
        name="context_attention",
    )(q_all, k_ctx, v_ctx)


def _nbr_attn_kernel(q_ref, k_ref, v_ref, kc_ref, vc_ref, rc_ref, o_ref):
    scale = NA_HEAD_DIM ** -0.5
    kr = min(NA_WIN_R, ROWS)
    kc = kc_ref[0].astype(BF16)
    vc = vc_ref[0].astype(BF16)
    for r in range(ROWS):
        r0 = min(max(r - kr // 2, 0), ROWS - kr)
        off = (r0 - r + NA_WIN_R - 1) * GRID_W
        q = q_ref[r * GRID_W:(r + 1) * GRID_W, :].astype(BF16)
        kb = k_ref[r0 * GRID_W:(r0 + kr) * GRID_W, :].astype(BF16)
        vb = v_ref[r0 * GRID_W:(r0 + kr) * GRID_W, :].astype(BF16)
        s_loc = _dot_nt(q, kb) * scale + rc_ref[0, :, off:off + kr * GRID_W]
        s_ctx = _dot_nt(q, kc) * scale
        m = jnp.maximum(jnp.max(s_loc, axis=-1, keepdims=True),
                        jnp.max(s_ctx, axis=-1, keepdims=True))
        p_loc = jnp.exp(s_loc - m)
        p_ctx = jnp.exp(s_ctx - m)
        l = jnp.sum(p_loc, axis=-1, keepdims=True) + jnp.sum(p_ctx, axis=-1, keepdims=True)
        o = _dot(p_loc.astype(BF16), vb) + _dot(p_ctx.astype(BF16), vc)
        o_ref[r * GRID_W:(r + 1) * GRID_W, :] = o / l


def _rpb_table(rpb):
    col = jnp.arange(GRID_W)
    c0 = jnp.clip(col - NA_WIN_C // 2, 0, GRID_W - NA_WIN_C)
    col_mask = (col[None, :] >= c0[:, None]) & (col[None, :] < c0[:, None] + NA_WIN_C)
    dc_idx = jnp.clip(col[None, :] - col[:, None] + NA_WIN_C - 1, 0, 2 * NA_WIN_C - 2)
    pick = (dc_idx[:, :, None] == jnp.arange(2 * NA_WIN_C - 1)[None, None, :]).astype(F32)
    t = jnp.einsum('hdj,qkj->hqdk', rpb, pick, precision=lax.Precision.HIGHEST)
    t = jnp.where(col_mask[None, :, None, :], t, -jnp.inf)
    t = jnp.pad(t, ((0, 0), (0, 0), (0, RPB_SLOTS - t.shape[2]), (0, 0)))
    return t.reshape(NA_HEADS, GRID_W, RPB_SLOTS * GRID_W)


def _neighbourhood_attention(q_all, k_dec, v_dec, kc, vc, rc):
    q_row0 = T_CTX // DEC_SEQ
    hd = NA_HEAD_DIM
    return pl.pallas_call(
        _nbr_attn_kernel,
        grid=(DEC_BATCH, NA_HEADS),
        in_specs=[pl.BlockSpec((DEC_SEQ, hd), lambda b, h: (b + q_row0, h)),
                  pl.BlockSpec((DEC_SEQ, hd), lambda b, h: (b, h)),
                  pl.BlockSpec((DEC_SEQ, hd), lambda b, h: (b, h)),
                  pl.BlockSpec((1, PAST_LEN, hd), lambda b, h: (b, 0, h)),
                  pl.BlockSpec((1, PAST_LEN, hd), lambda b, h: (b, 0, h)),
                  pl.BlockSpec((1, GRID_W, RPB_SLOTS * GRID_W), lambda b, h: (h, 0, 0))],
        out_specs=pl.BlockSpec((DEC_SEQ, hd), lambda b, h: (b, h)),
        out_shape=jax.ShapeDtypeStruct((T_DEC, NA_WIDTH), F32),
        compiler_params=_cparams(("arbitrary", "arbitrary")),
        name="neighbourhood_attention",
    )(q_all, k_dec, v_dec, kc, vc, rc)


CONV_HALO = 8


def _conv_silu(u_ref, w_ref, b_ref, pad_s, length):
    width = u_ref.shape[1]
    halo = jnp.zeros((CONV_HALO, width), F32)
    pad_s[0:CONV_HALO, 0:width] = halo
    pad_s[CONV_HALO + length:2 * CONV_HALO + length, 0:width] = halo
    pad_s[CONV_HALO:CONV_HALO + length, 0:width] = u_ref[...]
    acc = b_ref[...]
    for k in range(SSD_CONV):
        d = k - SSD_CONV // 2
        acc = acc + pad_s[CONV_HALO + d:CONV_HALO + d + length, 0:width] * w_ref[k:k + 1, :]
    return _silu(acc)


def _ssd_kernel(*refs, length, has_h0, emit_state):
    (xs_ref, b_ref, c_ref, z_ref, dt_ref, dtt_ref, cwx_ref, cwb_ref, cwc_ref,
     cbx_ref, cbb_ref, cbc_ref, dtb_row_ref, dtb_col_ref, al_row_ref, al_col_ref,
     e_ref, d_ref) = refs[:18]
    pos = 18
    if has_h0:
        h0f_ref, h0b_ref = refs[pos:pos + 2]
        pos += 2
    u_ref = refs[pos]
    pos += 1
    if emit_state:
        hf_ref, hb_ref = refs[pos:pos + 2]
        pos += 2
    xs_s, b_s, c_s, y_s, st_s, pad_s = refs[pos:]

    q = SSD_CHUNK
    nc = length // q
    xs_s[...] = _conv_silu(xs_ref, cwx_ref, cbx_ref, pad_s, length)
    b_s[...] = _conv_silu(b_ref, cwb_ref, cbb_ref, pad_s, length)
    c_s[...] = _conv_silu(c_ref, cwc_ref, cbc_ref, pad_s, length)
    y_s[...] = d_ref[0] * xs_s[...]

    for dirn in range(2):
        if has_h0:
            h0 = (h0f_ref if dirn == 0 else h0b_ref)[0, 0]
            st_s[dirn] = jnp.transpose(h0.reshape(GROUP_W, SSD_STATE))
        else:
            st_s[dirn] = jnp.zeros((SSD_STATE, GROUP_W), F32)

    ri = lax.broadcasted_iota(jnp.int32, (q, q), 0)
    ci = lax.broadcasted_iota(jnp.int32, (q, q), 1)
    lower = ri >= ci
    upper = ri <= ci
    lower_bf = jnp.where(lower, 1.0, 0.0).astype(BF16)
    upper_bf = jnp.where(upper, 1.0, 0.0).astype(BF16)

    def chunk(c, dirn):
        r0 = c * q if isinstance(c, int) else pl.multiple_of(c * q, q)
        tri = lower if dirn == 0 else upper
        xs_c = xs_s[pl.ds(r0, q), :]
        bc = b_s[pl.ds(r0, q), :]
        cc = c_s[pl.ds(r0, q), :]
        dtp = _softplus(dt_ref[pl.ds(r0, q), :] + dtb_row_ref[...])
        da = dtp * (-jnp.exp(al_row_ref[...]))
        pre = _dot_exact_lhs(lower_bf, da)
        cs = pre if dirn == 0 else pre[q - 1:q, :] - pre + da
        e_bf = e_ref[dirn, 0]
        dt_e = _dot_hi_mid_rhs(dtp, e_bf)
        cs_e = _dot_exact_rhs(cs, e_bf)
        dt_t = _softplus(dtt_ref[c, dirn, 0] + dtb_col_ref[dirn, 0])
        da_t = dt_t * (-jnp.exp(al_col_ref[dirn, 0]))
        cs_t = _dot_exact_rhs(da_t, upper_bf if dirn == 0 else lower_bf)

        cb = _dot_nt(cc.astype(BF16), bc.astype(BF16))
        bt = jnp.transpose(bc).astype(BF16)
        end = q - 1 if dirn == 0 else 0
        cs_end = cs_e[end:end + 1, :]
        xdt = xs_c * dt_e
        st = st_s[dirn]
        y = _dot(cc.astype(BF16), st.astype(BF16)) * jnp.exp(cs_e)
        st_s[dirn] = jnp.exp(cs_end) * st + _dot(bt, (xdt * jnp.exp(cs_end - cs_e)).astype(BF16))
        xdt_bf = xdt.astype(BF16)
        parts = []
        for r in range(HEADS_PER_GROUP):
            sl = slice(r * SSD_HEAD_DIM, (r + 1) * SSD_HEAD_DIM)
            diff = cs_e[:, r * SSD_HEAD_DIM:r * SSD_HEAD_DIM + 1] - cs_t[r:r + 1, :]
            lm = jnp.exp(jnp.where(tri, diff, -jnp.inf))
            parts.append(_dot((cb * lm).astype(BF16), xdt_bf[:, sl]))
        y = y + jnp.concatenate(parts, axis=-1)
        y_s[pl.ds(r0, q), :] = y_s[pl.ds(r0, q), :] + y

    if nc <= 2:
        for i in range(nc):
            chunk(i, 0)
            chunk(nc - 1 - i, 1)
    else:
        def both(i, carry):
            chunk(i, 0)
            chunk(nc - 1 - i, 1)
            return carry
        lax.fori_loop(0, nc, both, 0)

    u_ref[...] = y_s[...] * _silu(z_ref[...])
    if emit_state:
        hf_ref[0, 0] = jnp.transpose(st_s[0]).reshape(HEADS_PER_GROUP, SSD_HEAD_DIM, SSD_STATE)
        hb_ref[0, 0] = jnp.transpose(st_s[1]).reshape(HEADS_PER_GROUP, SSD_HEAD_DIM, SSD_STATE)


def _ssd_mixer(zx, dt_all, dt_t, consts, *, length, n_seq, seq_block0, h0=None, emit_state=False):
    (cw, cb, dtb_row, dtb_col, al_row, al_col, expand, d_row) = consts
    g_w, n_s = GROUP_W, SSD_STATE
    xs_cb0 = D_MODEL // g_w
    b_cb0 = (D_MODEL + SSD_WIDTH) // n_s
    c_cb0 = b_cb0 + SSD_GROUPS
    cw_b0 = SSD_WIDTH // n_s
    nck = length // SSD_CHUNK
    in_specs = [
        pl.BlockSpec((length, g_w), lambda s, g: (s + seq_block0, xs_cb0 + g)),
        pl.BlockSpec((length, n_s), lambda s, g: (s + seq_block0, b_cb0 + g)),
        pl.BlockSpec((length, n_s), lambda s, g: (s + seq_block0, c_cb0 + g)),
        pl.BlockSpec((length, g_w), lambda s, g: (s + seq_block0, g)),
        pl.BlockSpec((length, 128), lambda s, g: (s + seq_block0, 0)),
        pl.BlockSpec((nck, 2, 1, HEADS_PER_GROUP, SSD_CHUNK), lambda s, g: (s + seq_block0, 0, g, 0, 0)),
        pl.BlockSpec((SSD_CONV, g_w), lambda s, g: (0, g)),
        pl.BlockSpec((SSD_CONV, n_s), lambda s, g: (0, cw_b0 + g)),
        pl.BlockSpec((SSD_CONV, n_s), lambda s, g: (0, cw_b0 + SSD_GROUPS + g)),
        pl.BlockSpec((1, g_w), lambda s, g: (0, g)),
        pl.BlockSpec((1, n_s), lambda s, g: (0, cw_b0 + g)),
        pl.BlockSpec((1, n_s), lambda s, g: (0, cw_b0 + SSD_GROUPS + g)),
        pl.BlockSpec((1, 128), lambda s, g: (0, 0)),
        pl.BlockSpec((2, 1, HEADS_PER_GROUP, SSD_CHUNK), lambda s, g: (0, g, 0, 0)),
        pl.BlockSpec((1, 128), lambda s, g: (0, 0)),
        pl.BlockSpec((2, 1, HEADS_PER_GROUP, SSD_CHUNK), lambda s, g: (0, g, 0, 0)),
        pl.BlockSpec((2, 1, 128, g_w), lambda s, g: (0, g, 0, 0)),
        pl.BlockSpec((1, 1, g_w), lambda s, g: (g, 0, 0)),
    ]
    args = [zx, zx, zx, zx, dt_all, dt_t, cw, cw, cw, cb, cb, cb,
            dtb_row, dtb_col, al_row, al_col, expand, d_row]
    st_spec = pl.BlockSpec((1, 1, HEADS_PER_GROUP, SSD_HEAD_DIM, n_s), lambda s, g: (s, g, 0, 0, 0))
    if h0 is not None:
        in_specs += [st_spec, st_spec]
        args += list(h0)
    out_specs = [pl.BlockSpec((length, g_w), lambda s, g: (s, g))]
    out_shape = [jax.ShapeDtypeStruct((n_seq * length, SSD_WIDTH), F32)]
    if emit_state:
        st_shape = jax.ShapeDtypeStruct((n_seq, SSD_GROUPS, HEADS_PER_GROUP, SSD_HEAD_DIM, n_s), F32)
        out_specs += [st_spec, st_spec]
        out_shape += [st_shape, st_shape]
    return pl.pallas_call(
        functools.partial(_ssd_kernel, length=length, has_h0=h0 is not None, emit_state=emit_state),
        grid=(n_seq, SSD_GROUPS),
        in_specs=in_specs,
        out_specs=out_specs,
        out_shape=out_shape,
        scratch_shapes=[pltpu.VMEM((length, g_w), F32), pltpu.VMEM((length, n_s), F32),
                        pltpu.VMEM((length, n_s), F32), pltpu.VMEM((length, g_w), F32),
                        pltpu.VMEM((2, n_s, g_w), F32),
                        pltpu.VMEM((length + 2 * CONV_HALO, g_w), F32)],
        compiler_params=_cparams(("arbitrary", "arbitrary")),
        name="ssd_mixer_%d" % length,
    )(*args)


def _ssd_consts(conv_w, conv_b, dt_bias, a_log, d_skip):
    hpg = HEADS_PER_GROUP
    dtb_row = jnp.pad(dt_bias.reshape(1, 2 * SSD_HEADS), ((0, 0), (0, 128 - 2 * SSD_HEADS)))
    col = lambda p: jnp.broadcast_to(p.reshape(2, SSD_GROUPS, hpg, 1), (2, SSD_GROUPS, hpg, SSD_CHUNK))
    al_row = jnp.pad(a_log.reshape(1, 2 * SSD_HEADS), ((0, 0), (0, 128 - 2 * SSD_HEADS)))
    src = (jnp.arange(2)[:, None, None] * SSD_HEADS + jnp.arange(SSD_GROUPS)[None, :, None] * hpg
           + jnp.arange(GROUP_W)[None, None, :] // SSD_HEAD_DIM)
    expand = (jnp.arange(128)[None, None, :, None] == src[:, :, None, :]).astype(BF16)
    d_row = jnp.repeat(d_skip.reshape(SSD_GROUPS, 1, hpg), SSD_HEAD_DIM, axis=-1)
    return (conv_w, conv_b.reshape(1, SSD_CONV_CH), dtb_row, col(dt_bias), al_row, col(a_log), expand, d_row)


def _merge_kernel(nac_ref, nad_ref, uc_ref, ud_ref, nw_ref, w1_ref, w2_ref, g_na_ref, g_ssd_ref,
                  o_ref, *, n_ctx):
    is_ctx = pl.program_id(0) < n_ctx
    a1 = jnp.where(is_ctx, nac_ref[...], nad_ref[...]).astype(BF16)
    u = jnp.where(is_ctx, uc_ref[...], ud_ref[...])
    r = lax.rsqrt(jnp.mean(u * u, axis=-1, keepdims=True) + LN_EPS)
    a2 = (u * r * nw_ref[...]).astype(BF16)
    o = (jax.nn.sigmoid(g_na_ref[...]) * _dot(a1, w1_ref[...])
         + jax.nn.sigmoid(g_ssd_ref[...]) * _dot(a2, w2_ref[...]))
    o_ref[...] = o.astype(o_ref.dtype)


def _merge_branches(na_ctx, na_dec, u_ctx, u_dec, norm_w, w_na_out, w_ssd_out, gates):
    tm = 256
    n_ctx = T_CTX // tm
    ctx_rows = lambda i: (jnp.minimum(i, n_ctx - 1), 0)
    dec_rows = lambda i: (jnp.maximum(i - n_ctx, 0), 0)
    const = lambda i: (0, 0)
    resident = pl.Buffered(1)
    return pl.pallas_call(
        functools.partial(_merge_kernel, n_ctx=n_ctx),
        grid=(T_ALL // tm,),
        in_specs=[pl.BlockSpec((tm, D_MODEL), ctx_rows), pl.BlockSpec((tm, D_MODEL), dec_rows),
                  pl.BlockSpec((tm, D_MODEL), ctx_rows), pl.BlockSpec((tm, D_MODEL), dec_rows),
                  pl.BlockSpec((1, D_MODEL), const),
                  pl.BlockSpec((D_MODEL, D_MODEL), const, pipeline_mode=resident),
                  pl.BlockSpec((D_MODEL, D_MODEL), const, pipeline_mode=resident),
                  pl.BlockSpec((tm, D_MODEL), lambda i: (i, 1)),
                  pl.BlockSpec((tm, D_MODEL), lambda i: (i, 0))],
        out_specs=pl.BlockSpec((tm, D_MODEL), lambda i: (i, 0)),
        out_shape=jax.ShapeDtypeStruct((T_ALL, D_MODEL), BF16),
        compiler_params=_cparams(("arbitrary",)),
        name="merge_branches",
    )(na_ctx, na_dec, u_ctx, u_dec, norm_w, w_na_out, w_ssd_out, gates, gates)


def _post_mix_kernel(y_ref, wo_ref, xc_ref, xd_ref, m_ref, g_ref, b_ref, wr_ref, br_ref,
                     x1_ref, h2_ref, idx_ref, wgt_ref, *, n_ctx):
    i = pl.program_id(0)
    x = jnp.where(i < n_ctx, xc_ref[...], xd_ref[...])
    mix = _dot(y_ref[...], wo_ref[...])
    x1 = _layer_norm(DN_ALPHA * x + m_ref[0, 2:3, :] * mix) * g_ref[...] + b_ref[...]
    x1_ref[...] = x1
    h2 = _layer_norm(x1) * (1.0 + m_ref[0, 4:5, :]) + m_ref[0, 3:4, :]
    h2_ref[...] = h2
    logits = _dot_nt_f32(wr_ref[...], h2) + br_ref[...]
    eidx = lax.broadcasted_iota(jnp.int32, logits.shape, 0)
    vals, idxs = [], []
    for _ in range(TOP_K):
        m = jnp.max(logits, axis=0, keepdims=True)
        sel = jnp.min(jnp.where(logits == m, eidx, N_EXPERTS), axis=0, keepdims=True)
        logits = jnp.where(eidx == sel, -jnp.inf, logits)
        vals.append(m)
        idxs.append(sel)
    ex = [jnp.exp(v - vals[0]) for v in vals]
    tot = ex[0] + ex[1] + ex[2] + ex[3]
    idx_ref[...] = jnp.concatenate(idxs, axis=0)
    wgt_ref[...] = jnp.concatenate([e / tot for e in ex], axis=0)


def _post_mix(y, w_o, x_ctx, x_dec, mods, ln_g, ln_b, w_router_t, b_router):
    tm = 512
    n_ctx = T_CTX // tm
    row = lambda i: (i, 0)
    const = lambda i: (0, 0)
    return pl.pallas_call(
        functools.partial(_post_mix_kernel, n_ctx=n_ctx),
        grid=(T_ALL // tm,),
        in_specs=[pl.BlockSpec((tm, D_MODEL), row),
                  pl.BlockSpec((D_MODEL, D_MODEL), const, pipeline_mode=pl.Buffered(1)),
                  pl.BlockSpec((tm, D_MODEL), lambda i: (jnp.minimum(i, n_ctx - 1), 0)),
                  pl.BlockSpec((tm, D_MODEL), lambda i: (jnp.maximum(i - n_ctx, 0), 0)),
                  pl.BlockSpec((1, 6, D_MODEL), lambda i: (_mod_row(i, tm), 0, 0)),
                  pl.BlockSpec((1, D_MODEL), const), pl.BlockSpec((1, D_MODEL), const),
                  pl.BlockSpec((N_EXPERTS, D_MODEL), const), pl.BlockSpec((N_EXPERTS, 1), const)],
        out_specs=[pl.BlockSpec((tm, D_MODEL), row), pl.BlockSpec((tm, D_MODEL), row),
                   pl.BlockSpec((TOP_K, tm), lambda i: (0, i)), pl.BlockSpec((TOP_K, tm), lambda i: (0, i))],
        out_shape=[jax.ShapeDtypeStruct((T_ALL, D_MODEL), F32), jax.ShapeDtypeStruct((T_ALL, D_MODEL), F32),
                   jax.ShapeDtypeStruct((TOP_K, T_ALL), jnp.int32), jax.ShapeDtypeStruct((TOP_K, T_ALL), F32)],
        compiler_params=_cparams(("arbitrary",)),
        name="post_mix_router",
    )(y, w_o, x_ctx, x_dec, mods, ln_g, ln_b, w_router_t, b_router)


DISPATCH_TOK = 128


def _tile_rows(t):
    return pl.ds(pl.multiple_of(t * MOE_TM, MOE_TM), MOE_TM)


def _dispatch_kernel(t0_ref, nt_ref, nv_ref, pos_ref, x_hbm, xs_hbm, buf, zero_s, sem_in, sem_out, sem_z,
                     *, n_steps):
    i = pl.program_id(0)
    slot = i % 3

    def zero_copy(tile):
        return pltpu.make_async_copy(zero_s, xs_hbm.at[_tile_rows(tile)], sem_z)

    def in_copy(step, s):
        rows = pl.ds(pl.multiple_of(step * DISPATCH_TOK, DISPATCH_TOK), DISPATCH_TOK)
        return pltpu.make_async_copy(x_hbm.at[rows], buf.at[s], sem_in.at[s])

    def row_copy(s, t, dst):
        return pltpu.make_async_copy(buf.at[s, pl.ds(t, 1)], xs_hbm.at[pl.ds(dst, 1)], sem_out.at[s])

    def wait_rows(s):
        for _ in range(TOP_K * DISPATCH_TOK):
            row_copy(s, 0, 0).wait()

    @pl.when(i == 0)
    def _():
        in_copy(0, 0).start()
        zero_s[...] = jnp.zeros(zero_s.shape, zero_s.dtype)
        for e in range(N_EXPERTS):
            @pl.when(nt_ref[e] > 0)
            def _():
                zero_copy(t0_ref[e] + nt_ref[e] - 1).start()

        def tail_start(t, c):
            zero_copy(t).start()
            return c
        lax.fori_loop(nv_ref[0], MOE_TILES, tail_start, 0)
        for e in range(N_EXPERTS):
            @pl.when(nt_ref[e] > 0)
            def _():
                zero_copy(0).wait()

        def tail_wait(t, c):
            zero_copy(0).wait()
            return c
        lax.fori_loop(nv_ref[0], MOE_TILES, tail_wait, 0)

    in_copy(i, slot).wait()

    @pl.when(i + 1 < n_steps)
    def _():
        in_copy(i + 1, (i + 1) % 3).start()

    for k in range(TOP_K):
        for t in range(DISPATCH_TOK):
            row_copy(slot, t, pos_ref[0, k, t]).start(priority=t % 2)

    @pl.when(i > 0)
    def _():
        wait_rows((i + 2) % 3)

    @pl.when(i == n_steps - 1)
    def _():
        wait_rows(slot)


def _dispatch(h2, pos_blocks, tile_start, tiles, n_valid):
    n_steps = T_ALL // DISPATCH_TOK
    grid_spec = pltpu.PrefetchScalarGridSpec(
        num_scalar_prefetch=3,
        grid=(n_steps,),
        in_specs=[pl.BlockSpec((1, TOP_K, DISPATCH_TOK), lambda i, *_: (i, 0, 0), memory_space=pltpu.SMEM),
                  pl.BlockSpec(memory_space=pl.ANY)],
        out_specs=pl.BlockSpec(memory_space=pl.ANY),
        scratch_shapes=[pltpu.VMEM((3, DISPATCH_TOK, D_MODEL), F32),
                        pltpu.VMEM((MOE_TM, D_MODEL), F32),
                        pltpu.SemaphoreType.DMA((3,)), pltpu.SemaphoreType.DMA((3,)),
                        pltpu.SemaphoreType.DMA(())],
    )
    return pl.pallas_call(
        functools.partial(_dispatch_kernel, n_steps=n_steps),
        grid_spec=grid_spec,
        out_shape=jax.ShapeDtypeStruct((MOE_ROWS, D_MODEL), F32),
        compiler_params=_cparams(("arbitrary",)),
        name="moe_dispatch",
    )(tile_start, tiles, n_valid, pos_blocks, h2)


def _expert_tile_loop(t0, nt, in_copy, out_copy, compute, after_tile):
    @pl.when(nt > 0)
    def _():
        in_copy(t0, 0).start()

    def body(i, carry):
        slot = i % 2
        in_copy(t0 + i, slot).wait()

        @pl.when(i + 1 < nt)
        def _():
            in_copy(t0 + i + 1, 1 - slot).start()

        after_tile(i)

        @pl.when(i >= 2)
        def _():
            out_copy(t0 + i - 2, slot).wait()

        compute(slot)
        out_copy(t0 + i, slot).start()
        return carry

    lax.fori_loop(0, nt, body, 0)

    @pl.when(nt >= 2)
    def _():
        out_copy(t0, nt % 2).wait()

    @pl.when(nt >= 1)
    def _():
        out_copy(t0, (nt + 1) % 2).wait()


def _zero_tail_tiles(is_last, n_valid, obuf, out_copy):
    @pl.when(is_last)
    def _():
        obuf[0] = jnp.zeros(obuf.shape[1:], obuf.dtype)

        def body(t, carry):
            out_copy(t, 0).start()
            out_copy(t, 0).wait()
            return carry
        lax.fori_loop(n_valid, MOE_TILES, body, 0)


W_CHUNKS = 16


def _stream_weights(step, n_steps, chunks_per_tile, chunk_copy, stage, w_s, tile_loop):
    def start_chunks(target_step, lo, hi):
        def body(c, carry):
            chunk_copy(target_step, c).start()
            return carry
        lax.fori_loop(lo, hi, body, 0)

    @pl.when(step == 0)
    def _():
        start_chunks(0, 0, W_CHUNKS)

    for _ in range(W_CHUNKS):
        chunk_copy(0, 0).wait()
    w_s[...] = stage[...].astype(BF16)

    has_next = step + 1 < n_steps

    def after_tile(i):
        @pl.when(has_next)
        def _():
            start_chunks(step + 1, jnp.minimum(i * chunks_per_tile, W_CHUNKS),
                         jnp.minimum((i + 1) * chunks_per_tile, W_CHUNKS))

    n_tiles = tile_loop(after_tile)

    @pl.when(has_next)
    def _():
        start_chunks(step + 1, jnp.minimum(n_tiles * chunks_per_tile, W_CHUNKS), W_CHUNKS)


def _gate_up_kernel(t0_ref, nt_ref, cpt_ref, nv_ref, xs_hbm, w_hbm, bg_ref, bu_ref, hid_hbm,
                    stage, w_s, xbuf, obuf, sem_w, sem_in, sem_out, *, tn, nj):
    j = pl.program_id(0)
    e = pl.program_id(1)
    rows = D_MODEL // (W_CHUNKS // 2)

    def chunk_copy(step, c):
        m = c // (W_CHUNKS // 2)
        r = pl.multiple_of((c % (W_CHUNKS // 2)) * rows, rows)
        col = pl.multiple_of((m * nj + step // N_EXPERTS) * tn, tn)
        return pltpu.make_async_copy(w_hbm.at[step % N_EXPERTS, pl.ds(r, rows), pl.ds(col, tn)],
                                     stage.at[m, pl.ds(r, rows)], sem_w)

    def in_copy(t, slot):
        return pltpu.make_async_copy(xs_hbm.at[_tile_rows(t)], xbuf.at[slot], sem_in.at[slot])

    def out_copy(t, slot):
        cols = pl.ds(pl.multiple_of(j * tn, tn), tn)
        return pltpu.make_async_copy(obuf.at[slot], hid_hbm.at[_tile_rows(t), cols], sem_out.at[slot])

    def compute(slot):
        x = xbuf[slot].astype(BF16)
        gate = jnp.minimum(_dot(x, w_s[0]) + bg_ref[0], SWIGLU_LIMIT)
        up = jnp.clip(_dot(x, w_s[1]) + bu_ref[0], -SWIGLU_LIMIT, SWIGLU_LIMIT)
        hid = (up + 1.0) * gate * jax.nn.sigmoid(SWIGLU_ALPHA * gate)
        obuf[slot] = hid.astype(obuf.dtype)

    def tile_loop(after_tile):
        _expert_tile_loop(t0_ref[e], nt_ref[e], in_copy, out_copy, compute, after_tile)
        return nt_ref[e]

    _stream_weights(j * N_EXPERTS + e, nj * N_EXPERTS, cpt_ref[e], chunk_copy, stage, w_s, tile_loop)
    _zero_tail_tiles(e == N_EXPERTS - 1, nv_ref[0], obuf, out_copy)


def _moe_gate_up(xs, plan, w_gate_up, b_gate_up):
    tn = 1024
    nj = D_EXPERT // tn
    grid_spec = pltpu.PrefetchScalarGridSpec(
        num_scalar_prefetch=4,
        grid=(nj, N_EXPERTS),
        in_specs=[pl.BlockSpec(memory_space=pl.ANY), pl.BlockSpec(memory_space=pl.ANY),
                  pl.BlockSpec((1, 1, tn), lambda j, e, *_: (e, 0, j)),
                  pl.BlockSpec((1, 1, tn), lambda j, e, *_: (e, 0, nj + j))],
        out_specs=pl.BlockSpec(memory_space=pl.ANY),
        scratch_shapes=[pltpu.VMEM((2, D_MODEL, tn), F32), pltpu.VMEM((2, D_MODEL, tn), BF16),
                        pltpu.VMEM((2, MOE_TM, D_MODEL), F32), pltpu.VMEM((2, MOE_TM, tn), BF16),
                        pltpu.SemaphoreType.DMA(()), pltpu.SemaphoreType.DMA((2,)),
                        pltpu.SemaphoreType.DMA((2,))],
    )
    return pl.pallas_call(
        functools.partial(_gate_up_kernel, tn=tn, nj=nj),
        grid_spec=grid_spec,
        out_shape=jax.ShapeDtypeStruct((MOE_ROWS, D_EXPERT), BF16),
        compiler_params=_cparams(("arbitrary", "arbitrary")),
        name="moe_gate_up",
    )(*plan, xs, w_gate_up, b_gate_up, b_gate_up)


def _down_kernel(t0_ref, nt_ref, cpt_ref, nv_ref, hid_hbm, w_hbm, b_ref, ys_hbm,
                 stage, w_s, hbuf, obuf, sem_w, sem_in, sem_out):
    e = pl.program_id(0)
    rows = D_EXPERT // W_CHUNKS

    def chunk_copy(step, c):
        r = pl.ds(pl.multiple_of(c * rows, rows), rows)
        return pltpu.make_async_copy(w_hbm.at[step, r], stage.at[r], sem_w)

    def in_copy(t, slot):
        return pltpu.make_async_copy(hid_hbm.at[_tile_rows(t)], hbuf.at[slot], sem_in.at[slot])

    def out_copy(t, slot):
        return pltpu.make_async_copy(obuf.at[slot], ys_hbm.at[_tile_rows(t)], sem_out.at[slot])

    def compute(slot):
        obuf[slot] = _dot(hbuf[slot], w_s[...]) + b_ref[0]

    def tile_loop(after_tile):
        _expert_tile_loop(t0_ref[e], nt_ref[e], in_copy, out_copy, compute, after_tile)
        return nt_ref[e]

    _stream_weights(e, N_EXPERTS, cpt_ref[e], chunk_copy, stage, w_s, tile_loop)
    _zero_tail_tiles(e == N_EXPERTS - 1, nv_ref[0], obuf, out_copy)


def _moe_down(hid, plan, w_down, b_down):
    grid_spec = pltpu.PrefetchScalarGridSpec(
        num_scalar_prefetch=4,
        grid=(N_EXPERTS,),
        in_specs=[pl.BlockSpec(memory_space=pl.ANY), pl.BlockSpec(memory_space=pl.ANY),
                  pl.BlockSpec((1, 1, D_MODEL), lambda e, *_: (e, 0, 0))],
        out_specs=pl.BlockSpec(memory_space=pl.ANY),
        scratch_shapes=[pltpu.VMEM((D_EXPERT, D_MODEL), F32), pltpu.VMEM((D_EXPERT, D_MODEL), BF16),
                        pltpu.VMEM((2, MOE_TM, D_EXPERT), BF16), pltpu.VMEM((2, MOE_TM, D_MODEL), F32),
                        pltpu.SemaphoreType.DMA(()), pltpu.SemaphoreType.DMA((2,)),
                        pltpu.SemaphoreType.DMA((2,))],
    )
    return pl.pallas_call(
        _down_kernel,
        grid_spec=grid_spec,
        out_shape=jax.ShapeDtypeStruct((MOE_ROWS, D_MODEL), F32),
        compiler_params=_cparams(("arbitrary",)),
        name="moe_down",
    )(*plan, hid, w_down, b_down)


COMBINE_TOK = 64


def _start_rows(src_hbm, idx_ref, idx_lead, dst, sem, count):
    for r in range(count):
        pltpu.make_async_copy(src_hbm.at[pl.ds(idx_ref[idx_lead + (r,)], 1)], dst.at[pl.ds(r, 1)],
                              sem).start(priority=r % 2)


def _wait_rows(src_hbm, dst, sem, count):
    for _ in range(count):
        pltpu.make_async_copy(src_hbm.at[pl.ds(0, 1)], dst.at[pl.ds(0, 1)], sem).wait()


def _combine_kernel(pos_ref, pos_next_ref, ys_hbm, wgt_ref, x1_ref, m_ref, g_ref, b_ref, o_ref, buf, sem,
                    *, n_steps):
    i = pl.program_id(0)
    slot = i % 2

    def start(p_ref, s):
        for k in range(TOP_K):
            _start_rows(ys_hbm, p_ref, (0, k), buf.at[s, k], sem.at[s], COMBINE_TOK)

    @pl.when(i == 0)
    def _():
        start(pos_ref, 0)

    @pl.when(i + 1 < n_steps)
    def _():
        start(pos_next_ref, 1 - slot)

    _wait_rows(ys_hbm, buf.at[slot, 0], sem.at[slot], TOP_K * COMBINE_TOK)
    w = wgt_ref[...]
    ffn = w[:, 0:1] * buf[slot, 0]
    for k in range(1, TOP_K):
        ffn = ffn + w[:, k:k + 1] * buf[slot, k]
    x2 = _layer_norm(DN_ALPHA * x1_ref[...] + m_ref[0, 5:6, :] * ffn) * g_ref[...] + b_ref[...]
    o_ref[...] = x2


def _combine(ys, pos_blocks, wgt, x1, mods, ln_g, ln_b, *, rows, row0):
    tm = COMBINE_TOK
    b0 = row0 // tm
    n = rows // tm
    const = lambda i: (0, 0)
    pos_spec = lambda f: pl.BlockSpec((1, TOP_K, tm), f, memory_space=pltpu.SMEM)
    return pl.pallas_call(
        functools.partial(_combine_kernel, n_steps=n),
        grid=(n,),
        in_specs=[pos_spec(lambda i: (i + b0, 0, 0)),
                  pos_spec(lambda i: (jnp.minimum(i + 1, n - 1) + b0, 0, 0)),
                  pl.BlockSpec(memory_space=pl.ANY),
                  pl.BlockSpec((tm, TOP_K), lambda i: (i + b0, 0)),
                  pl.BlockSpec((tm, D_MODEL), lambda i: (i + b0, 0)),
                  pl.BlockSpec((1, 6, D_MODEL), lambda i: (_mod_row(i + b0, tm), 0, 0)),
                  pl.BlockSpec((1, D_MODEL), const), pl.BlockSpec((1, D_MODEL), const)],
        out_specs=pl.BlockSpec((tm, D_MODEL), lambda i: (i, 0)),
        out_shape=jax.ShapeDtypeStruct((rows, D_MODEL), F32),
        scratch_shapes=[pltpu.VMEM((2, TOP_K, tm, D_MODEL), F32), pltpu.SemaphoreType.DMA((2,))],
        compiler_params=_cparams(("arbitrary",)),
        name="moe_combine_ln2",
    )(pos_blocks, pos_blocks, ys, wgt, x1, mods, ln_g, ln_b)


def _moe_plan(idx_t):
    flat = idx_t.reshape(-1)
    onehot = (flat[:, None] == jnp.arange(N_EXPERTS)[None, :]).astype(jnp.int32)
    rank = jnp.take_along_axis(jnp.cumsum(onehot, axis=0), flat[:, None], axis=1)[:, 0] - 1
    counts = jnp.sum(onehot, axis=0)
    tiles = (counts + MOE_TM - 1) // MOE_TM
    tile_end = jnp.cumsum(tiles)
    tile_start = tile_end - tiles
    pos = tile_start[flat] * MOE_TM + rank
    n_valid = tile_end[-1]
    chunks_per_tile = (W_CHUNKS + jnp.maximum(tiles, 1) - 1) // jnp.maximum(tiles, 1)
    i32 = lambda a: a.astype(jnp.int32)
    return i32(pos.reshape(TOP_K, T_ALL)), (i32(tile_start), i32(tiles), i32(chunks_per_tile),
                                            i32(n_valid.reshape(1)))


def kernel(x_prompt, x_sample, cache_na_k, cache_na_v, state_ssd_fwd, state_ssd_bwd, c, c_ctx, w_mod, b_mod, w_in, ssd_conv_w, ssd_conv_b, ssd_dt_bias, ssd_a_log, ssd_d, ssd_norm_w, na_rpb, w_ssd_out, w_na_out, w_o, ln1_g, ln1_b, ln2_g, ln2_b, w_router, b_router, w_gate_up, b_gate_up, w_down, b_down):
    assert w_mod.shape[0] == 1, "single-layer trunk"
    x_ctx = x_prompt.reshape(T_CTX, D_MODEL)
    x_dec = x_sample.reshape(T_DEC, D_MODEL)

    cvec = jnp.concatenate([c_ctx[None], c, jnp.zeros((8 - 1 - DEC_BATCH, D_MODEL), F32)], axis=0)
    mods = _modulation(cvec, w_mod[0], b_mod[0])[:1 + DEC_BATCH].reshape(1 + DEC_BATCH, 6, D_MODEL)

    h = _ln_modulate(x_ctx, x_dec, mods)
    w = w_in[0]
    n_main = 3 * NA_WIDTH + SSD_WIDTH + SSD_CONV_CH
    w_bf = w.astype(BF16)
    w_dt = jnp.pad(w[:, n_main:n_main + 2 * SSD_HEADS], ((0, 0), (0, 128 - 2 * SSD_HEADS))).astype(BF16)
    w_gates = w_bf[:, n_main + 2 * SSD_HEADS:]
    tm = 1024
    ctx_blocks = T_CTX // tm
    q_all = _matmul(h, w_bf, rows=T_ALL, col_block0=0, n_out=NA_WIDTH, name="proj_q")
    k_ctx = _matmul(h, w_bf, rows=T_CTX, col_block0=2, n_out=NA_WIDTH, name="proj_k_ctx")
    k_dec = _matmul(h, w_bf, rows=T_DEC, row_block0=ctx_blocks, col_block0=2, n_out=NA_WIDTH, name="proj_k_dec")
    v_ctx = _matmul(h, w_bf, rows=T_CTX, col_block0=4, n_out=NA_WIDTH, name="proj_v_ctx")
    v_dec = _matmul(h, w_bf, rows=T_DEC, row_block0=ctx_blocks, col_block0=4, n_out=NA_WIDTH, name="proj_v_dec")
    zx = _matmul(h, w_bf, rows=T_ALL, col_block0=6, n_out=SSD_WIDTH + SSD_CONV_CH, name="proj_zxbc")
    gates = _matmul(h, w_gates, rows=T_ALL, name="proj_gates")
    dt_all = _matmul(h, w_dt, rows=T_ALL, tn=128, name="proj_dt")

    na_ctx = _context_attention(q_all, k_ctx, v_ctx)
    kc = cache_na_k[:, 0].reshape(DEC_BATCH, PAST_LEN, NA_WIDTH)
    vc = cache_na_v[:, 0].reshape(DEC_BATCH, PAST_LEN, NA_WIDTH)
    na_dec = _neighbourhood_attention(q_all, k_dec, v_dec, kc, vc, _rpb_table(na_rpb[0]))

    consts = _ssd_consts(ssd_conv_w[0], ssd_conv_b[0], ssd_dt_bias[0], ssd_a_log[0], ssd_d[0])
    dt_t = dt_all[:, :2 * SSD_HEADS].reshape(T_ALL // SSD_CHUNK, SSD_CHUNK, 2, SSD_GROUPS, HEADS_PER_GROUP)
    dt_t = jnp.transpose(dt_t, (0, 2, 3, 4, 1))
    u_ctx, h_f, h_b = _ssd_mixer(zx, dt_all, dt_t, consts, length=SEQ, n_seq=BATCH, seq_block0=0,
                                 emit_state=True)
    h0 = (state_ssd_fwd[:, 0].reshape(DEC_BATCH, SSD_GROUPS, HEADS_PER_GROUP, SSD_HEAD_DIM, SSD_STATE),
          state_ssd_bwd[:, 0].reshape(DEC_BATCH, SSD_GROUPS, HEADS_PER_GROUP, SSD_HEAD_DIM, SSD_STATE))
    (u_dec,) = _ssd_mixer(zx, dt_all, dt_t, consts, length=DEC_SEQ, n_seq=DEC_BATCH,
                          seq_block0=T_CTX // DEC_SEQ, h0=h0)

    y = _merge_branches(na_ctx, na_dec, u_ctx, u_dec, ssd_norm_w[0].reshape(1, SSD_WIDTH),
                        w_na_out[0].astype(BF16), w_ssd_out[0].astype(BF16), gates)
    x1, h2, idx_t, wgt_t = _post_mix(y, w_o[0].astype(BF16), x_ctx, x_dec, mods,
                                     ln1_g[0].reshape(1, D_MODEL), ln1_b[0].reshape(1, D_MODEL),
                                     jnp.transpose(w_router[0]), b_router[0].reshape(N_EXPERTS, 1))

    pos, plan = _moe_plan(idx_t)
    tile_start, tiles, _, n_valid = plan
    pos_d = jnp.transpose(pos.reshape(TOP_K, T_ALL // DISPATCH_TOK, DISPATCH_TOK), (1, 0, 2))
    xs = _dispatch(h2, pos_d, tile_start, tiles, n_valid)
    hid = _moe_gate_up(xs, plan, w_gate_up[0], b_gate_up[0].reshape(N_EXPERTS, 1, 2 * D_EXPERT))
    ys = _moe_down(hid, plan, w_down[0], b_down[0].reshape(N_EXPERTS, 1, D_MODEL))
    pos_c = jnp.transpose(pos.reshape(TOP_K, T_ALL // COMBINE_TOK, COMBINE_TOK), (1, 0, 2))
    wgt = jnp.transpose(wgt_t)
    g2, b2 = ln2_g[0].reshape(1, D_MODEL), ln2_b[0].reshape(1, D_MODEL)
    y_ctx = _combine(ys, pos_c, wgt, x1, mods, g2, b2, rows=T_CTX, row0=0)
    y_dec = _combine(ys, pos_c, wgt, x1, mods, g2, b2, rows=T_DEC, row0=T_CTX)

    return (y_ctx.reshape(BATCH, SEQ, D_MODEL),
            y_dec.reshape(DEC_BATCH, DEC_SEQ, D_MODEL),
            k_ctx.reshape(BATCH, 1, SEQ, NA_HEADS, NA_HEAD_DIM),
            v_ctx.reshape(BATCH, 1, SEQ, NA_HEADS, NA_HEAD_DIM),
            h_f.reshape(BATCH, 1, SSD_HEADS, SSD_HEAD_DIM, SSD_STATE),
            h_b.reshape(BATCH, 1, SSD_HEADS, SSD_HEAD_DIM, SSD_STATE))
```

```python
import functools

import jax
import jax.numpy as jnp
from jax import lax
from jax.experimental import pallas as pl
from jax.experimental.pallas import tpu as pltpu

F32 = jnp.float32
BF16 = jnp.bfloat16

D_MODEL = 2048
BATCH = 32
SEQ = 256
DEC_BATCH = 2
DEC_SEQ = 1024
PAST_LEN = 512
GRID_W = 64
NA_HEADS = 16
NA_HEAD_DIM = 128
NA_WIDTH = NA_HEADS * NA_HEAD_DIM
NA_WIN_R = 8
NA_WIN_C = 16
SSD_HEADS = 32
SSD_HEAD_DIM = 64
SSD_WIDTH = SSD_HEADS * SSD_HEAD_DIM
SSD_GROUPS = 4
SSD_STATE = 128
SSD_CONV = 5
SSD_CHUNK = 128
SSD_CONV_CH = SSD_WIDTH + 2 * SSD_GROUPS * SSD_STATE
N_EXPERTS = 32
TOP_K = 4
D_EXPERT = 2048
SWIGLU_LIMIT = 7.0
SWIGLU_ALPHA = 1.702
DN_ALPHA = 2.0 ** 0.25
LN_EPS = 1e-5

T_CTX = BATCH * SEQ
T_DEC = DEC_BATCH * DEC_SEQ
T_ALL = T_CTX + T_DEC
HEADS_PER_GROUP = SSD_HEADS // SSD_GROUPS
GROUP_W = HEADS_PER_GROUP * SSD_HEAD_DIM
ROWS = DEC_SEQ // GRID_W
RPB_SLOTS = 16

MOE_TM = 256
MOE_TILES = T_ALL * TOP_K // MOE_TM + N_EXPERTS
MOE_ROWS = MOE_TILES * MOE_TM

VMEM_LIMIT = 56 * 1024 * 1024


def _cparams(sem):
    return pltpu.CompilerParams(dimension_semantics=sem, vmem_limit_bytes=VMEM_LIMIT)


def _split3(x):
    hi = x.astype(BF16)
    r1 = x - hi.astype(F32)
    mid = r1.astype(BF16)
    lo = (r1 - mid.astype(F32)).astype(BF16)
    return hi, mid, lo


def _dot(a, b):
    return jnp.dot(a, b, preferred_element_type=F32)


def _dot_nt(a, b):
    return lax.dot_general(a, b, (((1,), (1,)), ((), ())), preferred_element_type=F32)


def _dot_exact_rhs(x, m_bf):
    hi, mid, lo = _split3(x)
    return _dot(hi, m_bf) + (_dot(mid, m_bf) + _dot(lo, m_bf))


def _dot_hi_mid_rhs(x, m_bf):
    hi = x.astype(BF16)
    mid = (x - hi.astype(F32)).astype(BF16)
    return _dot(hi, m_bf) + _dot(mid, m_bf)


def _dot_exact_lhs(m_bf, x):
    hi, mid, lo = _split3(x)
    return _dot(m_bf, hi) + (_dot(m_bf, mid) + _dot(m_bf, lo))


def _dot_nt_f32(a, b):
    ah, am, al = _split3(a)
    bh, bm, bl = _split3(b)
    small = _dot_nt(ah, bl) + _dot_nt(am, bm) + _dot_nt(al, bh)
    mid = _dot_nt(ah, bm) + _dot_nt(am, bh)
    return _dot_nt(ah, bh) + (mid + small)


def _dot_f32(a, b):
    ah, am, al = _split3(a)
    bh, bm, bl = _split3(b)
    small = _dot(ah, bl) + _dot(am, bm) + _dot(al, bh)
    mid = _dot(ah, bm) + _dot(am, bh)
    return _dot(ah, bh) + (mid + small)


def _silu(x):
    return x * jax.nn.sigmoid(x)


def _softplus(x):
    return jnp.maximum(x, 0.0) + jnp.log1p(jnp.exp(-jnp.abs(x)))


def _layer_norm(x):
    mu = jnp.mean(x, axis=-1, keepdims=True)
    xc = x - mu
    var = jnp.mean(xc * xc, axis=-1, keepdims=True)
    return xc * lax.rsqrt(var + LN_EPS)


def _mod_row(i, tm):
    n_ctx = T_CTX // tm
    per_b = DEC_SEQ // tm
    return jnp.where(i < n_ctx, 0, 1 + (i - n_ctx) // per_b)


def _mod_kernel(c_ref, w_ref, b_ref, o_ref):
    o_ref[...] = _dot_f32(_silu(c_ref[...]), w_ref[...]) + b_ref[...]


def _modulation(cvec, w_mod, b_mod):
    tn = 1024
    n = w_mod.shape[1]
    return pl.pallas_call(
        _mod_kernel,
        grid=(n // tn,),
        in_specs=[pl.BlockSpec((8, D_MODEL), lambda j: (0, 0)),
                  pl.BlockSpec((D_MODEL, tn), lambda j: (0, j)),
                  pl.BlockSpec((1, tn), lambda j: (0, j))],
        out_specs=pl.BlockSpec((8, tn), lambda j: (0, j)),
        out_shape=jax.ShapeDtypeStruct((8, n), F32),
        compiler_params=_cparams(("arbitrary",)),
        name="modulation",
    )(cvec, w_mod, b_mod.reshape(1, n))


def _ln_mod_kernel(xc_ref, xd_ref, m_ref, o_ref, *, n_ctx):
    i = pl.program_id(0)
    x = jnp.where(i < n_ctx, xc_ref[...], xd_ref[...])
    y = _layer_norm(x) * (1.0 + m_ref[0, 1:2, :]) + m_ref[0, 0:1, :]
    o_ref[...] = y.astype(o_ref.dtype)


def _ln_modulate(x_ctx, x_dec, mods):
    tm = 512
    n_ctx = T_CTX // tm
    return pl.pallas_call(
        functools.partial(_ln_mod_kernel, n_ctx=n_ctx),
        grid=(T_ALL // tm,),
        in_specs=[pl.BlockSpec((tm, D_MODEL), lambda i: (jnp.minimum(i, n_ctx - 1), 0)),
                  pl.BlockSpec((tm, D_MODEL), lambda i: (jnp.maximum(i - n_ctx, 0), 0)),
                  pl.BlockSpec((1, 6, D_MODEL), lambda i: (_mod_row(i, tm), 0, 0))],
        out_specs=pl.BlockSpec((tm, D_MODEL), lambda i: (i, 0)),
        out_shape=jax.ShapeDtypeStruct((T_ALL, D_MODEL), BF16),
        compiler_params=_cparams(("arbitrary",)),
        name="ln_modulate",
    )(x_ctx, x_dec, mods)


def _mm_kernel(x_ref, w_ref, o_ref):
    o_ref[...] = _dot(x_ref[...], w_ref[...]).astype(o_ref.dtype)


def _matmul(x, w, *, rows, row_block0=0, col_block0=0, n_out=None, tm=1024, tn=1024,
            out_dtype=F32, name="matmul"):
    k = x.shape[1]
    n_out = w.shape[1] if n_out is None else n_out
    return pl.pallas_call(
        _mm_kernel,
        grid=(rows // tm, n_out // tn),
        in_specs=[pl.BlockSpec((tm, k), lambda i, j: (i + row_block0, 0)),
                  pl.BlockSpec((k, tn), lambda i, j: (0, j + col_block0))],
        out_specs=pl.BlockSpec((tm, tn), lambda i, j: (i, j)),
        out_shape=jax.ShapeDtypeStruct((rows, n_out), out_dtype),
        compiler_params=_cparams(("arbitrary", "arbitrary")),
        name=name,
    )(x, w)


def _ctx_attn_kernel(q_ref, k_ref, v_ref, o_ref):
    scale = NA_HEAD_DIM ** -0.5
    for h in range(NA_HEADS):
        sl = slice(h * NA_HEAD_DIM, (h + 1) * NA_HEAD_DIM)
        q = q_ref[:, sl].astype(BF16)
        k = k_ref[:, sl].astype(BF16)
        v = v_ref[:, sl].astype(BF16)
        s = _dot_nt(q, k) * scale
        m = jnp.max(s, axis=-1, keepdims=True)
        p = jnp.exp(s - m)
        l = jnp.sum(p, axis=-1, keepdims=True)
        o_ref[:, sl] = (_dot(p.astype(BF16), v) / l).astype(o_ref.dtype)


def _context_attention(q_all, k_ctx, v_ctx):
    spec = pl.BlockSpec((SEQ, NA_WIDTH), lambda b: (b, 0))
    return pl.pallas_call(
        _ctx_attn_kernel,
        grid=(BATCH,),
        in_specs=[spec, spec, spec],
        out_specs=spec,
        out_shape=jax.ShapeDtypeStruct((T_CTX, NA_WIDTH), BF16),
        compiler_params=_cparams(("arbitrary",)),
        name="context_attention",
    )(q_all, k_ctx, v_ctx)


def _nbr_attn_kernel(q_ref, k_ref, v_ref, kc_ref, vc_ref, rc_ref, o_ref):
    scale = NA_HEAD_DIM ** -0.5
    kr = min(NA_WIN_R, ROWS)
    kc = kc_ref[0].astype(BF16)
    vc = vc_ref[0].astype(BF16)
    for r in range(ROWS):
        r0 = min(max(r - kr // 2, 0), ROWS - kr)
        off = (r0 - r + NA_WIN_R - 1) * GRID_W
        q = q_ref[r * GRID_W:(r + 1) * GRID_W, :].astype(BF16)
        kb = k_ref[r0 * GRID_W:(r0 + kr) * GRID_W, :].astype(BF16)
        vb = v_ref[r0 * GRID_W:(r0 + kr) * GRID_W, :].astype(BF16)
        s_loc = _dot_nt(q, kb) * scale + rc_ref[0, :, off:off + kr * GRID_W]
        s_ctx = _dot_nt(q, kc) * scale
        m = jnp.maximum(jnp.max(s_loc, axis=-1, keepdims=True),
                        jnp.max(s_ctx, axis=-1, keepdims=True))
        p_loc = jnp.exp(s_loc - m)
        p_ctx = jnp.exp(s_ctx - m)
        l = jnp.sum(p_loc, axis=-1, keepdims=True) + jnp.sum(p_ctx, axis=-1, keepdims=True)
        o = _dot(p_loc.astype(BF16), vb) + _dot(p_ctx.astype(BF16), vc)
        o_ref[r * GRID_W:(r + 1) * GRID_W, :] = (o / l).astype(o_ref.dtype)


def _rpb_table(rpb):
    col = jnp.arange(GRID_W)
    c0 = jnp.clip(col - NA_WIN_C // 2, 0, GRID_W - NA_WIN_C)
    col_mask = (col[None, :] >= c0[:, None]) & (col[None, :] < c0[:, None] + NA_WIN_C)
    dc_idx = jnp.clip(col[None, :] - col[:, None] + NA_WIN_C - 1, 0, 2 * NA_WIN_C - 2)
    pick = (dc_idx[:, :, None] == jnp.arange(2 * NA_WIN_C - 1)[None, None, :]).astype(F32)
    t = jnp.einsum('hdj,qkj->hqdk', rpb, pick, precision=lax.Precision.HIGHEST)
    t = jnp.where(col_mask[None, :, None, :], t, -jnp.inf)
    t = jnp.pad(t, ((0, 0), (0, 0), (0, RPB_SLOTS - t.shape[2]), (0, 0)))
    return t.reshape(NA_HEADS, GRID_W, RPB_SLOTS * GRID_W)


def _neighbourhood_attention(q_all, k_dec, v_dec, kc, vc, rc):
    q_row0 = T_CTX // DEC_SEQ
    hd = NA_HEAD_DIM
    return pl.pallas_call(
        _nbr_attn_kernel,
        grid=(DEC_BATCH, NA_HEADS),
        in_specs=[pl.BlockSpec((DEC_SEQ, hd), lambda b, h: (b + q_row0, h)),
                  pl.BlockSpec((DEC_SEQ, hd), lambda b, h: (b, h)),
                  pl.BlockSpec((DEC_SEQ, hd), lambda b, h: (b, h)),
                  pl.BlockSpec((1, PAST_LEN, hd), lambda b, h: (b, 0, h)),
                  pl.BlockSpec((1, PAST_LEN, hd), lambda b, h: (b, 0, h)),
                  pl.BlockSpec((1, GRID_W, RPB_SLOTS * GRID_W), lambda b, h: (h, 0, 0))],
        out_specs=pl.BlockSpec((DEC_SEQ, hd), lambda b, h: (b, h)),
        out_shape=jax.ShapeDtypeStruct((T_DEC, NA_WIDTH), BF16),
        compiler_params=_cparams(("arbitrary", "arbitrary")),
        name="neighbourhood_attention",
    )(q_all, k_dec, v_dec, kc, vc, rc)


CONV_HALO = 8


def _conv_silu(u_ref, w_ref, b_ref, pad_s, length):
    width = u_ref.shape[1]
    halo = jnp.zeros((CONV_HALO, width), F32)
    pad_s[0:CONV_HALO, 0:width] = halo
    pad_s[CONV_HALO + length:2 * CONV_HALO + length, 0:width] = halo
    pad_s[CONV_HALO:CONV_HALO + length, 0:width] = u_ref[...]
    acc = b_ref[...]
    for k in range(SSD_CONV):
        d = k - SSD_CONV // 2
        acc = acc + pad_s[CONV_HALO + d:CONV_HALO + d + length, 0:width] * w_ref[k:k + 1, :]
    return _silu(acc)


def _ssd_kernel(*refs, length, has_h0, emit_state):
    (xs_ref, b_ref, c_ref, z_ref, dt_ref, dtt_ref, cwx_ref, cwb_ref, cwc_ref,
     cbx_ref, cbb_ref, cbc_ref, dtb_row_ref, dtb_col_ref, al_row_ref, al_col_ref,
     e_ref, d_ref) = refs[:18]
    pos = 18
    if has_h0:
        h0f_ref, h0b_ref = refs[pos:pos + 2]
        pos += 2
    u_ref = refs[pos]
    pos += 1
    if emit_state:
        hf_ref, hb_ref = refs[pos:pos + 2]
        pos += 2
    xs_s, b_s, c_s, y_s, st_s, pad_s = refs[pos:]

    q = SSD_CHUNK
    nc = length // q
    xs_s[...] = _conv_silu(xs_ref, cwx_ref, cbx_ref, pad_s, length)
    b_s[...] = _conv_silu(b_ref, cwb_ref, cbb_ref, pad_s, length)
    c_s[...] = _conv_silu(c_ref, cwc_ref, cbc_ref, pad_s, length)
    y_s[...] = d_ref[0] * xs_s[...]

    for dirn in range(2):
        if has_h0:
            h0 = (h0f_ref if dirn == 0 else h0b_ref)[0, 0]
            st_s[dirn] = jnp.transpose(h0.reshape(GROUP_W, SSD_STATE))
        else:
            st_s[dirn] = jnp.zeros((SSD_STATE, GROUP_W), F32)

    ri = lax.broadcasted_iota(jnp.int32, (q, q), 0)
    ci = lax.broadcasted_iota(jnp.int32, (q, q), 1)
    lower = ri >= ci
    upper = ri <= ci
    lower_bf = jnp.where(lower, 1.0, 0.0).astype(BF16)
    upper_bf = jnp.where(upper, 1.0, 0.0).astype(BF16)

    def chunk(c, dirn):
        r0 = c * q if isinstance(c, int) else pl.multiple_of(c * q, q)
        tri = lower if dirn == 0 else upper
        xs_c = xs_s[pl.ds(r0, q), :]
        bc = b_s[pl.ds(r0, q), :]
        cc = c_s[pl.ds(r0, q), :]
        dtp = _softplus(dt_ref[pl.ds(r0, q), :] + dtb_row_ref[...])
        da = dtp * (-jnp.exp(al_row_ref[...]))
        pre = _dot_exact_lhs(lower_bf, da)
        cs = pre if dirn == 0 else pre[q - 1:q, :] - pre + da
        e_bf = e_ref[dirn, 0]
        dt_e = _dot_hi_mid_rhs(dtp, e_bf)
        cs_e = _dot_exact_rhs(cs, e_bf)
        dt_t = _softplus(dtt_ref[c, dirn, 0] + dtb_col_ref[dirn, 0])
        da_t = dt_t * (-jnp.exp(al_col_ref[dirn, 0]))
        cs_t = _dot_exact_rhs(da_t, upper_bf if dirn == 0 else lower_bf)

        cb = _dot_nt(cc.astype(BF16), bc.astype(BF16))
        bt = jnp.transpose(bc).astype(BF16)
        end = q - 1 if dirn == 0 else 0
        cs_end = cs_e[end:end + 1, :]
        xdt = xs_c * dt_e
        st = st_s[dirn]
        y = _dot(cc.astype(BF16), st.astype(BF16)) * jnp.exp(cs_e)
        st_s[dirn] = jnp.exp(cs_end) * st + _dot(bt, (xdt * jnp.exp(cs_end - cs_e)).astype(BF16))
        xdt_bf = xdt.astype(BF16)
        parts = []
        for r in range(HEADS_PER_GROUP):
            sl = slice(r * SSD_HEAD_DIM, (r + 1) * SSD_HEAD_DIM)
            diff = cs_e[:, r * SSD_HEAD_DIM:r * SSD_HEAD_DIM + 1] - cs_t[r:r + 1, :]
            lm = jnp.exp(jnp.where(tri, diff, -jnp.inf))
            parts.append(_dot((cb * lm).astype(BF16), xdt_bf[:, sl]))
        y = y + jnp.concatenate(parts, axis=-1)
        y_s[pl.ds(r0, q), :] = y_s[pl.ds(r0, q), :] + y

    if nc <= 2:
        for i in range(nc):
            chunk(i, 0)
            chunk(nc - 1 - i, 1)
    else:
        def both(i, carry):
            chunk(i, 0)
            chunk(nc - 1 - i, 1)
            return carry
        lax.fori_loop(0, nc, both, 0)

    u_ref[...] = y_s[...] * _silu(z_ref[...])
    if emit_state:
        hf_ref[0, 0] = jnp.transpose(st_s[0]).reshape(HEADS_PER_GROUP, SSD_HEAD_DIM, SSD_STATE)
        hb_ref[0, 0] = jnp.transpose(st_s[1]).reshape(HEADS_PER_GROUP, SSD_HEAD_DIM, SSD_STATE)


def _ssd_mixer(zx, dt_all, dt_t, consts, *, length, n_seq, seq_block0, h0=None, emit_state=False):
    (cw, cb, dtb_row, dtb_col, al_row, al_col, expand, d_row) = consts
    g_w, n_s = GROUP_W, SSD_STATE
    xs_cb0 = D_MODEL // g_w
    b_cb0 = (D_MODEL + SSD_WIDTH) // n_s
    c_cb0 = b_cb0 + SSD_GROUPS
    cw_b0 = SSD_WIDTH // n_s
    nck = length // SSD_CHUNK
    in_specs = [
        pl.BlockSpec((length, g_w), lambda s, g: (s + seq_block0, xs_cb0 + g)),
        pl.BlockSpec((length, n_s), lambda s, g: (s + seq_block0, b_cb0 + g)),
        pl.BlockSpec((length, n_s), lambda s, g: (s + seq_block0, c_cb0 + g)),
        pl.BlockSpec((length, g_w), lambda s, g: (s + seq_block0, g)),
        pl.BlockSpec((length, 128), lambda s, g: (s + seq_block0, 0)),
        pl.BlockSpec((nck, 2, 1, HEADS_PER_GROUP, SSD_CHUNK), lambda s, g: (s + seq_block0, 0, g, 0, 0)),
        pl.BlockSpec((SSD_CONV, g_w), lambda s, g: (0, g)),
        pl.BlockSpec((SSD_CONV, n_s), lambda s, g: (0, cw_b0 + g)),
        pl.BlockSpec((SSD_CONV, n_s), lambda s, g: (0, cw_b0 + SSD_GROUPS + g)),
        pl.BlockSpec((1, g_w), lambda s, g: (0, g)),
        pl.BlockSpec((1, n_s), lambda s, g: (0, cw_b0 + g)),
        pl.BlockSpec((1, n_s), lambda s, g: (0, cw_b0 + SSD_GROUPS + g)),
        pl.BlockSpec((1, 128), lambda s, g: (0, 0)),
        pl.BlockSpec((2, 1, HEADS_PER_GROUP, SSD_CHUNK), lambda s, g: (0, g, 0, 0)),
        pl.BlockSpec((1, 128), lambda s, g: (0, 0)),
        pl.BlockSpec((2, 1, HEADS_PER_GROUP, SSD_CHUNK), lambda s, g: (0, g, 0, 0)),
        pl.BlockSpec((2, 1, 128, g_w), lambda s, g: (0, g, 0, 0)),
        pl.BlockSpec((1, 1, g_w), lambda s, g: (g, 0, 0)),
    ]
    args = [zx, zx, zx, zx, dt_all, dt_t, cw, cw, cw, cb, cb, cb,
            dtb_row, dtb_col, al_row, al_col, expand, d_row]
    st_spec = pl.BlockSpec((1, 1, HEADS_PER_GROUP, SSD_HEAD_DIM, n_s), lambda s, g: (s, g, 0, 0, 0))
    if h0 is not None:
        in_specs += [st_spec, st_spec]
        args += list(h0)
    out_specs = [pl.BlockSpec((length, g_w), lambda s, g: (s, g))]
    out_shape = [jax.ShapeDtypeStruct((n_seq * length, SSD_WIDTH), F32)]
    if emit_state:
        st_shape = jax.ShapeDtypeStruct((n_seq, SSD_GROUPS, HEADS_PER_GROUP, SSD_HEAD_DIM, n_s), F32)
        out_specs += [st_spec, st_spec]
        out_shape += [st_shape, st_shape]
    return pl.pallas_call(
        functools.partial(_ssd_kernel, length=length, has_h0=h0 is not None, emit_state=emit_state),
        grid=(n_seq, SSD_GROUPS),
        in_specs=in_specs,
        out_specs=out_specs,
        out_shape=out_shape,
        scratch_shapes=[pltpu.VMEM((length, g_w), F32), pltpu.VMEM((length, n_s), F32),
                        pltpu.VMEM((length, n_s), F32), pltpu.VMEM((length, g_w), F32),
                        pltpu.VMEM((2, n_s, g_w), F32),
                        pltpu.VMEM((length + 2 * CONV_HALO, g_w), F32)],
        compiler_params=_cparams(("arbitrary", "arbitrary")),
        name="ssd_mixer_%d" % length,
    )(*args)


def _ssd_consts(conv_w, conv_b, dt_bias, a_log, d_skip):
    hpg = HEADS_PER_GROUP
    dtb_row = jnp.pad(dt_bias.reshape(1, 2 * SSD_HEADS), ((0, 0), (0, 128 - 2 * SSD_HEADS)))
    col = lambda p: jnp.broadcast_to(p.reshape(2, SSD_GROUPS, hpg, 1), (2, SSD_GROUPS, hpg, SSD_CHUNK))
    al_row = jnp.pad(a_log.reshape(1, 2 * SSD_HEADS), ((0, 0), (0, 128 - 2 * SSD_HEADS)))
    src = (jnp.arange(2)[:, None, None] * SSD_HEADS + jnp.arange(SSD_GROUPS)[None, :, None] * hpg
           + jnp.arange(GROUP_W)[None, None, :] // SSD_HEAD_DIM)
    expand = (jnp.arange(128)[None, None, :, None] == src[:, :, None, :]).astype(BF16)
    d_row = jnp.repeat(d_skip.reshape(SSD_GROUPS, 1, hpg), SSD_HEAD_DIM, axis=-1)
    return (conv_w, conv_b.reshape(1, SSD_CONV_CH), dtb_row, col(dt_bias), al_row, col(a_log), expand, d_row)


def _merge_kernel(nac_ref, nad_ref, uc_ref, ud_ref, nw_ref, w1_ref, w2_ref, g_na_ref, g_ssd_ref,
                  o_ref, *, n_ctx):
    is_ctx = pl.program_id(0) < n_ctx
    a1 = jnp.where(is_ctx, nac_ref[...], nad_ref[...]).astype(BF16)
    u = jnp.where(is_ctx, uc_ref[...], ud_ref[...])
    r = lax.rsqrt(jnp.mean(u * u, axis=-1, keepdims=True) + LN_EPS)
    a2 = (u * r * nw_ref[...]).astype(BF16)
    o = (jax.nn.sigmoid(g_na_ref[...]) * _dot(a1, w1_ref[...])
         + jax.nn.sigmoid(g_ssd_ref[...]) * _dot(a2, w2_ref[...]))
    o_ref[...] = o.astype(o_ref.dtype)


def _merge_branches(na_ctx, na_dec, u_ctx, u_dec, norm_w, w_na_out, w_ssd_out, gates):
    tm = 256
    n_ctx = T_CTX // tm
    ctx_rows = lambda i: (jnp.minimum(i, n_ctx - 1), 0)
    dec_rows = lambda i: (jnp.maximum(i - n_ctx, 0), 0)
    const = lambda i: (0, 0)
    resident = pl.Buffered(1)
    return pl.pallas_call(
        functools.partial(_merge_kernel, n_ctx=n_ctx),
        grid=(T_ALL // tm,),
        in_specs=[pl.BlockSpec((tm, D_MODEL), ctx_rows), pl.BlockSpec((tm, D_MODEL), dec_rows),
                  pl.BlockSpec((tm, D_MODEL), ctx_rows), pl.BlockSpec((tm, D_MODEL), dec_rows),
                  pl.BlockSpec((1, D_MODEL), const),
                  pl.BlockSpec((D_MODEL, D_MODEL), const, pipeline_mode=resident),
                  pl.BlockSpec((D_MODEL, D_MODEL), const, pipeline_mode=resident),
                  pl.BlockSpec((tm, D_MODEL), lambda i: (i, 1)),
                  pl.BlockSpec((tm, D_MODEL), lambda i: (i, 0))],
        out_specs=pl.BlockSpec((tm, D_MODEL), lambda i: (i, 0)),
        out_shape=jax.ShapeDtypeStruct((T_ALL, D_MODEL), BF16),
        compiler_params=_cparams(("arbitrary",)),
        name="merge_branches",
    )(na_ctx, na_dec, u_ctx, u_dec, norm_w, w_na_out, w_ssd_out, gates, gates)


def _post_mix_kernel(y_ref, wo_ref, xc_ref, xd_ref, m_ref, g_ref, b_ref, wr_ref, br_ref,
                     x1_ref, h2_ref, idx_ref, wgt_ref, *, n_ctx):
    i = pl.program_id(0)
    x = jnp.where(i < n_ctx, xc_ref[...], xd_ref[...])
    mix = _dot(y_ref[...], wo_ref[...])
    x1 = _layer_norm(DN_ALPHA * x + m_ref[0, 2:3, :] * mix) * g_ref[...] + b_ref[...]
    x1_ref[...] = x1
    h2 = _layer_norm(x1) * (1.0 + m_ref[0, 4:5, :]) + m_ref[0, 3:4, :]
    h2_ref[...] = h2
    logits = _dot_nt_f32(wr_ref[...], h2) + br_ref[...]
    eidx = lax.broadcasted_iota(jnp.int32, logits.shape, 0)
    vals, idxs = [], []
    for _ in range(TOP_K):
        m = jnp.max(logits, axis=0, keepdims=True)
        sel = jnp.min(jnp.where(logits == m, eidx, N_EXPERTS), axis=0, keepdims=True)
        logits = jnp.where(eidx == sel, -jnp.inf, logits)
        vals.append(m)
        idxs.append(sel)
    ex = [jnp.exp(v - vals[0]) for v in vals]
    tot = ex[0] + ex[1] + ex[2] + ex[3]
    idx_ref[...] = jnp.concatenate(idxs, axis=0)
    wgt_ref[...] = jnp.concatenate([e / tot for e in ex], axis=0)


def _post_mix(y, w_o, x_ctx, x_dec, mods, ln_g, ln_b, w_router_t, b_router):
    tm = 512
    n_ctx = T_CTX // tm
    row = lambda i: (i, 0)
    const = lambda i: (0, 0)
    return pl.pallas_call(
        functools.partial(_post_mix_kernel, n_ctx=n_ctx),
        grid=(T_ALL // tm,),
        in_specs=[pl.BlockSpec((tm, D_MODEL), row),
                  pl.BlockSpec((D_MODEL, D_MODEL), const, pipeline_mode=pl.Buffered(1)),
                  pl.BlockSpec((tm, D_MODEL), lambda i: (jnp.minimum(i, n_ctx - 1), 0)),
                  pl.BlockSpec((tm, D_MODEL), lambda i: (jnp.maximum(i - n_ctx, 0), 0)),
                  pl.BlockSpec((1, 6, D_MODEL), lambda i: (_mod_row(i, tm), 0, 0)),
                  pl.BlockSpec((1, D_MODEL), const), pl.BlockSpec((1, D_MODEL), const),
                  pl.BlockSpec((N_EXPERTS, D_MODEL), const), pl.BlockSpec((N_EXPERTS, 1), const)],
        out_specs=[pl.BlockSpec((tm, D_MODEL), row), pl.BlockSpec((tm, D_MODEL), row),
                   pl.BlockSpec((TOP_K, tm), lambda i: (0, i)), pl.BlockSpec((TOP_K, tm), lambda i: (0, i))],
        out_shape=[jax.ShapeDtypeStruct((T_ALL, D_MODEL), F32), jax.ShapeDtypeStruct((T_ALL, D_MODEL), F32),
                   jax.ShapeDtypeStruct((TOP_K, T_ALL), jnp.int32), jax.ShapeDtypeStruct((TOP_K, T_ALL), F32)],
        compiler_params=_cparams(("arbitrary",)),
        name="post_mix_router",
    )(y, w_o, x_ctx, x_dec, mods, ln_g, ln_b, w_router_t, b_router)


DISPATCH_TOK = 128


def _tile_rows(t):
    return pl.ds(pl.multiple_of(t * MOE_TM, MOE_TM), MOE_TM)


def _dispatch_kernel(t0_ref, nt_ref, nv_ref, pos_ref, x_hbm, xs_hbm, buf, zero_s, sem_in, sem_out, sem_z,
                     *, n_steps):
    i = pl.program_id(0)
    slot = i % 3

    def zero_copy(tile):
        return pltpu.make_async_copy(zero_s, xs_hbm.at[_tile_rows(tile)], sem_z)

    def in_copy(step, s):
        rows = pl.ds(pl.multiple_of(step * DISPATCH_TOK, DISPATCH_TOK), DISPATCH_TOK)
        return pltpu.make_async_copy(x_hbm.at[rows], buf.at[s], sem_in.at[s])

    def row_copy(s, t, dst):
        return pltpu.make_async_copy(buf.at[s, pl.ds(t, 1)], xs_hbm.at[pl.ds(dst, 1)], sem_out.at[s])

    def wait_rows(s):
        for _ in range(TOP_K * DISPATCH_TOK):
            row_copy(s, 0, 0).wait()

    @pl.when(i == 0)
    def _():
        in_copy(0, 0).start()
        zero_s[...] = jnp.zeros(zero_s.shape, zero_s.dtype)
        for e in range(N_EXPERTS):
            @pl.when(nt_ref[e] > 0)
            def _():
                zero_copy(t0_ref[e] + nt_ref[e] - 1).start()

        def tail_start(t, c):
            zero_copy(t).start()
            return c
        lax.fori_loop(nv_ref[0], MOE_TILES, tail_start, 0)
        for e in range(N_EXPERTS):
            @pl.when(nt_ref[e] > 0)
            def _():
                zero_copy(0).wait()

        def tail_wait(t, c):
            zero_copy(0).wait()
            return c
        lax.fori_loop(nv_ref[0], MOE_TILES, tail_wait, 0)

    in_copy(i, slot).wait()

    @pl.when(i + 1 < n_steps)
    def _():
        in_copy(i + 1, (i + 1) % 3).start()

    for k in range(TOP_K):
        for t in range(DISPATCH_TOK):
            row_copy(slot, t, pos_ref[0, k, t]).start(priority=t % 2)

    @pl.when(i > 0)
    def _():
        wait_rows((i + 2) % 3)

    @pl.when(i == n_steps - 1)
    def _():
        wait_rows(slot)


def _dispatch(h2, pos_blocks, tile_start, tiles, n_valid):
    n_steps = T_ALL // DISPATCH_TOK
    grid_spec = pltpu.PrefetchScalarGridSpec(
        num_scalar_prefetch=3,
        grid=(n_steps,),
        in_specs=[pl.BlockSpec((1, TOP_K, DISPATCH_TOK), lambda i, *_: (i, 0, 0), memory_space=pltpu.SMEM),
                  pl.BlockSpec(memory_space=pl.ANY)],
        out_specs=pl.BlockSpec(memory_space=pl.ANY),
        scratch_shapes=[pltpu.VMEM((3, DISPATCH_TOK, D_MODEL), F32),
                        pltpu.VMEM((MOE_TM, D_MODEL), F32),
                        pltpu.SemaphoreType.DMA((3,)), pltpu.SemaphoreType.DMA((3,)),
                        pltpu.SemaphoreType.DMA(())],
    )
    return pl.pallas_call(
        functools.partial(_dispatch_kernel, n_steps=n_steps),
        grid_spec=grid_spec,
        out_shape=jax.ShapeDtypeStruct((MOE_ROWS, D_MODEL), F32),
        compiler_params=_cparams(("arbitrary",)),
        name="moe_dispatch",
    )(tile_start, tiles, n_valid, pos_blocks, h2)


def _expert_tile_loop(t0, nt, in_copy, out_copy, compute, after_tile):
    @pl.when(nt > 0)
    def _():
        in_copy(t0, 0).start()

    def body(i, carry):
        slot = i % 2
        in_copy(t0 + i, slot).wait()

        @pl.when(i + 1 < nt)
        def _():
            in_copy(t0 + i + 1, 1 - slot).start()

        after_tile(i)

        @pl.when(i >= 2)
        def _():
            out_copy(t0 + i - 2, slot).wait()

        compute(slot)
        out_copy(t0 + i, slot).start()
        return carry

    lax.fori_loop(0, nt, body, 0)

    @pl.when(nt >= 2)
    def _():
        out_copy(t0, nt % 2).wait()

    @pl.when(nt >= 1)
    def _():
        out_copy(t0, (nt + 1) % 2).wait()


def _zero_tail_tiles(is_last, n_valid, obuf, out_copy):
    @pl.when(is_last)
    def _():
        obuf[0] = jnp.zeros(obuf.shape[1:], obuf.dtype)

        def body(t, carry):
            out_copy(t, 0).start()
            out_copy(t, 0).wait()
            return carry
        lax.fori_loop(n_valid, MOE_TILES, body, 0)


W_CHUNKS = 16


def _stream_weights(step, n_steps, chunks_per_tile, chunk_copy, stage, w_s, tile_loop):
    def start_chunks(target_step, lo, hi):
        def body(c, carry):
            chunk_copy(target_step, c).start()
            return carry
        lax.fori_loop(lo, hi, body, 0)

    @pl.when(step == 0)
    def _():
        start_chunks(0, 0, W_CHUNKS)

    for _ in range(W_CHUNKS):
        chunk_copy(0, 0).wait()
    w_s[...] = stage[...].astype(BF16)

    has_next = step + 1 < n_steps

    def after_tile(i):
        @pl.when(has_next)
        def _():
            start_chunks(step + 1, jnp.minimum(i * chunks_per_tile, W_CHUNKS),
                         jnp.minimum((i + 1) * chunks_per_tile, W_CHUNKS))

    n_tiles = tile_loop(after_tile)

    @pl.when(has_next)
    def _():
        start_chunks(step + 1, jnp.minimum(n_tiles * chunks_per_tile, W_CHUNKS), W_CHUNKS)


GU_CHUNKS = 32
GU_RING = 8


def _gate_up_kernel(t0_ref, nt_ref, cpt_ref, nv_ref, xs_hbm, w_hbm, bg_ref, bu_ref, hid_hbm,
                    ring, w_s, xbuf, obuf, sem_w, sem_in, sem_out):
    e = pl.program_id(0)
    cur = e % 2
    nxt = 1 - cur
    per_mat = GU_CHUNKS // 2
    rows = D_MODEL // per_mat

    def chunk_copy(expert, c):
        r = pl.multiple_of((c % per_mat) * rows, rows)
        col = pl.multiple_of((c // per_mat) * D_EXPERT, D_EXPERT)
        return pltpu.make_async_copy(w_hbm.at[expert, pl.ds(r, rows), pl.ds(col, D_EXPERT)],
                                     ring.at[c % GU_RING], sem_w.at[c % GU_RING])

    def issue(expert, lo, hi):
        def body(c, carry):
            chunk_copy(expert, c).start()
            return carry
        lax.fori_loop(lo, hi, body, 0)

    def land(buf, lo, hi):
        def body(c, carry):
            chunk_copy(0, c).wait()
            r = pl.multiple_of((c % per_mat) * rows, rows)
            w_s[buf, c // per_mat, pl.ds(r, rows), :] = ring[c % GU_RING].astype(BF16)
            return carry
        lax.fori_loop(lo, hi, body, 0)

    def fetch_exposed(expert, buf, lo):
        for rr in range(GU_CHUNKS // GU_RING):
            a = jnp.maximum(lo, rr * GU_RING)
            issue(expert, a, (rr + 1) * GU_RING)
            land(buf, a, (rr + 1) * GU_RING)

    @pl.when(e == 0)
    def _():
        fetch_exposed(0, 0, 0)

    has_next = e + 1 < N_EXPERTS
    k = cpt_ref[e]
    nt = nt_ref[e]
    clip = lambda v: jnp.clip(v, 0, GU_CHUNKS)

    def after_tile(i):
        @pl.when(has_next)
        def _():
            land(nxt, clip((i - 1) * k), clip(i * k))
            issue(e + 1, clip(i * k), clip((i + 1) * k))

    def in_copy(t, slot):
        return pltpu.make_async_copy(xs_hbm.at[_tile_rows(t)], xbuf.at[slot], sem_in.at[slot])

    def out_copy(t, slot):
        return pltpu.make_async_copy(obuf.at[slot], hid_hbm.at[_tile_rows(t)], sem_out.at[slot])

    def compute(slot):
        x = xbuf[slot].astype(BF16)
        gate = jnp.minimum(_dot(x, w_s[cur, 0]) + bg_ref[0], SWIGLU_LIMIT)
        up = jnp.clip(_dot(x, w_s[cur, 1]) + bu_ref[0], -SWIGLU_LIMIT, SWIGLU_LIMIT)
        hid = (up + 1.0) * gate * jax.nn.sigmoid(SWIGLU_ALPHA * gate)
        obuf[slot] = hid.astype(obuf.dtype)

    _expert_tile_loop(t0_ref[e], nt, in_copy, out_copy, compute, after_tile)

    @pl.when(has_next)
    def _():
        land(nxt, clip((nt - 1) * k), clip(nt * k))
        fetch_exposed(e + 1, nxt, clip(nt * k))

    _zero_tail_tiles(e == N_EXPERTS - 1, nv_ref[0], obuf, out_copy)


def _moe_gate_up(xs, plan, w_gate_up, b_gate_up):
    grid_spec = pltpu.PrefetchScalarGridSpec(
        num_scalar_prefetch=4,
        grid=(N_EXPERTS,),
        in_specs=[pl.BlockSpec(memory_space=pl.ANY), pl.BlockSpec(memory_space=pl.ANY),
                  pl.BlockSpec((1, 1, D_EXPERT), lambda e, *_: (e, 0, 0)),
                  pl.BlockSpec((1, 1, D_EXPERT), lambda e, *_: (e, 0, 1))],
        out_specs=pl.BlockSpec(memory_space=pl.ANY),
        scratch_shapes=[pltpu.VMEM((GU_RING, D_MODEL // (GU_CHUNKS // 2), D_EXPERT), F32),
                        pltpu.VMEM((2, 2, D_MODEL, D_EXPERT), BF16),
                        pltpu.VMEM((2, MOE_TM, D_MODEL), F32), pltpu.VMEM((2, MOE_TM, D_EXPERT), BF16),
                        pltpu.SemaphoreType.DMA((GU_RING,)), pltpu.SemaphoreType.DMA((2,)),
                        pltpu.SemaphoreType.DMA((2,))],
    )
    return pl.pallas_call(
        _gate_up_kernel,
        grid_spec=grid_spec,
        out_shape=jax.ShapeDtypeStruct((MOE_ROWS, D_EXPERT), BF16),
        compiler_params=_cparams(("arbitrary",)),
        name="moe_gate_up",
    )(*plan, xs, w_gate_up, b_gate_up, b_gate_up)


def _down_kernel(t0_ref, nt_ref, cpt_ref, nv_ref, hid_hbm, w_hbm, b_ref, ys_hbm,
                 stage, w_s, hbuf, obuf, sem_w, sem_in, sem_out):
    e = pl.program_id(0)
    rows = D_EXPERT // W_CHUNKS

    def chunk_copy(step, c):
        r = pl.ds(pl.multiple_of(c * rows, rows), rows)
        return pltpu.make_async_copy(w_hbm.at[step, r], stage.at[r], sem_w)

    def in_copy(t, slot):
        return pltpu.make_async_copy(hid_hbm.at[_tile_rows(t)], hbuf.at[slot], sem_in.at[slot])

    def out_copy(t, slot):
        return pltpu.make_async_copy(obuf.at[slot], ys_hbm.at[_tile_rows(t)], sem_out.at[slot])

    def compute(slot):
        obuf[slot] = _dot(hbuf[slot], w_s[...]) + b_ref[0]

    def tile_loop(after_tile):
        _expert_tile_loop(t0_ref[e], nt_ref[e], in_copy, out_copy, compute, after_tile)
        return nt_ref[e]

    _stream_weights(e, N_EXPERTS, cpt_ref[e], chunk_copy, stage, w_s, tile_loop)
    _zero_tail_tiles(e == N_EXPERTS - 1, nv_ref[0], obuf, out_copy)


def _moe_down(hid, plan, w_down, b_down):
    grid_spec = pltpu.PrefetchScalarGridSpec(
        num_scalar_prefetch=4,
        grid=(N_EXPERTS,),
        in_specs=[pl.BlockSpec(memory_space=pl.ANY), pl.BlockSpec(memory_space=pl.ANY),
                  pl.BlockSpec((1, 1, D_MODEL), lambda e, *_: (e, 0, 0))],
        out_specs=pl.BlockSpec(memory_space=pl.ANY),
        scratch_shapes=[pltpu.VMEM((D_EXPERT, D_MODEL), F32), pltpu.VMEM((D_EXPERT, D_MODEL), BF16),
                        pltpu.VMEM((2, MOE_TM, D_EXPERT), BF16), pltpu.VMEM((2, MOE_TM, D_MODEL), F32),
                        pltpu.SemaphoreType.DMA(()), pltpu.SemaphoreType.DMA((2,)),
                        pltpu.SemaphoreType.DMA((2,))],
    )
    return pl.pallas_call(
        _down_kernel,
        grid_spec=grid_spec,
        out_shape=jax.ShapeDtypeStruct((MOE_ROWS, D_MODEL), F32),
        compiler_params=_cparams(("arbitrary",)),
        name="moe_down",
    )(*plan, hid, w_down, b_down)


COMBINE_TOK = 64


def _start_rows(src_hbm, idx_ref, idx_lead, dst, sem, count):
    for r in range(count):
        pltpu.make_async_copy(src_hbm.at[pl.ds(idx_ref[idx_lead + (r,)], 1)], dst.at[pl.ds(r, 1)],
                              sem).start(priority=r % 2)


def _wait_rows(src_hbm, dst, sem, count):
    for _ in range(count):
        pltpu.make_async_copy(src_hbm.at[pl.ds(0, 1)], dst.at[pl.ds(0, 1)], sem).wait()


def _combine_kernel(pos_ref, pos_next_ref, ys_hbm, wgt_ref, x1_ref, m_ref, g_ref, b_ref, o_ref, buf, sem,
                    *, n_steps):
    i = pl.program_id(0)
    slot = i % 2

    def start(p_ref, s):
        for k in range(TOP_K):
            _start_rows(ys_hbm, p_ref, (0, k), buf.at[s, k], sem.at[s], COMBINE_TOK)

    @pl.when(i == 0)
    def _():
        start(pos_ref, 0)

    @pl.when(i + 1 < n_steps)
    def _():
        start(pos_next_ref, 1 - slot)

    _wait_rows(ys_hbm, buf.at[slot, 0], sem.at[slot], TOP_K * COMBINE_TOK)
    w = wgt_ref[...]
    ffn = w[:, 0:1] * buf[slot, 0]
    for k in range(1, TOP_K):
        ffn = ffn + w[:, k:k + 1] * buf[slot, k]
    x2 = _layer_norm(DN_ALPHA * x1_ref[...] + m_ref[0, 5:6, :] * ffn) * g_ref[...] + b_ref[...]
    o_ref[...] = x2


def _combine(ys, pos_blocks, wgt, x1, mods, ln_g, ln_b, *, rows, row0):
    tm = COMBINE_TOK
    b0 = row0 // tm
    n = rows // tm
    const = lambda i: (0, 0)
    pos_spec = lambda f: pl.BlockSpec((1, TOP_K, tm), f, memory_space=pltpu.SMEM)
    return pl.pallas_call(
        functools.partial(_combine_kernel, n_steps=n),
        grid=(n,),
        in_specs=[pos_spec(lambda i: (i + b0, 0, 0)),
                  pos_spec(lambda i: (jnp.minimum(i + 1, n - 1) + b0, 0, 0)),
                  pl.BlockSpec(memory_space=pl.ANY),
                  pl.BlockSpec((tm, TOP_K), lambda i: (i + b0, 0)),
                  pl.BlockSpec((tm, D_MODEL), lambda i: (i + b0, 0)),
                  pl.BlockSpec((1, 6, D_MODEL), lambda i: (_mod_row(i + b0, tm), 0, 0)),
                  pl.BlockSpec((1, D_MODEL), const), pl.BlockSpec((1, D_MODEL), const)],
        out_specs=pl.BlockSpec((tm, D_MODEL), lambda i: (i, 0)),
        out_shape=jax.ShapeDtypeStruct((rows, D_MODEL), F32),
        scratch_shapes=[pltpu.VMEM((2, TOP_K, tm, D_MODEL), F32), pltpu.SemaphoreType.DMA((2,))],
        compiler_params=_cparams(("arbitrary",)),
        name="moe_combine_ln2",
    )(pos_blocks, pos_blocks, ys, wgt, x1, mods, ln_g, ln_b)


def _moe_plan(idx_t):
    flat = idx_t.reshape(-1)
    onehot = (flat[:, None] == jnp.arange(N_EXPERTS)[None, :]).astype(jnp.int32)
    rank = jnp.take_along_axis(jnp.cumsum(onehot, axis=0), flat[:, None], axis=1)[:, 0] - 1
    counts = jnp.sum(onehot, axis=0)
    tiles = (counts + MOE_TM - 1) // MOE_TM
    tile_end = jnp.cumsum(tiles)
    tile_start = tile_end - tiles
    pos = tile_start[flat] * MOE_TM + rank
    n_valid = tile_end[-1]
    per_tile = lambda n_chunks: (n_chunks + jnp.maximum(tiles, 1) - 1) // jnp.maximum(tiles, 1)
    i32 = lambda a: a.astype(jnp.int32)
    plan = lambda rate: (i32(tile_start), i32(tiles), i32(rate), i32(n_valid.reshape(1)))
    return (i32(pos.reshape(TOP_K, T_ALL)), plan(jnp.minimum(per_tile(GU_CHUNKS), GU_RING)),
            plan(per_tile(W_CHUNKS)))


def kernel(x_prompt, x_sample, cache_na_k, cache_na_v, state_ssd_fwd, state_ssd_bwd, c, c_ctx, w_mod, b_mod, w_in, ssd_conv_w, ssd_conv_b, ssd_dt_bias, ssd_a_log, ssd_d, ssd_norm_w, na_rpb, w_ssd_out, w_na_out, w_o, ln1_g, ln1_b, ln2_g, ln2_b, w_router, b_router, w_gate_up, b_gate_up, w_down, b_down):
    assert w_mod.shape[0] == 1, "single-layer trunk"
    x_ctx = x_prompt.reshape(T_CTX, D_MODEL)
    x_dec = x_sample.reshape(T_DEC, D_MODEL)

    cvec = jnp.concatenate([c_ctx[None], c, jnp.zeros((8 - 1 - DEC_BATCH, D_MODEL), F32)], axis=0)
    mods = _modulation(cvec, w_mod[0], b_mod[0])[:1 + DEC_BATCH].reshape(1 + DEC_BATCH, 6, D_MODEL)

    h = _ln_modulate(x_ctx, x_dec, mods)
    w = w_in[0]
    n_main = 3 * NA_WIDTH + SSD_WIDTH + SSD_CONV_CH
    w_bf = w.astype(BF16)
    w_dt = jnp.pad(w[:, n_main:n_main + 2 * SSD_HEADS], ((0, 0), (0, 128 - 2 * SSD_HEADS))).astype(BF16)
    w_gates = w_bf[:, n_main + 2 * SSD_HEADS:]
    tm = 1024
    ctx_blocks = T_CTX // tm
    q_all = _matmul(h, w_bf, rows=T_ALL, col_block0=0, n_out=NA_WIDTH, out_dtype=BF16,
                    name="proj_q")
    k_ctx = _matmul(h, w_bf, rows=T_CTX, col_block0=2, n_out=NA_WIDTH, name="proj_k_ctx")
    k_dec = _matmul(h, w_bf, rows=T_DEC, row_block0=ctx_blocks, col_block0=2, n_out=NA_WIDTH, out_dtype=BF16,
                    name="proj_k_dec")
    v_ctx = _matmul(h, w_bf, rows=T_CTX, col_block0=4, n_out=NA_WIDTH, name="proj_v_ctx")
    v_dec = _matmul(h, w_bf, rows=T_DEC, row_block0=ctx_blocks, col_block0=4, n_out=NA_WIDTH, out_dtype=BF16,
                    name="proj_v_dec")
    zx = _matmul(h, w_bf, rows=T_ALL, col_block0=6, n_out=SSD_WIDTH + SSD_CONV_CH, name="proj_zxbc")
    gates = _matmul(h, w_gates, rows=T_ALL, name="proj_gates")
    dt_all = _matmul(h, w_dt, rows=T_ALL, tn=128, name="proj_dt")

    na_ctx = _context_attention(q_all, k_ctx, v_ctx)
    kc = cache_na_k[:, 0].reshape(DEC_BATCH, PAST_LEN, NA_WIDTH)
    vc = cache_na_v[:, 0].reshape(DEC_BATCH, PAST_LEN, NA_WIDTH)
    na_dec = _neighbourhood_attention(q_all, k_dec, v_dec, kc, vc, _rpb_table(na_rpb[0]))

    consts = _ssd_consts(ssd_conv_w[0], ssd_conv_b[0], ssd_dt_bias[0], ssd_a_log[0], ssd_d[0])
    dt_t = dt_all[:, :2 * SSD_HEADS].reshape(T_ALL // SSD_CHUNK, SSD_CHUNK, 2, SSD_GROUPS, HEADS_PER_GROUP)
    dt_t = jnp.transpose(dt_t, (0, 2, 3, 4, 1))
    u_ctx, h_f, h_b = _ssd_mixer(zx, dt_all, dt_t, consts, length=SEQ, n_seq=BATCH, seq_block0=0,
                                 emit_state=True)
    h0 = (state_ssd_fwd[:, 0].reshape(DEC_BATCH, SSD_GROUPS, HEADS_PER_GROUP, SSD_HEAD_DIM, SSD_STATE),
          state_ssd_bwd[:, 0].reshape(DEC_BATCH, SSD_GROUPS, HEADS_PER_GROUP, SSD_HEAD_DIM, SSD_STATE))
    (u_dec,) = _ssd_mixer(zx, dt_all, dt_t, consts, length=DEC_SEQ, n_seq=DEC_BATCH,
                          seq_block0=T_CTX // DEC_SEQ, h0=h0)

    y = _merge_branches(na_ctx, na_dec, u_ctx, u_dec, ssd_norm_w[0].reshape(1, SSD_WIDTH),
                        w_na_out[0].astype(BF16), w_ssd_out[0].astype(BF16), gates)
    x1, h2, idx_t, wgt_t = _post_mix(y, w_o[0].astype(BF16), x_ctx, x_dec, mods,
                                     ln1_g[0].reshape(1, D_MODEL), ln1_b[0].reshape(1, D_MODEL),
                                     jnp.transpose(w_router[0]), b_router[0].reshape(N_EXPERTS, 1))

    pos, plan_gu, plan_down = _moe_plan(idx_t)
    tile_start, tiles, _, n_valid = plan_gu
    pos_d = jnp.transpose(pos.reshape(TOP_K, T_ALL // DISPATCH_TOK, DISPATCH_TOK), (1, 0, 2))
    xs = _dispatch(h2, pos_d, tile_start, tiles, n_valid)
    hid = _moe_gate_up(xs, plan_gu, w_gate_up[0], b_gate_up[0].reshape(N_EXPERTS, 1, 2 * D_EXPERT))
    ys = _moe_down(hid, plan_down, w_down[0], b_down[0].reshape(N_EXPERTS, 1, D_MODEL))
    pos_c = jnp.transpose(pos.reshape(TOP_K, T_ALL // COMBINE_TOK, COMBINE_TOK), (1, 0, 2))
    wgt = jnp.transpose(wgt_t)
    g2, b2 = ln2_g[0].reshape(1, D_MODEL), ln2_b[0].reshape(1, D_MODEL)
    y_ctx = _combine(ys, pos_c, wgt, x1, mods, g2, b2, rows=T_CTX, row0=0)
    y_dec = _combine(ys, pos_c, wgt, x1, mods, g2, b2, rows=T_DEC, row0=T_CTX)

    return (y_ctx.reshape(BATCH, SEQ, D_MODEL),
            y_dec.reshape(DEC_BATCH, DEC_SEQ, D_MODEL),
            k_ctx.reshape(BATCH, 1, SEQ, NA_HEADS, NA_HEAD_DIM),
            v_ctx.reshape(BATCH, 1, SEQ, NA_HEADS, NA_HEAD_DIM),
            h_f.reshape(BATCH, 1, SSD_HEADS, SSD_HEAD_DIM, SSD_STATE),
            h_b.reshape(BATCH, 1, SSD_HEADS, SSD_HEAD_DIM, SSD_STATE))
```

```python
import functools

import jax
import jax.numpy as jnp
from jax import lax
from jax.experimental import pallas as pl
from jax.experimental.pallas import tpu as pltpu

F32 = jnp.float32
BF16 = jnp.bfloat16

D_MODEL = 2048
BATCH = 32
SEQ = 256
DEC_BATCH = 2
DEC_SEQ = 1024
PAST_LEN = 512
GRID_W = 64
NA_HEADS = 16
NA_HEAD_DIM = 128
NA_WIDTH = NA_HEADS * NA_HEAD_DIM
NA_WIN_R = 8
NA_WIN_C = 16
SSD_HEADS = 32
SSD_HEAD_DIM = 64
SSD_WIDTH = SSD_HEADS * SSD_HEAD_DIM
SSD_GROUPS = 4
SSD_STATE = 128
SSD_CONV = 5
SSD_CHUNK = 128
SSD_CONV_CH = SSD_WIDTH + 2 * SSD_GROUPS * SSD_STATE
N_EXPERTS = 32
TOP_K = 4
D_EXPERT = 2048
SWIGLU_LIMIT = 7.0
SWIGLU_ALPHA = 1.702
DN_ALPHA = 2.0 ** 0.25
LN_EPS = 1e-5

T_CTX = BATCH * SEQ
T_DEC = DEC_BATCH * DEC_SEQ
T_ALL = T_CTX + T_DEC
HEADS_PER_GROUP = SSD_HEADS // SSD_GROUPS
GROUP_W = HEADS_PER_GROUP * SSD_HEAD_DIM
ROWS = DEC_SEQ // GRID_W
RPB_SLOTS = 16

MOE_TM = 256
MOE_TILES = T_ALL * TOP_K // MOE_TM + N_EXPERTS
MOE_ROWS = MOE_TILES * MOE_TM

VMEM_LIMIT = 56 * 1024 * 1024


def _cparams(sem):
    return pltpu.CompilerParams(dimension_semantics=sem, vmem_limit_bytes=VMEM_LIMIT)


def _split3(x):
    hi = x.astype(BF16)
    r1 = x - hi.astype(F32)
    mid = r1.astype(BF16)
    lo = (r1 - mid.astype(F32)).astype(BF16)
    return hi, mid, lo


def _dot(a, b):
    return jnp.dot(a, b, preferred_element_type=F32)


def _dot_nt(a, b):
    return lax.dot_general(a, b, (((1,), (1,)), ((), ())), preferred_element_type=F32)


def _dot_exact_rhs(x, m_bf):
    hi, mid, lo = _split3(x)
    return _dot(hi, m_bf) + (_dot(mid, m_bf) + _dot(lo, m_bf))


def _dot_hi_mid_rhs(x, m_bf):
    hi = x.astype(BF16)
    mid = (x - hi.astype(F32)).astype(BF16)
    return _dot(hi, m_bf) + _dot(mid, m_bf)


def _dot_exact_lhs(m_bf, x):
    hi, mid, lo = _split3(x)
    return _dot(m_bf, hi) + (_dot(m_bf, mid) + _dot(m_bf, lo))


def _dot_nt_f32(a, b):
    ah, am, al = _split3(a)
    bh, bm, bl = _split3(b)
    small = _dot_nt(ah, bl) + _dot_nt(am, bm) + _dot_nt(al, bh)
    mid = _dot_nt(ah, bm) + _dot_nt(am, bh)
    return _dot_nt(ah, bh) + (mid + small)


def _dot_f32(a, b):
    ah, am, al = _split3(a)
    bh, bm, bl = _split3(b)
    small = _dot(ah, bl) + _dot(am, bm) + _dot(al, bh)
    mid = _dot(ah, bm) + _dot(am, bh)
    return _dot(ah, bh) + (mid + small)


def _silu(x):
    return x * jax.nn.sigmoid(x)


def _softplus(x):
    return jnp.maximum(x, 0.0) + jnp.log1p(jnp.exp(-jnp.abs(x)))


def _layer_norm(x):
    mu = jnp.mean(x, axis=-1, keepdims=True)
    xc = x - mu
    var = jnp.mean(xc * xc, axis=-1, keepdims=True)
    return xc * lax.rsqrt(var + LN_EPS)


def _mod_row(i, tm):
    n_ctx = T_CTX // tm
    per_b = DEC_SEQ // tm
    return jnp.where(i < n_ctx, 0, 1 + (i - n_ctx) // per_b)


def _mod_kernel(c_ref, w_ref, b_ref, o_ref):
    o_ref[...] = _dot_f32(_silu(c_ref[...]), w_ref[...]) + b_ref[...]


def _modulation(cvec, w_mod, b_mod):
    tn = 1024
    n = w_mod.shape[1]
    return pl.pallas_call(
        _mod_kernel,
        grid=(n // tn,),
        in_specs=[pl.BlockSpec((8, D_MODEL), lambda j: (0, 0)),
                  pl.BlockSpec((D_MODEL, tn), lambda j: (0, j)),
                  pl.BlockSpec((1, tn), lambda j: (0, j))],
        out_specs=pl.BlockSpec((8, tn), lambda j: (0, j)),
        out_shape=jax.ShapeDtypeStruct((8, n), F32),
        compiler_params=_cparams(("arbitrary",)),
        name="modulation",
    )(cvec, w_mod, b_mod.reshape(1, n))


def _ln_mod_kernel(xc_ref, xd_ref, m_ref, o_ref, *, n_ctx):
    i = pl.program_id(0)
    x = jnp.where(i < n_ctx, xc_ref[...], xd_ref[...])
    y = _layer_norm(x) * (1.0 + m_ref[0, 1:2, :]) + m_ref[0, 0:1, :]
    o_ref[...] = y.astype(o_ref.dtype)


def _ln_modulate(x_ctx, x_dec, mods):
    tm = 512
    n_ctx = T_CTX // tm
    return pl.pallas_call(
        functools.partial(_ln_mod_kernel, n_ctx=n_ctx),
        grid=(T_ALL // tm,),
        in_specs=[pl.BlockSpec((tm, D_MODEL), lambda i: (jnp.minimum(i, n_ctx - 1), 0)),
                  pl.BlockSpec((tm, D_MODEL), lambda i: (jnp.maximum(i - n_ctx, 0), 0)),
                  pl.BlockSpec((1, 6, D_MODEL), lambda i: (_mod_row(i, tm), 0, 0))],
        out_specs=pl.BlockSpec((tm, D_MODEL), lambda i: (i, 0)),
        out_shape=jax.ShapeDtypeStruct((T_ALL, D_MODEL), BF16),
        compiler_params=_cparams(("arbitrary",)),
        name="ln_modulate",
    )(x_ctx, x_dec, mods)


def _mm_kernel(x_ref, w_ref, o_ref):
    o_ref[...] = _dot(x_ref[...], w_ref[...]).astype(o_ref.dtype)


def _matmul(x, w, *, rows, row_block0=0, col_block0=0, n_out=None, tm=1024, tn=1024,
            out_dtype=F32, name="matmul"):
    k = x.shape[1]
    n_out = w.shape[1] if n_out is None else n_out
    return pl.pallas_call(
        _mm_kernel,
        grid=(rows // tm, n_out // tn),
        in_specs=[pl.BlockSpec((tm, k), lambda i, j: (i + row_block0, 0)),
                  pl.BlockSpec((k, tn), lambda i, j: (0, j + col_block0))],
        out_specs=pl.BlockSpec((tm, tn), lambda i, j: (i, j)),
        out_shape=jax.ShapeDtypeStruct((rows, n_out), out_dtype),
        compiler_params=_cparams(("arbitrary", "arbitrary")),
        name=name,
    )(x, w)


def _ctx_attn_kernel(q_ref, k_ref, v_ref, o_ref):
    scale = NA_HEAD_DIM ** -0.5
    for h in range(NA_HEADS):
        sl = slice(h * NA_HEAD_DIM, (h + 1) * NA_HEAD_DIM)
        q = q_ref[:, sl].astype(BF16)
        k = k_ref[:, sl].astype(BF16)
        v = v_ref[:, sl].astype(BF16)
        s = _dot_nt(q, k) * scale
        m = jnp.max(s, axis=-1, keepdims=True)
        p = jnp.exp(s - m)
        l = jnp.sum(p, axis=-1, keepdims=True)
        o_ref[:, sl] = (_dot(p.astype(BF16), v) / l).astype(o_ref.dtype)


def _context_attention(q_all, k_ctx, v_ctx):
    spec = pl.BlockSpec((SEQ, NA_WIDTH), lambda b: (b, 0))
    return pl.pallas_call(
        _ctx_attn_kernel,
        grid=(BATCH,),
        in_specs=[spec, spec, spec],
        out_specs=spec,
        out_shape=jax.ShapeDtypeStruct((T_CTX, NA_WIDTH), BF16),
        compiler_params=_cparams(("arbitrary",)),
        name="context_attention",
    )(q_all, k_ctx, v_ctx)


def _nbr_attn_kernel(q_ref, k_ref, v_ref, kc_ref, vc_ref, rc_ref, o_ref):
    scale = NA_HEAD_DIM ** -0.5
    kr = min(NA_WIN_R, ROWS)
    kc = kc_ref[0].astype(BF16)
    vc = vc_ref[0].astype(BF16)
    band_start = lambda r: min(max(r - kr // 2, 0), ROWS - kr)
    groups = {}
    for r in range(ROWS):
        groups.setdefault(band_start(r), []).append(r)
    for r0, rows in groups.items():
        lo, hi = rows[0] * GRID_W, (rows[-1] + 1) * GRID_W
        q = q_ref[lo:hi, :].astype(BF16)
        kb = k_ref[r0 * GRID_W:(r0 + kr) * GRID_W, :].astype(BF16)
        vb = v_ref[r0 * GRID_W:(r0 + kr) * GRID_W, :].astype(BF16)
        offs = [(r0 - r + NA_WIN_R - 1) * GRID_W for r in rows]
        bias = jnp.concatenate([rc_ref[0, :, off:off + kr * GRID_W] for off in offs], axis=0)
        s_loc = _dot_nt(q, kb) * scale + bias
        s_ctx = _dot_nt(q, kc) * scale
        m = jnp.maximum(jnp.max(s_loc, axis=-1, keepdims=True),
                        jnp.max(s_ctx, axis=-1, keepdims=True))
        p_loc = jnp.exp(s_loc - m)
        p_ctx = jnp.exp(s_ctx - m)
        l = jnp.sum(p_loc, axis=-1, keepdims=True) + jnp.sum(p_ctx, axis=-1, keepdims=True)
        o = _dot(p_loc.astype(BF16), vb) + _dot(p_ctx.astype(BF16), vc)
        o_ref[lo:hi, :] = (o / l).astype(o_ref.dtype)


def _rpb_table(rpb):
    col = jnp.arange(GRID_W)
    c0 = jnp.clip(col - NA_WIN_C // 2, 0, GRID_W - NA_WIN_C)
    col_mask = (col[None, :] >= c0[:, None]) & (col[None, :] < c0[:, None] + NA_WIN_C)
    dc_idx = jnp.clip(col[None, :] - col[:, None] + NA_WIN_C - 1, 0, 2 * NA_WIN_C - 2)
    pick = (dc_idx[:, :, None] == jnp.arange(2 * NA_WIN_C - 1)[None, None, :]).astype(F32)
    t = jnp.einsum('hdj,qkj->hqdk', rpb, pick, precision=lax.Precision.HIGHEST)
    t = jnp.where(col_mask[None, :, None, :], t, -jnp.inf)
    t = jnp.pad(t, ((0, 0), (0, 0), (0, RPB_SLOTS - t.shape[2]), (0, 0)))
    return t.reshape(NA_HEADS, GRID_W, RPB_SLOTS * GRID_W)


def _neighbourhood_attention(q_all, k_dec, v_dec, kc, vc, rc):
    q_row0 = T_CTX // DEC_SEQ
    hd = NA_HEAD_DIM
    return pl.pallas_call(
        _nbr_attn_kernel,
        grid=(DEC_BATCH, NA_HEADS),
        in_specs=[pl.BlockSpec((DEC_SEQ, hd), lambda b, h: (b + q_row0, h)),
                  pl.BlockSpec((DEC_SEQ, hd), lambda b, h: (b, h)),
                  pl.BlockSpec((DEC_SEQ, hd), lambda b, h: (b, h)),
                  pl.BlockSpec((1, PAST_LEN, hd), lambda b, h: (b, 0, h)),
                  pl.BlockSpec((1, PAST_LEN, hd), lambda b, h: (b, 0, h)),
                  pl.BlockSpec((1, GRID_W, RPB_SLOTS * GRID_W), lambda b, h: (h, 0, 0))],
        out_specs=pl.BlockSpec((DEC_SEQ, hd), lambda b, h: (b, h)),
        out_shape=jax.ShapeDtypeStruct((T_DEC, NA_WIDTH), BF16),
        compiler_params=_cparams(("arbitrary", "arbitrary")),
        name="neighbourhood_attention",
    )(q_all, k_dec, v_dec, kc, vc, rc)


CONV_HALO = 8


def _conv_silu(u_ref, w_ref, b_ref, pad_s, length):
    width = u_ref.shape[1]
    halo = jnp.zeros((CONV_HALO, width), F32)
    pad_s[0:CONV_HALO, 0:width] = halo
    pad_s[CONV_HALO + length:2 * CONV_HALO + length, 0:width] = halo
    pad_s[CONV_HALO:CONV_HALO + length, 0:width] = u_ref[...]
    acc = b_ref[...]
    for k in range(SSD_CONV):
        d = k - SSD_CONV // 2
        acc = acc + pad_s[CONV_HALO + d:CONV_HALO + d + length, 0:width] * w_ref[k:k + 1, :]
    return _silu(acc)


def _ssd_kernel(*refs, length, has_h0, emit_state):
    (xs_ref, b_ref, c_ref, z_ref, dt_ref, dtt_ref, cwx_ref, cwb_ref, cwc_ref,
     cbx_ref, cbb_ref, cbc_ref, dtb_row_ref, dtb_col_ref, al_row_ref, al_col_ref,
     e_ref, d_ref) = refs[:18]
    pos = 18
    if has_h0:
        h0f_ref, h0b_ref = refs[pos:pos + 2]
        pos += 2
    u_ref = refs[pos]
    pos += 1
    if emit_state:
        hf_ref, hb_ref = refs[pos:pos + 2]
        pos += 2
    xs_s, b_s, c_s, y_s, st_s, pad_s = refs[pos:]

    q = SSD_CHUNK
    nc = length // q
    xs_s[...] = _conv_silu(xs_ref, cwx_ref, cbx_ref, pad_s, length)
    b_s[...] = _conv_silu(b_ref, cwb_ref, cbb_ref, pad_s, length)
    c_s[...] = _conv_silu(c_ref, cwc_ref, cbc_ref, pad_s, length)
    y_s[...] = d_ref[0] * xs_s[...]

    for dirn in range(2):
        if has_h0:
            h0 = (h0f_ref if dirn == 0 else h0b_ref)[0, 0]
            st_s[dirn] = jnp.transpose(h0.reshape(GROUP_W, SSD_STATE))
        else:
            st_s[dirn] = jnp.zeros((SSD_STATE, GROUP_W), F32)

    ri = lax.broadcasted_iota(jnp.int32, (q, q), 0)
    ci = lax.broadcasted_iota(jnp.int32, (q, q), 1)
    lower = ri >= ci
    upper = ri <= ci
    lower_bf = jnp.where(lower, 1.0, 0.0).astype(BF16)
    upper_bf = jnp.where(upper, 1.0, 0.0).astype(BF16)

    shared = {}

    def chunk_common(c, r0):
        if isinstance(c, int) and c in shared:
            return shared[c]
        xs_c = xs_s[pl.ds(r0, q), :]
        bc = b_s[pl.ds(r0, q), :]
        cc_bf = c_s[pl.ds(r0, q), :].astype(BF16)
        dtp = _softplus(dt_ref[pl.ds(r0, q), :] + dtb_row_ref[...])
        da = dtp * (-jnp.exp(al_row_ref[...]))
        pre = _dot_exact_lhs(lower_bf, da)
        cb = _dot_nt(cc_bf, bc.astype(BF16))
        bt = jnp.transpose(bc).astype(BF16)
        vals = (xs_c, cc_bf, dtp, da, pre, cb, bt)
        if isinstance(c, int):
            shared[c] = vals
        return vals

    def chunk(c, dirn):
        r0 = c * q if isinstance(c, int) else pl.multiple_of(c * q, q)
        tri = lower if dirn == 0 else upper
        xs_c, cc_bf, dtp, da, pre, cb, bt = chunk_common(c, r0)
        cs = pre if dirn == 0 else pre[q - 1:q, :] - pre + da
        e_bf = e_ref[dirn, 0]
        dt_e = _dot_hi_mid_rhs(dtp, e_bf)
        cs_e = _dot_exact_rhs(cs, e_bf)
        dt_t = _softplus(dtt_ref[c, dirn, 0] + dtb_col_ref[dirn, 0])
        da_t = dt_t * (-jnp.exp(al_col_ref[dirn, 0]))
        cs_t = _dot_exact_rhs(da_t, upper_bf if dirn == 0 else lower_bf)

        end = q - 1 if dirn == 0 else 0
        cs_end = cs_e[end:end + 1, :]
        xdt = xs_c * dt_e
        st = st_s[dirn]
        y = _dot(cc_bf, st.astype(BF16)) * jnp.exp(cs_e)
        st_s[dirn] = jnp.exp(cs_end) * st + _dot(bt, (xdt * jnp.exp(cs_end - cs_e)).astype(BF16))
        xdt_bf = xdt.astype(BF16)
        parts = []
        for r in range(HEADS_PER_GROUP):
            sl = slice(r * SSD_HEAD_DIM, (r + 1) * SSD_HEAD_DIM)
            diff = cs_e[:, r * SSD_HEAD_DIM:r * SSD_HEAD_DIM + 1] - cs_t[r:r + 1, :]
            lm = jnp.exp(jnp.where(tri, diff, -jnp.inf))
            parts.append(_dot((cb * lm).astype(BF16), xdt_bf[:, sl]))
        y = y + jnp.concatenate(parts, axis=-1)
        y_s[pl.ds(r0, q), :] = y_s[pl.ds(r0, q), :] + y

    if nc <= 2:
        for i in range(nc):
            chunk(i, 0)
            chunk(nc - 1 - i, 1)
    else:
        def both(i, carry):
            chunk(i, 0)
            chunk(nc - 1 - i, 1)
            return carry
        lax.fori_loop(0, nc, both, 0)

    u_ref[...] = y_s[...] * _silu(z_ref[...])
    if emit_state:
        hf_ref[0, 0] = jnp.transpose(st_s[0]).reshape(HEADS_PER_GROUP, SSD_HEAD_DIM, SSD_STATE)
        hb_ref[0, 0] = jnp.transpose(st_s[1]).reshape(HEADS_PER_GROUP, SSD_HEAD_DIM, SSD_STATE)


def _ssd_mixer(zx, dt_all, dt_t, consts, *, length, n_seq, seq_block0, h0=None, emit_state=False):
    (cw, cb, dtb_row, dtb_col, al_row, al_col, expand, d_row) = consts
    g_w, n_s = GROUP_W, SSD_STATE
    xs_cb0 = D_MODEL // g_w
    b_cb0 = (D_MODEL + SSD_WIDTH) // n_s
    c_cb0 = b_cb0 + SSD_GROUPS
    cw_b0 = SSD_WIDTH // n_s
    nck = length // SSD_CHUNK
    in_specs = [
        pl.BlockSpec((length, g_w), lambda s, g: (s + seq_block0, xs_cb0 + g)),
        pl.BlockSpec((length, n_s), lambda s, g: (s + seq_block0, b_cb0 + g)),
        pl.BlockSpec((length, n_s), lambda s, g: (s + seq_block0, c_cb0 + g)),
        pl.BlockSpec((length, g_w), lambda s, g: (s + seq_block0, g)),
        pl.BlockSpec((length, 128), lambda s, g: (s + seq_block0, 0)),
        pl.BlockSpec((nck, 2, 1, HEADS_PER_GROUP, SSD_CHUNK), lambda s, g: (s + seq_block0, 0, g, 0, 0)),
        pl.BlockSpec((SSD_CONV, g_w), lambda s, g: (0, g)),
        pl.BlockSpec((SSD_CONV, n_s), lambda s, g: (0, cw_b0 + g)),
        pl.BlockSpec((SSD_CONV, n_s), lambda s, g: (0, cw_b0 + SSD_GROUPS + g)),
        pl.BlockSpec((1, g_w), lambda s, g: (0, g)),
        pl.BlockSpec((1, n_s), lambda s, g: (0, cw_b0 + g)),
        pl.BlockSpec((1, n_s), lambda s, g: (0, cw_b0 + SSD_GROUPS + g)),
        pl.BlockSpec((1, 128), lambda s, g: (0, 0)),
        pl.BlockSpec((2, 1, HEADS_PER_GROUP, SSD_CHUNK), lambda s, g: (0, g, 0, 0)),
        pl.BlockSpec((1, 128), lambda s, g: (0, 0)),
        pl.BlockSpec((2, 1, HEADS_PER_GROUP, SSD_CHUNK), lambda s, g: (0, g, 0, 0)),
        pl.BlockSpec((2, 1, 128, g_w), lambda s, g: (0, g, 0, 0)),
        pl.BlockSpec((1, 1, g_w), lambda s, g: (g, 0, 0)),
    ]
    args = [zx, zx, zx, zx, dt_all, dt_t, cw, cw, cw, cb, cb, cb,
            dtb_row, dtb_col, al_row, al_col, expand, d_row]
    st_spec = pl.BlockSpec((1, 1, HEADS_PER_GROUP, SSD_HEAD_DIM, n_s), lambda s, g: (s, g, 0, 0, 0))
    if h0 is not None:
        in_specs += [st_spec, st_spec]
        args += list(h0)
    out_specs = [pl.BlockSpec((length, g_w), lambda s, g: (s, g))]
    out_shape = [jax.ShapeDtypeStruct((n_seq * length, SSD_WIDTH), F32)]
    if emit_state:
        st_shape = jax.ShapeDtypeStruct((n_seq, SSD_GROUPS, HEADS_PER_GROUP, SSD_HEAD_DIM, n_s), F32)
        out_specs += [st_spec, st_spec]
        out_shape += [st_shape, st_shape]
    return pl.pallas_call(
        functools.partial(_ssd_kernel, length=length, has_h0=h0 is not None, emit_state=emit_state),
        grid=(n_seq, SSD_GROUPS),
        in_specs=in_specs,
        out_specs=out_specs,
        out_shape=out_shape,
        scratch_shapes=[pltpu.VMEM((length, g_w), F32), pltpu.VMEM((length, n_s), F32),
                        pltpu.VMEM((length, n_s), F32), pltpu.VMEM((length, g_w), F32),
                        pltpu.VMEM((2, n_s, g_w), F32),
                        pltpu.VMEM((length + 2 * CONV_HALO, g_w), F32)],
        compiler_params=_cparams(("arbitrary", "arbitrary")),
        name="ssd_mixer_%d" % length,
    )(*args)


def _ssd_consts(conv_w, conv_b, dt_bias, a_log, d_skip):
    hpg = HEADS_PER_GROUP
    dtb_row = jnp.pad(dt_bias.reshape(1, 2 * SSD_HEADS), ((0, 0), (0, 128 - 2 * SSD_HEADS)))
    col = lambda p: jnp.broadcast_to(p.reshape(2, SSD_GROUPS, hpg, 1), (2, SSD_GROUPS, hpg, SSD_CHUNK))
    al_row = jnp.pad(a_log.reshape(1, 2 * SSD_HEADS), ((0, 0), (0, 128 - 2 * SSD_HEADS)))
    src = (jnp.arange(2)[:, None, None] * SSD_HEADS + jnp.arange(SSD_GROUPS)[None, :, None] * hpg
           + jnp.arange(GROUP_W)[None, None, :] // SSD_HEAD_DIM)
    expand = (jnp.arange(128)[None, None, :, None] == src[:, :, None, :]).astype(BF16)
    d_row = jnp.repeat(d_skip.reshape(SSD_GROUPS, 1, hpg), SSD_HEAD_DIM, axis=-1)
    return (conv_w, conv_b.reshape(1, SSD_CONV_CH), dtb_row, col(dt_bias), al_row, col(a_log), expand, d_row)


def _merge_kernel(nac_ref, nad_ref, uc_ref, ud_ref, nw_ref, w1_ref, w2_ref, g_na_ref, g_ssd_ref,
                  o_ref, *, n_ctx):
    is_ctx = pl.program_id(0) < n_ctx
    a1 = jnp.where(is_ctx, nac_ref[...], nad_ref[...]).astype(BF16)
    u = jnp.where(is_ctx, uc_ref[...], ud_ref[...])
    r = lax.rsqrt(jnp.mean(u * u, axis=-1, keepdims=True) + LN_EPS)
    a2 = (u * r * nw_ref[...]).astype(BF16)
    o = (jax.nn.sigmoid(g_na_ref[...]) * _dot(a1, w1_ref[...])
         + jax.nn.sigmoid(g_ssd_ref[...]) * _dot(a2, w2_ref[...]))
    o_ref[...] = o.astype(o_ref.dtype)


def _merge_branches(na_ctx, na_dec, u_ctx, u_dec, norm_w, w_na_out, w_ssd_out, gates):
    tm = 256
    n_ctx = T_CTX // tm
    ctx_rows = lambda i: (jnp.minimum(i, n_ctx - 1), 0)
    dec_rows = lambda i: (jnp.maximum(i - n_ctx, 0), 0)
    const = lambda i: (0, 0)
    resident = pl.Buffered(1)
    return pl.pallas_call(
        functools.partial(_merge_kernel, n_ctx=n_ctx),
        grid=(T_ALL // tm,),
        in_specs=[pl.BlockSpec((tm, D_MODEL), ctx_rows), pl.BlockSpec((tm, D_MODEL), dec_rows),
                  pl.BlockSpec((tm, D_MODEL), ctx_rows), pl.BlockSpec((tm, D_MODEL), dec_rows),
                  pl.BlockSpec((1, D_MODEL), const),
                  pl.BlockSpec((D_MODEL, D_MODEL), const, pipeline_mode=resident),
                  pl.BlockSpec((D_MODEL, D_MODEL), const, pipeline_mode=resident),
                  pl.BlockSpec((tm, D_MODEL), lambda i: (i, 1)),
                  pl.BlockSpec((tm, D_MODEL), lambda i: (i, 0))],
        out_specs=pl.BlockSpec((tm, D_MODEL), lambda i: (i, 0)),
        out_shape=jax.ShapeDtypeStruct((T_ALL, D_MODEL), BF16),
        compiler_params=_cparams(("arbitrary",)),
        name="merge_branches",
    )(na_ctx, na_dec, u_ctx, u_dec, norm_w, w_na_out, w_ssd_out, gates, gates)


def _post_mix_kernel(y_ref, wo_ref, xc_ref, xd_ref, m_ref, g_ref, b_ref, wr_ref, br_ref,
                     x1_ref, h2_ref, idx_ref, wgt_ref, *, n_ctx):
    i = pl.program_id(0)
    x = jnp.where(i < n_ctx, xc_ref[...], xd_ref[...])
    mix = _dot(y_ref[...], wo_ref[...])
    x1 = _layer_norm(DN_ALPHA * x + m_ref[0, 2:3, :] * mix) * g_ref[...] + b_ref[...]
    x1_ref[...] = x1
    h2 = _layer_norm(x1) * (1.0 + m_ref[0, 4:5, :]) + m_ref[0, 3:4, :]
    h2_ref[...] = h2
    logits = _dot_nt_f32(wr_ref[...], h2) + br_ref[...]
    eidx = lax.broadcasted_iota(jnp.int32, logits.shape, 0)
    vals, idxs = [], []
    for _ in range(TOP_K):
        m = jnp.max(logits, axis=0, keepdims=True)
        sel = jnp.min(jnp.where(logits == m, eidx, N_EXPERTS), axis=0, keepdims=True)
        logits = jnp.where(eidx == sel, -jnp.inf, logits)
        vals.append(m)
        idxs.append(sel)
    ex = [jnp.exp(v - vals[0]) for v in vals]
    tot = ex[0] + ex[1] + ex[2] + ex[3]
    idx_ref[...] = jnp.concatenate(idxs, axis=0)
    wgt_ref[...] = jnp.concatenate([e / tot for e in ex], axis=0)


def _post_mix(y, w_o, x_ctx, x_dec, mods, ln_g, ln_b, w_router_t, b_router):
    tm = 512
    n_ctx = T_CTX // tm
    row = lambda i: (i, 0)
    const = lambda i: (0, 0)
    return pl.pallas_call(
        functools.partial(_post_mix_kernel, n_ctx=n_ctx),
        grid=(T_ALL // tm,),
        in_specs=[pl.BlockSpec((tm, D_MODEL), row),
                  pl.BlockSpec((D_MODEL, D_MODEL), const, pipeline_mode=pl.Buffered(1)),
                  pl.BlockSpec((tm, D_MODEL), lambda i: (jnp.minimum(i, n_ctx - 1), 0)),
                  pl.BlockSpec((tm, D_MODEL), lambda i: (jnp.maximum(i - n_ctx, 0), 0)),
                  pl.BlockSpec((1, 6, D_MODEL), lambda i: (_mod_row(i, tm), 0, 0)),
                  pl.BlockSpec((1, D_MODEL), const), pl.BlockSpec((1, D_MODEL), const),
                  pl.BlockSpec((N_EXPERTS, D_MODEL), const), pl.BlockSpec((N_EXPERTS, 1), const)],
        out_specs=[pl.BlockSpec((tm, D_MODEL), row), pl.BlockSpec((tm, D_MODEL), row),
                   pl.BlockSpec((TOP_K, tm), lambda i: (0, i)), pl.BlockSpec((TOP_K, tm), lambda i: (0, i))],
        out_shape=[jax.ShapeDtypeStruct((T_ALL, D_MODEL), F32), jax.ShapeDtypeStruct((T_ALL, D_MODEL), F32),
                   jax.ShapeDtypeStruct((TOP_K, T_ALL), jnp.int32), jax.ShapeDtypeStruct((TOP_K, T_ALL), F32)],
        compiler_params=_cparams(("arbitrary",)),
        name="post_mix_router",
    )(y, w_o, x_ctx, x_dec, mods, ln_g, ln_b, w_router_t, b_router)


DISPATCH_TOK = 128


def _tile_rows(t):
    return pl.ds(pl.multiple_of(t * MOE_TM, MOE_TM), MOE_TM)


def _dispatch_kernel(t0_ref, nt_ref, nv_ref, pos_ref, x_hbm, xs_hbm, buf, zero_s, sem_in, sem_out, sem_z,
                     *, n_steps):
    i = pl.program_id(0)
    slot = i % 3

    def zero_copy(tile):
        return pltpu.make_async_copy(zero_s, xs_hbm.at[_tile_rows(tile)], sem_z)

    def in_copy(step, s):
        rows = pl.ds(pl.multiple_of(step * DISPATCH_TOK, DISPATCH_TOK), DISPATCH_TOK)
        return pltpu.make_async_copy(x_hbm.at[rows], buf.at[s], sem_in.at[s])

    def row_copy(s, t, dst):
        return pltpu.make_async_copy(buf.at[s, pl.ds(t, 1)], xs_hbm.at[pl.ds(dst, 1)], sem_out.at[s])

    def wait_rows(s):
        for _ in range(TOP_K * DISPATCH_TOK):
            row_copy(s, 0, 0).wait()

    @pl.when(i == 0)
    def _():
        in_copy(0, 0).start()
        zero_s[...] = jnp.zeros(zero_s.shape, zero_s.dtype)
        for e in range(N_EXPERTS):
            @pl.when(nt_ref[e] > 0)
            def _():
                zero_copy(t0_ref[e] + nt_ref[e] - 1).start()

        def tail_start(t, c):
            zero_copy(t).start()
            return c
        lax.fori_loop(nv_ref[0], MOE_TILES, tail_start, 0)
        for e in range(N_EXPERTS):
            @pl.when(nt_ref[e] > 0)
            def _():
                zero_copy(0).wait()

        def tail_wait(t, c):
            zero_copy(0).wait()
            return c
        lax.fori_loop(nv_ref[0], MOE_TILES, tail_wait, 0)

    in_copy(i, slot).wait()

    @pl.when(i + 1 < n_steps)
    def _():
        in_copy(i + 1, (i + 1) % 3).start()

    for k in range(TOP_K):
        for t in range(DISPATCH_TOK):
            row_copy(slot, t, pos_ref[0, k, t]).start(priority=t % 2)

    @pl.when(i > 0)
    def _():
        wait_rows((i + 2) % 3)

    @pl.when(i == n_steps - 1)
    def _():
        wait_rows(slot)


def _dispatch(h2, pos_blocks, tile_start, tiles, n_valid):
    n_steps = T_ALL // DISPATCH_TOK
    grid_spec = pltpu.PrefetchScalarGridSpec(
        num_scalar_prefetch=3,
        grid=(n_steps,),
        in_specs=[pl.BlockSpec((1, TOP_K, DISPATCH_TOK), lambda i, *_: (i, 0, 0), memory_space=pltpu.SMEM),
                  pl.BlockSpec(memory_space=pl.ANY)],
        out_specs=pl.BlockSpec(memory_space=pl.ANY),
        scratch_shapes=[pltpu.VMEM((3, DISPATCH_TOK, D_MODEL), F32),
                        pltpu.VMEM((MOE_TM, D_MODEL), F32),
                        pltpu.SemaphoreType.DMA((3,)), pltpu.SemaphoreType.DMA((3,)),
                        pltpu.SemaphoreType.DMA(())],
    )
    return pl.pallas_call(
        functools.partial(_dispatch_kernel, n_steps=n_steps),
        grid_spec=grid_spec,
        out_shape=jax.ShapeDtypeStruct((MOE_ROWS, D_MODEL), F32),
        compiler_params=_cparams(("arbitrary",)),
        name="moe_dispatch",
    )(tile_start, tiles, n_valid, pos_blocks, h2)


def _expert_tile_loop(t0, nt, in_copy, out_copy, compute, after_tile):
    @pl.when(nt > 0)
    def _():
        in_copy(t0, 0).start()

    def body(i, carry):
        slot = i % 2
        in_copy(t0 + i, slot).wait()

        @pl.when(i + 1 < nt)
        def _():
            in_copy(t0 + i + 1, 1 - slot).start()

        after_tile(i)

        @pl.when(i >= 2)
        def _():
            out_copy(t0 + i - 2, slot).wait()

        compute(slot)
        out_copy(t0 + i, slot).start()
        return carry

    lax.fori_loop(0, nt, body, 0)

    @pl.when(nt >= 2)
    def _():
        out_copy(t0, nt % 2).wait()

    @pl.when(nt >= 1)
    def _():
        out_copy(t0, (nt + 1) % 2).wait()


def _zero_tail_tiles(is_last, n_valid, obuf, out_copy):
    @pl.when(is_last)
    def _():
        obuf[0] = jnp.zeros(obuf.shape[1:], obuf.dtype)

        def body(t, carry):
            out_copy(t, 0).start()
            out_copy(t, 0).wait()
            return carry
        lax.fori_loop(n_valid, MOE_TILES, body, 0)


W_CHUNKS = 16


def _stream_weights(step, n_steps, chunks_per_tile, chunk_copy, stage, w_s, tile_loop):
    def start_chunks(target_step, lo, hi):
        def body(c, carry):
            chunk_copy(target_step, c).start()
            return carry
        lax.fori_loop(lo, hi, body, 0)

    @pl.when(step == 0)
    def _():
        start_chunks(0, 0, W_CHUNKS)

    for _ in range(W_CHUNKS):
        chunk_copy(0, 0).wait()
    w_s[...] = stage[...].astype(BF16)

    has_next = step + 1 < n_steps

    def after_tile(i):
        @pl.when(has_next)
        def _():
            start_chunks(step + 1, jnp.minimum(i * chunks_per_tile, W_CHUNKS),
                         jnp.minimum((i + 1) * chunks_per_tile, W_CHUNKS))

    n_tiles = tile_loop(after_tile)

    @pl.when(has_next)
    def _():
        start_chunks(step + 1, jnp.minimum(n_tiles * chunks_per_tile, W_CHUNKS), W_CHUNKS)


GU_CHUNKS = 32
GU_RING = 8


def _gate_up_kernel(t0_ref, nt_ref, cpt_ref, nv_ref, xs_hbm, w_hbm, bg_ref, bu_ref, hid_hbm,
                    ring, w_s, xbuf, obuf, sem_w, sem_in, sem_out):
    e = pl.program_id(0)
    cur = e % 2
    nxt = 1 - cur
    per_mat = GU_CHUNKS // 2
    rows = D_MODEL // per_mat

    def chunk_copy(expert, c):
        r = pl.multiple_of((c % per_mat) * rows, rows)
        col = pl.multiple_of((c // per_mat) * D_EXPERT, D_EXPERT)
        return pltpu.make_async_copy(w_hbm.at[expert, pl.ds(r, rows), pl.ds(col, D_EXPERT)],
                                     ring.at[c % GU_RING], sem_w.at[c % GU_RING])

    def issue(expert, lo, hi):
        def body(c, carry):
            chunk_copy(expert, c).start()
            return carry
        lax.fori_loop(lo, hi, body, 0)

    def land(buf, lo, hi):
        def body(c, carry):
            chunk_copy(0, c).wait()
            r = pl.multiple_of((c % per_mat) * rows, rows)
            w_s[buf, c // per_mat, pl.ds(r, rows), :] = ring[c % GU_RING].astype(BF16)
            return carry
        lax.fori_loop(lo, hi, body, 0)

    def fetch_exposed(expert, buf, lo):
        for rr in range(GU_CHUNKS // GU_RING):
            a = jnp.maximum(lo, rr * GU_RING)
            issue(expert, a, (rr + 1) * GU_RING)
            land(buf, a, (rr + 1) * GU_RING)

    @pl.when(e == 0)
    def _():
        fetch_exposed(0, 0, 0)

    has_next = e + 1 < N_EXPERTS
    k = cpt_ref[e]
    nt = nt_ref[e]
    clip = lambda v: jnp.clip(v, 0, GU_CHUNKS)

    def after_tile(i):
        @pl.when(has_next)
        def _():
            land(nxt, clip((i - 1) * k), clip(i * k))
            issue(e + 1, clip(i * k), clip((i + 1) * k))

    def in_copy(t, slot):
        return pltpu.make_async_copy(xs_hbm.at[_tile_rows(t)], xbuf.at[slot], sem_in.at[slot])

    def out_copy(t, slot):
        return pltpu.make_async_copy(obuf.at[slot], hid_hbm.at[_tile_rows(t)], sem_out.at[slot])

    def compute(slot):
        x = xbuf[slot].astype(BF16)
        gate = jnp.minimum(_dot(x, w_s[cur, 0]) + bg_ref[0], SWIGLU_LIMIT)
        up = jnp.clip(_dot(x, w_s[cur, 1]) + bu_ref[0], -SWIGLU_LIMIT, SWIGLU_LIMIT)
        hid = (up + 1.0) * gate * jax.nn.sigmoid(SWIGLU_ALPHA * gate)
        obuf[slot] = hid.astype(obuf.dtype)

    _expert_tile_loop(t0_ref[e], nt, in_copy, out_copy, compute, after_tile)

    @pl.when(has_next)
    def _():
        land(nxt, clip((nt - 1) * k), clip(nt * k))
        fetch_exposed(e + 1, nxt, clip(nt * k))

    _zero_tail_tiles(e == N_EXPERTS - 1, nv_ref[0], obuf, out_copy)


def _moe_gate_up(xs, plan, w_gate_up, b_gate_up):
    grid_spec = pltpu.PrefetchScalarGridSpec(
        num_scalar_prefetch=4,
        grid=(N_EXPERTS,),
        in_specs=[pl.BlockSpec(memory_space=pl.ANY), pl.BlockSpec(memory_space=pl.ANY),
                  pl.BlockSpec((1, 1, D_EXPERT), lambda e, *_: (e, 0, 0)),
                  pl.BlockSpec((1, 1, D_EXPERT), lambda e, *_: (e, 0, 1))],
        out_specs=pl.BlockSpec(memory_space=pl.ANY),
        scratch_shapes=[pltpu.VMEM((GU_RING, D_MODEL // (GU_CHUNKS // 2), D_EXPERT), F32),
                        pltpu.VMEM((2, 2, D_MODEL, D_EXPERT), BF16),
                        pltpu.VMEM((2, MOE_TM, D_MODEL), F32), pltpu.VMEM((2, MOE_TM, D_EXPERT), BF16),
                        pltpu.SemaphoreType.DMA((GU_RING,)), pltpu.SemaphoreType.DMA((2,)),
                        pltpu.SemaphoreType.DMA((2,))],
    )
    return pl.pallas_call(
        _gate_up_kernel,
        grid_spec=grid_spec,
        out_shape=jax.ShapeDtypeStruct((MOE_ROWS, D_EXPERT), BF16),
        compiler_params=_cparams(("arbitrary",)),
        name="moe_gate_up",
    )(*plan, xs, w_gate_up, b_gate_up, b_gate_up)


def _down_kernel(t0_ref, nt_ref, cpt_ref, nv_ref, hid_hbm, w_hbm, b_ref, ys_hbm,
                 stage, w_s, hbuf, obuf, sem_w, sem_in, sem_out):
    e = pl.program_id(0)
    rows = D_EXPERT // W_CHUNKS

    def chunk_copy(step, c):
        r = pl.ds(pl.multiple_of(c * rows, rows), rows)
        return pltpu.make_async_copy(w_hbm.at[step, r], stage.at[r], sem_w)

    def in_copy(t, slot):
        return pltpu.make_async_copy(hid_hbm.at[_tile_rows(t)], hbuf.at[slot], sem_in.at[slot])

    def out_copy(t, slot):
        return pltpu.make_async_copy(obuf.at[slot], ys_hbm.at[_tile_rows(t)], sem_out.at[slot])

    def compute(slot):
        obuf[slot] = _dot(hbuf[slot], w_s[...]) + b_ref[0]

    def tile_loop(after_tile):
        _expert_tile_loop(t0_ref[e], nt_ref[e], in_copy, out_copy, compute, after_tile)
        return nt_ref[e]

    _stream_weights(e, N_EXPERTS, cpt_ref[e], chunk_copy, stage, w_s, tile_loop)
    _zero_tail_tiles(e == N_EXPERTS - 1, nv_ref[0], obuf, out_copy)


def _moe_down(hid, plan, w_down, b_down):
    grid_spec = pltpu.PrefetchScalarGridSpec(
        num_scalar_prefetch=4,
        grid=(N_EXPERTS,),
        in_specs=[pl.BlockSpec(memory_space=pl.ANY), pl.BlockSpec(memory_space=pl.ANY),
                  pl.BlockSpec((1, 1, D_MODEL), lambda e, *_: (e, 0, 0))],
        out_specs=pl.BlockSpec(memory_space=pl.ANY),
        scratch_shapes=[pltpu.VMEM((D_EXPERT, D_MODEL), F32), pltpu.VMEM((D_EXPERT, D_MODEL), BF16),
                        pltpu.VMEM((2, MOE_TM, D_EXPERT), BF16), pltpu.VMEM((2, MOE_TM, D_MODEL), F32),
                        pltpu.SemaphoreType.DMA(()), pltpu.SemaphoreType.DMA((2,)),
                        pltpu.SemaphoreType.DMA((2,))],
    )
    return pl.pallas_call(
        _down_kernel,
        grid_spec=grid_spec,
        out_shape=jax.ShapeDtypeStruct((MOE_ROWS, D_MODEL), F32),
        compiler_params=_cparams(("arbitrary",)),
        name="moe_down",
    )(*plan, hid, w_down, b_down)


COMBINE_TOK = 64


def _start_rows(src_hbm, idx_ref, idx_lead, dst, sem, count):
    for r in range(count):
        pltpu.make_async_copy(src_hbm.at[pl.ds(idx_ref[idx_lead + (r,)], 1)], dst.at[pl.ds(r, 1)],
                              sem).start(priority=r % 2)


def _wait_rows(src_hbm, dst, sem, count):
    for _ in range(count):
        pltpu.make_async_copy(src_hbm.at[pl.ds(0, 1)], dst.at[pl.ds(0, 1)], sem).wait()


def _combine_kernel(pos_ref, pos_next_ref, ys_hbm, wgt_ref, x1_ref, m_ref, g_ref, b_ref, o_ref, buf, sem,
                    *, n_steps):
    i = pl.program_id(0)
    slot = i % 2

    def start(p_ref, s):
        for k in range(TOP_K):
            _start_rows(ys_hbm, p_ref, (0, k), buf.at[s, k], sem.at[s], COMBINE_TOK)

    @pl.when(i == 0)
    def _():
        start(pos_ref, 0)

    @pl.when(i + 1 < n_steps)
    def _():
        start(pos_next_ref, 1 - slot)

    _wait_rows(ys_hbm, buf.at[slot, 0], sem.at[slot], TOP_K * COMBINE_TOK)
    w = wgt_ref[...]
    ffn = w[:, 0:1] * buf[slot, 0]
    for k in range(1, TOP_K):
        ffn = ffn + w[:, k:k + 1] * buf[slot, k]
    x2 = _layer_norm(DN_ALPHA * x1_ref[...] + m_ref[0, 5:6, :] * ffn) * g_ref[...] + b_ref[...]
    o_ref[...] = x2


def _combine(ys, pos_blocks, wgt, x1, mods, ln_g, ln_b, *, rows, row0):
    tm = COMBINE_TOK
    b0 = row0 // tm
    n = rows // tm
    const = lambda i: (0, 0)
    pos_spec = lambda f: pl.BlockSpec((1, TOP_K, tm), f, memory_space=pltpu.SMEM)
    return pl.pallas_call(
        functools.partial(_combine_kernel, n_steps=n),
        grid=(n,),
        in_specs=[pos_spec(lambda i: (i + b0, 0, 0)),
                  pos_spec(lambda i: (jnp.minimum(i + 1, n - 1) + b0, 0, 0)),
                  pl.BlockSpec(memory_space=pl.ANY),
                  pl.BlockSpec((tm, TOP_K), lambda i: (i + b0, 0)),
                  pl.BlockSpec((tm, D_MODEL), lambda i: (i + b0, 0)),
                  pl.BlockSpec((1, 6, D_MODEL), lambda i: (_mod_row(i + b0, tm), 0, 0)),
                  pl.BlockSpec((1, D_MODEL), const), pl.BlockSpec((1, D_MODEL), const)],
        out_specs=pl.BlockSpec((tm, D_MODEL), lambda i: (i, 0)),
        out_shape=jax.ShapeDtypeStruct((rows, D_MODEL), F32),
        scratch_shapes=[pltpu.VMEM((2, TOP_K, tm, D_MODEL), F32), pltpu.SemaphoreType.DMA((2,))],
        compiler_params=_cparams(("arbitrary",)),
        name="moe_combine_ln2",
    )(pos_blocks, pos_blocks, ys, wgt, x1, mods, ln_g, ln_b)


def _moe_plan(idx_t):
    flat = idx_t.reshape(-1)
    onehot = (flat[:, None] == jnp.arange(N_EXPERTS)[None, :]).astype(jnp.int32)
    rank = jnp.take_along_axis(jnp.cumsum(onehot, axis=0), flat[:, None], axis=1)[:, 0] - 1
    counts = jnp.sum(onehot, axis=0)
    tiles = (counts + MOE_TM - 1) // MOE_TM
    tile_end = jnp.cumsum(tiles)
    tile_start = tile_end - tiles
    pos = tile_start[flat] * MOE_TM + rank
    n_valid = tile_end[-1]
    per_tile = lambda n_chunks: (n_chunks + jnp.maximum(tiles, 1) - 1) // jnp.maximum(tiles, 1)
    i32 = lambda a: a.astype(jnp.int32)
    plan = lambda rate: (i32(tile_start), i32(tiles), i32(rate), i32(n_valid.reshape(1)))
    return (i32(pos.reshape(TOP_K, T_ALL)), plan(jnp.minimum(per_tile(GU_CHUNKS), GU_RING)),
            plan(per_tile(W_CHUNKS)))


def kernel(x_prompt, x_sample, cache_na_k, cache_na_v, state_ssd_fwd, state_ssd_bwd, c, c_ctx, w_mod, b_mod, w_in, ssd_conv_w, ssd_conv_b, ssd_dt_bias, ssd_a_log, ssd_d, ssd_norm_w, na_rpb, w_ssd_out, w_na_out, w_o, ln1_g, ln1_b, ln2_g, ln2_b, w_router, b_router, w_gate_up, b_gate_up, w_down, b_down):
    assert w_mod.shape[0] == 1, "single-layer trunk"
    x_ctx = x_prompt.reshape(T_CTX, D_MODEL)
    x_dec = x_sample.reshape(T_DEC, D_MODEL)

    cvec = jnp.concatenate([c_ctx[None], c, jnp.zeros((8 - 1 - DEC_BATCH, D_MODEL), F32)], axis=0)
    mods = _modulation(cvec, w_mod[0], b_mod[0])[:1 + DEC_BATCH].reshape(1 + DEC_BATCH, 6, D_MODEL)

    h = _ln_modulate(x_ctx, x_dec, mods)
    w = w_in[0]
    n_main = 3 * NA_WIDTH + SSD_WIDTH + SSD_CONV_CH
    w_bf = w.astype(BF16)
    w_dt = jnp.pad(w[:, n_main:n_main + 2 * SSD_HEADS], ((0, 0), (0, 128 - 2 * SSD_HEADS))).astype(BF16)
    w_gates = w_bf[:, n_main + 2 * SSD_HEADS:]
    tm = 1024
    ctx_blocks = T_CTX // tm
    q_all = _matmul(h, w_bf, rows=T_ALL, col_block0=0, n_out=NA_WIDTH, out_dtype=BF16,
                    name="proj_q")
    k_ctx = _matmul(h, w_bf, rows=T_CTX, col_block0=2, n_out=NA_WIDTH, name="proj_k_ctx")
    k_dec = _matmul(h, w_bf, rows=T_DEC, row_block0=ctx_blocks, col_block0=2, n_out=NA_WIDTH, out_dtype=BF16,
                    name="proj_k_dec")
    v_ctx = _matmul(h, w_bf, rows=T_CTX, col_block0=4, n_out=NA_WIDTH, name="proj_v_ctx")
    v_dec = _matmul(h, w_bf, rows=T_DEC, row_block0=ctx_blocks, col_block0=4, n_out=NA_WIDTH, out_dtype=BF16,
                    name="proj_v_dec")
    zx = _matmul(h, w_bf, rows=T_ALL, col_block0=6, n_out=SSD_WIDTH + SSD_CONV_CH, name="proj_zxbc")
    gates = _matmul(h, w_gates, rows=T_ALL, name="proj_gates")
    dt_all = _matmul(h, w_dt, rows=T_ALL, tn=128, name="proj_dt")

    na_ctx = _context_attention(q_all, k_ctx, v_ctx)
    kc = cache_na_k[:, 0].reshape(DEC_BATCH, PAST_LEN, NA_WIDTH)
    vc = cache_na_v[:, 0].reshape(DEC_BATCH, PAST_LEN, NA_WIDTH)
    na_dec = _neighbourhood_attention(q_all, k_dec, v_dec, kc, vc, _rpb_table(na_rpb[0]))

    consts = _ssd_consts(ssd_conv_w[0], ssd_conv_b[0], ssd_dt_bias[0], ssd_a_log[0], ssd_d[0])
    dt_t = dt_all[:, :2 * SSD_HEADS].reshape(T_ALL // SSD_CHUNK, SSD_CHUNK, 2, SSD_GROUPS, HEADS_PER_GROUP)
    dt_t = jnp.transpose(dt_t, (0, 2, 3, 4, 1))
    u_ctx, h_f, h_b = _ssd_mixer(zx, dt_all, dt_t, consts, length=SEQ, n_seq=BATCH, seq_block0=0,
                                 emit_state=True)
    h0 = (state_ssd_fwd[:, 0].reshape(DEC_BATCH, SSD_GROUPS, HEADS_PER_GROUP, SSD_HEAD_DIM, SSD_STATE),
          state_ssd_bwd[:, 0].reshape(DEC_BATCH, SSD_GROUPS, HEADS_PER_GROUP, SSD_HEAD_DIM, SSD_STATE))
    (u_dec,) = _ssd_mixer(zx, dt_all, dt_t, consts, length=DEC_SEQ, n_seq=DEC_BATCH,
                          seq_block0=T_CTX // DEC_SEQ, h0=h0)

    y = _merge_branches(na_ctx, na_dec, u_ctx, u_dec, ssd_norm_w[0].reshape(1, SSD_WIDTH),
                        w_na_out[0].astype(BF16), w_ssd_out[0].astype(BF16), gates)
    x1, h2, idx_t, wgt_t = _post_mix(y, w_o[0].astype(BF16), x_ctx, x_dec, mods,
                                     ln1_g[0].reshape(1, D_MODEL), ln1_b[0].reshape(1, D_MODEL),
                                     jnp.transpose(w_router[0]), b_router[0].reshape(N_EXPERTS, 1))

    pos, plan_gu, plan_down = _moe_plan(idx_t)
    tile_start, tiles, _, n_valid = plan_gu
    pos_d = jnp.transpose(pos.reshape(TOP_K, T_ALL // DISPATCH_TOK, DISPATCH_TOK), (1, 0, 2))
    xs = _dispatch(h2, pos_d, tile_start, tiles, n_valid)
    hid = _moe_gate_up(xs, plan_gu, w_gate_up[0], b_gate_up[0].reshape(N_EXPERTS, 1, 2 * D_EXPERT))
    ys = _moe_down(hid, plan_down, w_down[0], b_down[0].reshape(N_EXPERTS, 1, D_MODEL))
    pos_c = jnp.transpose(pos.reshape(TOP_K, T_ALL // COMBINE_TOK, COMBINE_TOK), (1, 0, 2))
    wgt = jnp.transpose(wgt_t)
    g2, b2 = ln2_g[0].reshape(1, D_MODEL), ln2_b[0].reshape(1, D_MODEL)
    y_ctx = _combine(ys, pos_c, wgt, x1, mods, g2, b2, rows=T_CTX, row0=0)
    y_dec = _combine(ys, pos_c, wgt, x1, mods, g2, b2, rows=T_DEC, row0=T_CTX)

    return (y_ctx.reshape(BATCH, SEQ, D_MODEL),
            y_dec.reshape(DEC_BATCH, DEC_SEQ, D_MODEL),
            k_ctx.reshape(BATCH, 1, SEQ, NA_HEADS, NA_HEAD_DIM),
            v_ctx.reshape(BATCH, 1, SEQ, NA_HEADS, NA_HEAD_DIM),
            h_f.reshape(BATCH, 1, SSD_HEADS, SSD_HEAD_DIM, SSD_STATE),
            h_b.reshape(BATCH, 1, SSD_HEADS, SSD_HEAD_DIM, SSD_STATE))
```

```python
import functools

import jax
import jax.numpy as jnp
from jax import lax
from jax.experimental import pallas as pl
from jax.experimental.pallas import tpu as pltpu

F32 = jnp.float32
BF16 = jnp.bfloat16

D_MODEL = 2048
BATCH = 32
SEQ = 256
DEC_BATCH = 2
DEC_SEQ = 1024
PAST_LEN = 512
GRID_W = 64
NA_HEADS = 16
NA_HEAD_DIM = 128
NA_WIDTH = NA_HEADS * NA_HEAD_DIM
NA_WIN_R = 8
NA_WIN_C = 16
SSD_HEADS = 32
SSD_HEAD_DIM = 64
SSD_WIDTH = SSD_HEADS * SSD_HEAD_DIM
SSD_GROUPS = 4
SSD_STATE = 128
SSD_CONV = 5
SSD_CHUNK = 128
SSD_CONV_CH = SSD_WIDTH + 2 * SSD_GROUPS * SSD_STATE
N_EXPERTS = 32
TOP_K = 4
D_EXPERT = 2048
SWIGLU_LIMIT = 7.0
SWIGLU_ALPHA = 1.702
DN_ALPHA = 2.0 ** 0.25
LN_EPS = 1e-5

T_CTX = BATCH * SEQ
T_DEC = DEC_BATCH * DEC_SEQ
T_ALL = T_CTX + T_DEC
HEADS_PER_GROUP = SSD_HEADS // SSD_GROUPS
GROUP_W = HEADS_PER_GROUP * SSD_HEAD_DIM
ROWS = DEC_SEQ // GRID_W
RPB_SLOTS = 16

MOE_TM = 256
MOE_TILES = T_ALL * TOP_K // MOE_TM + N_EXPERTS
MOE_ROWS = MOE_TILES * MOE_TM

VMEM_LIMIT = 56 * 1024 * 1024


def _cparams(sem):
    return pltpu.CompilerParams(dimension_semantics=sem, vmem_limit_bytes=VMEM_LIMIT)


def _split3(x):
    hi = x.astype(BF16)
    r1 = x - hi.astype(F32)
    mid = r1.astype(BF16)
    lo = (r1 - mid.astype(F32)).astype(BF16)
    return hi, mid, lo


def _dot(a, b):
    return jnp.dot(a, b, preferred_element_type=F32)


def _dot_nt(a, b):
    return lax.dot_general(a, b, (((1,), (1,)), ((), ())), preferred_element_type=F32)


def _dot_exact_rhs(x, m_bf):
    hi, mid, lo = _split3(x)
    return _dot(hi, m_bf) + (_dot(mid, m_bf) + _dot(lo, m_bf))


def _dot_hi_mid_rhs(x, m_bf):
    hi = x.astype(BF16)
    mid = (x - hi.astype(F32)).astype(BF16)
    return _dot(hi, m_bf) + _dot(mid, m_bf)


def _dot_exact_lhs(m_bf, x):
    hi, mid, lo = _split3(x)
    return _dot(m_bf, hi) + (_dot(m_bf, mid) + _dot(m_bf, lo))


def _dot_nt_f32(a, b):
    ah, am, al = _split3(a)
    bh, bm, bl = _split3(b)
    small = _dot_nt(ah, bl) + _dot_nt(am, bm) + _dot_nt(al, bh)
    mid = _dot_nt(ah, bm) + _dot_nt(am, bh)
    return _dot_nt(ah, bh) + (mid + small)


def _dot_f32(a, b):
    ah, am, al = _split3(a)
    bh, bm, bl = _split3(b)
    small = _dot(ah, bl) + _dot(am, bm) + _dot(al, bh)
    mid = _dot(ah, bm) + _dot(am, bh)
    return _dot(ah, bh) + (mid + small)


def _silu(x):
    return x * jax.nn.sigmoid(x)


def _softplus(x):
    return jnp.maximum(x, 0.0) + jnp.log1p(jnp.exp(-jnp.abs(x)))


def _layer_norm(x):
    mu = jnp.mean(x, axis=-1, keepdims=True)
    xc = x - mu
    var = jnp.mean(xc * xc, axis=-1, keepdims=True)
    return xc * lax.rsqrt(var + LN_EPS)


def _mod_row(i, tm):
    n_ctx = T_CTX // tm
    per_b = DEC_SEQ // tm
    return jnp.where(i < n_ctx, 0, 1 + (i - n_ctx) // per_b)


def _mod_kernel(c_ref, w_ref, b_ref, o_ref):
    o_ref[...] = _dot_f32(_silu(c_ref[...]), w_ref[...]) + b_ref[...]


def _modulation(cvec, w_mod, b_mod):
    tn = 1024
    n = w_mod.shape[1]
    return pl.pallas_call(
        _mod_kernel,
        grid=(n // tn,),
        in_specs=[pl.BlockSpec((8, D_MODEL), lambda j: (0, 0)),
                  pl.BlockSpec((D_MODEL, tn), lambda j: (0, j)),
                  pl.BlockSpec((1, tn), lambda j: (0, j))],
        out_specs=pl.BlockSpec((8, tn), lambda j: (0, j)),
        out_shape=jax.ShapeDtypeStruct((8, n), F32),
        compiler_params=_cparams(("arbitrary",)),
        name="modulation",
    )(cvec, w_mod, b_mod.reshape(1, n))


def _ln_mod_kernel(xc_ref, xd_ref, m_ref, o_ref, *, n_ctx):
    i = pl.program_id(0)
    x = jnp.where(i < n_ctx, xc_ref[...], xd_ref[...])
    y = _layer_norm(x) * (1.0 + m_ref[0, 1:2, :]) + m_ref[0, 0:1, :]
    o_ref[...] = y.astype(o_ref.dtype)


def _ln_modulate(x_ctx, x_dec, mods):
    tm = 512
    n_ctx = T_CTX // tm
    return pl.pallas_call(
        functools.partial(_ln_mod_kernel, n_ctx=n_ctx),
        grid=(T_ALL // tm,),
        in_specs=[pl.BlockSpec((tm, D_MODEL), lambda i: (jnp.minimum(i, n_ctx - 1), 0)),
                  pl.BlockSpec((tm, D_MODEL), lambda i: (jnp.maximum(i - n_ctx, 0), 0)),
                  pl.BlockSpec((1, 6, D_MODEL), lambda i: (_mod_row(i, tm), 0, 0))],
        out_specs=pl.BlockSpec((tm, D_MODEL), lambda i: (i, 0)),
        out_shape=jax.ShapeDtypeStruct((T_ALL, D_MODEL), BF16),
        compiler_params=_cparams(("arbitrary",)),
        name="ln_modulate",
    )(x_ctx, x_dec, mods)


def _mm_kernel(x_ref, w_ref, o_ref):
    o_ref[...] = _dot(x_ref[...], w_ref[...]).astype(o_ref.dtype)


def _matmul(x, w, *, rows, row_block0=0, col_block0=0, n_out=None, tm=1024, tn=1024,
            out_dtype=F32, name="matmul"):
    k = x.shape[1]
    n_out = w.shape[1] if n_out is None else n_out
    return pl.pallas_call(
        _mm_kernel,
        grid=(rows // tm, n_out // tn),
        in_specs=[pl.BlockSpec((tm, k), lambda i, j: (i + row_block0, 0)),
                  pl.BlockSpec((k, tn), lambda i, j: (0, j + col_block0))],
        out_specs=pl.BlockSpec((tm, tn), lambda i, j: (i, j)),
        out_shape=jax.ShapeDtypeStruct((rows, n_out), out_dtype),
        compiler_params=_cparams(("arbitrary", "arbitrary")),
        name=name,
    )(x, w)


def _ctx_attn_kernel(q_ref, k_ref, v_ref, o_ref):
    scale = NA_HEAD_DIM ** -0.5
    for h in range(NA_HEADS):
        sl = slice(h * NA_HEAD_DIM, (h + 1) * NA_HEAD_DIM)
        q = q_ref[:, sl].astype(BF16)
        k = k_ref[:, sl].astype(BF16)
        v = v_ref[:, sl].astype(BF16)
        s = _dot_nt(q, k) * scale
        m = jnp.max(s, axis=-1, keepdims=True)
        p = jnp.exp(s - m)
        l = jnp.sum(p, axis=-1, keepdims=True)
        o_ref[:, sl] = (_dot(p.astype(BF16), v) / l).astype(o_ref.dtype)


def _context_attention(q_all, k_ctx, v_ctx):
    spec = pl.BlockSpec((SEQ, NA_WIDTH), lambda b: (b, 0))
    return pl.pallas_call(
        _ctx_attn_kernel,
        grid=(BATCH,),
        in_specs=[spec, spec, spec],
        out_specs=spec,
        out_shape=jax.ShapeDtypeStruct((T_CTX, NA_WIDTH), BF16),
        compiler_params=_cparams(("arbitrary",)),
        name="context_attention",
    )(q_all, k_ctx, v_ctx)


def _nbr_attn_kernel(q_ref, k_ref, v_ref, kc_ref, vc_ref, rc_ref, o_ref):
    scale = NA_HEAD_DIM ** -0.5
    kr = min(NA_WIN_R, ROWS)
    kc = kc_ref[0].astype(BF16)
    vc = vc_ref[0].astype(BF16)
    band_start = lambda r: min(max(r - kr // 2, 0), ROWS - kr)
    groups = {}
    for r in range(ROWS):
        groups.setdefault(band_start(r), []).append(r)
    for r0, rows in groups.items():
        lo, hi = rows[0] * GRID_W, (rows[-1] + 1) * GRID_W
        q = q_ref[lo:hi, :].astype(BF16)
        kb = k_ref[r0 * GRID_W:(r0 + kr) * GRID_W, :].astype(BF16)
        vb = v_ref[r0 * GRID_W:(r0 + kr) * GRID_W, :].astype(BF16)
        offs = [(r0 - r + NA_WIN_R - 1) * GRID_W for r in rows]
        bias = jnp.concatenate([rc_ref[0, :, off:off + kr * GRID_W] for off in offs], axis=0)
        s_loc = _dot_nt(q, kb) * scale + bias
        s_ctx = _dot_nt(q, kc) * scale
        m = jnp.maximum(jnp.max(s_loc, axis=-1, keepdims=True),
                        jnp.max(s_ctx, axis=-1, keepdims=True))
        p_loc = jnp.exp(s_loc - m)
        p_ctx = jnp.exp(s_ctx - m)
        l = jnp.sum(p_loc, axis=-1, keepdims=True) + jnp.sum(p_ctx, axis=-1, keepdims=True)
        o = _dot(p_loc.astype(BF16), vb) + _dot(p_ctx.astype(BF16), vc)
        o_ref[lo:hi, :] = (o / l).astype(o_ref.dtype)


def _rpb_table(rpb):
    col = jnp.arange(GRID_W)
    c0 = jnp.clip(col - NA_WIN_C // 2, 0, GRID_W - NA_WIN_C)
    col_mask = (col[None, :] >= c0[:, None]) & (col[None, :] < c0[:, None] + NA_WIN_C)
    dc_idx = jnp.clip(col[None, :] - col[:, None] + NA_WIN_C - 1, 0, 2 * NA_WIN_C - 2)
    pick = (dc_idx[:, :, None] == jnp.arange(2 * NA_WIN_C - 1)[None, None, :]).astype(F32)
    t = jnp.einsum('hdj,qkj->hqdk', rpb, pick, precision=lax.Precision.HIGHEST)
    t = jnp.where(col_mask[None, :, None, :], t, -jnp.inf)
    t = jnp.pad(t, ((0, 0), (0, 0), (0, RPB_SLOTS - t.shape[2]), (0, 0)))
    return t.reshape(NA_HEADS, GRID_W, RPB_SLOTS * GRID_W)


def _neighbourhood_attention(q_all, k_dec, v_dec, kc, vc, rc):
    q_row0 = T_CTX // DEC_SEQ
    hd = NA_HEAD_DIM
    return pl.pallas_call(
        _nbr_attn_kernel,
        grid=(DEC_BATCH, NA_HEADS),
        in_specs=[pl.BlockSpec((DEC_SEQ, hd), lambda b, h: (b + q_row0, h)),
                  pl.BlockSpec((DEC_SEQ, hd), lambda b, h: (b, h)),
                  pl.BlockSpec((DEC_SEQ, hd), lambda b, h: (b, h)),
                  pl.BlockSpec((1, PAST_LEN, hd), lambda b, h: (b, 0, h)),
                  pl.BlockSpec((1, PAST_LEN, hd), lambda b, h: (b, 0, h)),
                  pl.BlockSpec((1, GRID_W, RPB_SLOTS * GRID_W), lambda b, h: (h, 0, 0))],
        out_specs=pl.BlockSpec((DEC_SEQ, hd), lambda b, h: (b, h)),
        out_shape=jax.ShapeDtypeStruct((T_DEC, NA_WIDTH), BF16),
        compiler_params=_cparams(("arbitrary", "arbitrary")),
        name="neighbourhood_attention",
    )(q_all, k_dec, v_dec, kc, vc, rc)


CONV_HALO = 8


def _conv_silu(u_ref, w_ref, b_ref, pad_s, length):
    width = u_ref.shape[1]
    halo = jnp.zeros((CONV_HALO, width), F32)
    pad_s[0:CONV_HALO, 0:width] = halo
    pad_s[CONV_HALO + length:2 * CONV_HALO + length, 0:width] = halo
    pad_s[CONV_HALO:CONV_HALO + length, 0:width] = u_ref[...]
    acc = b_ref[...]
    for k in range(SSD_CONV):
        d = k - SSD_CONV // 2
        acc = acc + pad_s[CONV_HALO + d:CONV_HALO + d + length, 0:width] * w_ref[k:k + 1, :]
    return _silu(acc)


def _ssd_kernel(*refs, length, has_h0, emit_state):
    (xs_ref, b_ref, c_ref, z_ref, dt_ref, dtt_ref, cwx_ref, cwb_ref, cwc_ref,
     cbx_ref, cbb_ref, cbc_ref, dtb_row_ref, dtb_col_ref, al_row_ref, al_col_ref,
     e_ref, d_ref) = refs[:18]
    pos = 18
    if has_h0:
        h0f_ref, h0b_ref = refs[pos:pos + 2]
        pos += 2
    u_ref = refs[pos]
    pos += 1
    if emit_state:
        hf_ref, hb_ref = refs[pos:pos + 2]
        pos += 2
    xs_s, b_s, c_s, y_s, st_s, pad_s = refs[pos:]

    q = SSD_CHUNK
    nc = length // q
    xs_s[...] = _conv_silu(xs_ref, cwx_ref, cbx_ref, pad_s, length)
    b_s[...] = _conv_silu(b_ref, cwb_ref, cbb_ref, pad_s, length)
    c_s[...] = _conv_silu(c_ref, cwc_ref, cbc_ref, pad_s, length)
    y_s[...] = d_ref[0] * xs_s[...]

    for dirn in range(2):
        if has_h0:
            h0 = (h0f_ref if dirn == 0 else h0b_ref)[0, 0]
            st_s[dirn] = jnp.transpose(h0.reshape(GROUP_W, SSD_STATE))
        else:
            st_s[dirn] = jnp.zeros((SSD_STATE, GROUP_W), F32)

    ri = lax.broadcasted_iota(jnp.int32, (q, q), 0)
    ci = lax.broadcasted_iota(jnp.int32, (q, q), 1)
    lower = ri >= ci
    upper = ri <= ci
    lower_bf = jnp.where(lower, 1.0, 0.0).astype(BF16)
    upper_bf = jnp.where(upper, 1.0, 0.0).astype(BF16)

    shared = {}

    def chunk_common(c, r0):
        if isinstance(c, int) and c in shared:
            return shared[c]
        xs_c = xs_s[pl.ds(r0, q), :]
        bc = b_s[pl.ds(r0, q), :]
        cc_bf = c_s[pl.ds(r0, q), :].astype(BF16)
        dtp = _softplus(dt_ref[pl.ds(r0, q), :] + dtb_row_ref[...])
        da = dtp * (-jnp.exp(al_row_ref[...]))
        pre = _dot_exact_lhs(lower_bf, da)
        cb = _dot_nt(cc_bf, bc.astype(BF16))
        bt = jnp.transpose(bc).astype(BF16)
        vals = (xs_c, cc_bf, dtp, da, pre, cb, bt)
        if isinstance(c, int):
            shared[c] = vals
        return vals

    def chunk(c, dirn):
        r0 = c * q if isinstance(c, int) else pl.multiple_of(c * q, q)
        tri = lower if dirn == 0 else upper
        xs_c, cc_bf, dtp, da, pre, cb, bt = chunk_common(c, r0)
        cs = pre if dirn == 0 else pre[q - 1:q, :] - pre + da
        e_bf = e_ref[dirn, 0]
        dt_e = _dot_hi_mid_rhs(dtp, e_bf)
        cs_e = _dot_exact_rhs(cs, e_bf)
        dt_t = _softplus(dtt_ref[c, dirn, 0] + dtb_col_ref[dirn, 0])
        da_t = dt_t * (-jnp.exp(al_col_ref[dirn, 0]))
        cs_t = _dot_exact_rhs(da_t, upper_bf if dirn == 0 else lower_bf)

        end = q - 1 if dirn == 0 else 0
        cs_end = cs_e[end:end + 1, :]
        xdt = xs_c * dt_e
        st = st_s[dirn]
        y = _dot(cc_bf, st.astype(BF16)) * jnp.exp(cs_e)
        st_s[dirn] = jnp.exp(cs_end) * st + _dot(bt, (xdt * jnp.exp(cs_end - cs_e)).astype(BF16))
        xdt_bf = xdt.astype(BF16)
        parts = []
        for r in range(HEADS_PER_GROUP):
            sl = slice(r * SSD_HEAD_DIM, (r + 1) * SSD_HEAD_DIM)
            diff = cs_e[:, r * SSD_HEAD_DIM:r * SSD_HEAD_DIM + 1] - cs_t[r:r + 1, :]
            lm = jnp.exp(jnp.where(tri, diff, -jnp.inf))
            parts.append(_dot((cb * lm).astype(BF16), xdt_bf[:, sl]))
        y = y + jnp.concatenate(parts, axis=-1)
        y_s[pl.ds(r0, q), :] = y_s[pl.ds(r0, q), :] + y

    if nc <= 2:
        for i in range(nc):
            chunk(i, 0)
            chunk(nc - 1 - i, 1)
    else:
        def both(i, carry):
            chunk(i, 0)
            chunk(nc - 1 - i, 1)
            return carry
        lax.fori_loop(0, nc, both, 0)

    u_ref[...] = y_s[...] * _silu(z_ref[...])
    if emit_state:
        hf_ref[0, 0] = jnp.transpose(st_s[0]).reshape(HEADS_PER_GROUP, SSD_HEAD_DIM, SSD_STATE)
        hb_ref[0, 0] = jnp.transpose(st_s[1]).reshape(HEADS_PER_GROUP, SSD_HEAD_DIM, SSD_STATE)


def _ssd_mixer(zx, dt_all, dt_t, consts, *, length, n_seq, seq_block0, h0=None, emit_state=False):
    (cw, cb, dtb_row, dtb_col, al_row, al_col, expand, d_row) = consts
    g_w, n_s = GROUP_W, SSD_STATE
    xs_cb0 = D_MODEL // g_w
    b_cb0 = (D_MODEL + SSD_WIDTH) // n_s
    c_cb0 = b_cb0 + SSD_GROUPS
    cw_b0 = SSD_WIDTH // n_s
    nck = length // SSD_CHUNK
    in_specs = [
        pl.BlockSpec((length, g_w), lambda s, g: (s + seq_block0, xs_cb0 + g)),
        pl.BlockSpec((length, n_s), lambda s, g: (s + seq_block0, b_cb0 + g)),
        pl.BlockSpec((length, n_s), lambda s, g: (s + seq_block0, c_cb0 + g)),
        pl.BlockSpec((length, g_w), lambda s, g: (s + seq_block0, g)),
        pl.BlockSpec((length, 128), lambda s, g: (s + seq_block0, 0)),
        pl.BlockSpec((nck, 2, 1, HEADS_PER_GROUP, SSD_CHUNK), lambda s, g: (s + seq_block0, 0, g, 0, 0)),
        pl.BlockSpec((SSD_CONV, g_w), lambda s, g: (0, g)),
        pl.BlockSpec((SSD_CONV, n_s), lambda s, g: (0, cw_b0 + g)),
        pl.BlockSpec((SSD_CONV, n_s), lambda s, g: (0, cw_b0 + SSD_GROUPS + g)),
        pl.BlockSpec((1, g_w), lambda s, g: (0, g)),
        pl.BlockSpec((1, n_s), lambda s, g: (0, cw_b0 + g)),
        pl.BlockSpec((1, n_s), lambda s, g: (0, cw_b0 + SSD_GROUPS + g)),
        pl.BlockSpec((1, 128), lambda s, g: (0, 0)),
        pl.BlockSpec((2, 1, HEADS_PER_GROUP, SSD_CHUNK), lambda s, g: (0, g, 0, 0)),
        pl.BlockSpec((1, 128), lambda s, g: (0, 0)),
        pl.BlockSpec((2, 1, HEADS_PER_GROUP, SSD_CHUNK), lambda s, g: (0, g, 0, 0)),
        pl.BlockSpec((2, 1, 128, g_w), lambda s, g: (0, g, 0, 0)),
        pl.BlockSpec((1, 1, g_w), lambda s, g: (g, 0, 0)),
    ]
    args = [zx, zx, zx, zx, dt_all, dt_t, cw, cw, cw, cb, cb, cb,
            dtb_row, dtb_col, al_row, al_col, expand, d_row]
    st_spec = pl.BlockSpec((1, 1, HEADS_PER_GROUP, SSD_HEAD_DIM, n_s), lambda s, g: (s, g, 0, 0, 0))
    if h0 is not None:
        in_specs += [st_spec, st_spec]
        args += list(h0)
    out_specs = [pl.BlockSpec((length, g_w), lambda s, g: (s, g))]
    out_shape = [jax.ShapeDtypeStruct((n_seq * length, SSD_WIDTH), F32)]
    if emit_state:
        st_shape = jax.ShapeDtypeStruct((n_seq, SSD_GROUPS, HEADS_PER_GROUP, SSD_HEAD_DIM, n_s), F32)
        out_specs += [st_spec, st_spec]
        out_shape += [st_shape, st_shape]
    return pl.pallas_call(
        functools.partial(_ssd_kernel, length=length, has_h0=h0 is not None, emit_state=emit_state),
        grid=(n_seq, SSD_GROUPS),
        in_specs=in_specs,
        out_specs=out_specs,
        out_shape=out_shape,
        scratch_shapes=[pltpu.VMEM((length, g_w), F32), pltpu.VMEM((length, n_s), F32),
                        pltpu.VMEM((length, n_s), F32), pltpu.VMEM((length, g_w), F32),
                        pltpu.VMEM((2, n_s, g_w), F32),
                        pltpu.VMEM((length + 2 * CONV_HALO, g_w), F32)],
        compiler_params=_cparams(("arbitrary", "arbitrary")),
        name="ssd_mixer_%d" % length,
    )(*args)


def _ssd_consts(conv_w, conv_b, dt_bias, a_log, d_skip):
    hpg = HEADS_PER_GROUP
    dtb_row = jnp.pad(dt_bias.reshape(1, 2 * SSD_HEADS), ((0, 0), (0, 128 - 2 * SSD_HEADS)))
    col = lambda p: jnp.broadcast_to(p.reshape(2, SSD_GROUPS, hpg, 1), (2, SSD_GROUPS, hpg, SSD_CHUNK))
    al_row = jnp.pad(a_log.reshape(1, 2 * SSD_HEADS), ((0, 0), (0, 128 - 2 * SSD_HEADS)))
    src = (jnp.arange(2)[:, None, None] * SSD_HEADS + jnp.arange(SSD_GROUPS)[None, :, None] * hpg
           + jnp.arange(GROUP_W)[None, None, :] // SSD_HEAD_DIM)
    expand = (jnp.arange(128)[None, None, :, None] == src[:, :, None, :]).astype(BF16)
    d_row = jnp.repeat(d_skip.reshape(SSD_GROUPS, 1, hpg), SSD_HEAD_DIM, axis=-1)
    return (conv_w, conv_b.reshape(1, SSD_CONV_CH), dtb_row, col(dt_bias), al_row, col(a_log), expand, d_row)


def _merge_kernel(nac_ref, nad_ref, uc_ref, ud_ref, nw_ref, w1_ref, w2_ref, g_na_ref, g_ssd_ref,
                  o_ref, *, n_ctx):
    is_ctx = pl.program_id(0) < n_ctx
    a1 = jnp.where(is_ctx, nac_ref[...], nad_ref[...]).astype(BF16)
    u = jnp.where(is_ctx, uc_ref[...], ud_ref[...])
    r = lax.rsqrt(jnp.mean(u * u, axis=-1, keepdims=True) + LN_EPS)
    a2 = (u * r * nw_ref[...]).astype(BF16)
    o = (jax.nn.sigmoid(g_na_ref[...]) * _dot(a1, w1_ref[...])
         + jax.nn.sigmoid(g_ssd_ref[...]) * _dot(a2, w2_ref[...]))
    o_ref[...] = o.astype(o_ref.dtype)


def _merge_branches(na_ctx, na_dec, u_ctx, u_dec, norm_w, w_na_out, w_ssd_out, gates):
    tm = 256
    n_ctx = T_CTX // tm
    ctx_rows = lambda i: (jnp.minimum(i, n_ctx - 1), 0)
    dec_rows = lambda i: (jnp.maximum(i - n_ctx, 0), 0)
    const = lambda i: (0, 0)
    resident = pl.Buffered(1)
    return pl.pallas_call(
        functools.partial(_merge_kernel, n_ctx=n_ctx),
        grid=(T_ALL // tm,),
        in_specs=[pl.BlockSpec((tm, D_MODEL), ctx_rows), pl.BlockSpec((tm, D_MODEL), dec_rows),
                  pl.BlockSpec((tm, D_MODEL), ctx_rows), pl.BlockSpec((tm, D_MODEL), dec_rows),
                  pl.BlockSpec((1, D_MODEL), const),
                  pl.BlockSpec((D_MODEL, D_MODEL), const, pipeline_mode=resident),
                  pl.BlockSpec((D_MODEL, D_MODEL), const, pipeline_mode=resident),
                  pl.BlockSpec((tm, D_MODEL), lambda i: (i, 1)),
                  pl.BlockSpec((tm, D_MODEL), lambda i: (i, 0))],
        out_specs=pl.BlockSpec((tm, D_MODEL), lambda i: (i, 0)),
        out_shape=jax.ShapeDtypeStruct((T_ALL, D_MODEL), BF16),
        compiler_params=_cparams(("arbitrary",)),
        name="merge_branches",
    )(na_ctx, na_dec, u_ctx, u_dec, norm_w, w_na_out, w_ssd_out, gates, gates)


def _post_mix_kernel(y_ref, wo_ref, xc_ref, xd_ref, m_ref, g_ref, b_ref, wr_ref, br_ref,
                     x1_ref, h2_ref, idx_ref, wgt_ref, *, n_ctx):
    i = pl.program_id(0)
    x = jnp.where(i < n_ctx, xc_ref[...], xd_ref[...])
    mix = _dot(y_ref[...], wo_ref[...])
    x1 = _layer_norm(DN_ALPHA * x + m_ref[0, 2:3, :] * mix) * g_ref[...] + b_ref[...]
    x1_ref[...] = x1
    h2 = _layer_norm(x1) * (1.0 + m_ref[0, 4:5, :]) + m_ref[0, 3:4, :]
    h2_ref[...] = h2
    logits = _dot_nt_f32(wr_ref[...], h2) + br_ref[...]
    eidx = lax.broadcasted_iota(jnp.int32, logits.shape, 0)
    vals, idxs = [], []
    for _ in range(TOP_K):
        m = jnp.max(logits, axis=0, keepdims=True)
        sel = jnp.min(jnp.where(logits == m, eidx, N_EXPERTS), axis=0, keepdims=True)
        logits = jnp.where(eidx == sel, -jnp.inf, logits)
        vals.append(m)
        idxs.append(sel)
    ex = [jnp.exp(v - vals[0]) for v in vals]
    tot = ex[0] + ex[1] + ex[2] + ex[3]
    idx_ref[...] = jnp.concatenate(idxs, axis=0)
    wgt_ref[...] = jnp.concatenate([e / tot for e in ex], axis=0)


def _post_mix(y, w_o, x_ctx, x_dec, mods, ln_g, ln_b, w_router_t, b_router):
    tm = 512
    n_ctx = T_CTX // tm
    row = lambda i: (i, 0)
    const = lambda i: (0, 0)
    return pl.pallas_call(
        functools.partial(_post_mix_kernel, n_ctx=n_ctx),
        grid=(T_ALL // tm,),
        in_specs=[pl.BlockSpec((tm, D_MODEL), row),
                  pl.BlockSpec((D_MODEL, D_MODEL), const, pipeline_mode=pl.Buffered(1)),
                  pl.BlockSpec((tm, D_MODEL), lambda i: (jnp.minimum(i, n_ctx - 1), 0)),
                  pl.BlockSpec((tm, D_MODEL), lambda i: (jnp.maximum(i - n_ctx, 0), 0)),
                  pl.BlockSpec((1, 6, D_MODEL), lambda i: (_mod_row(i, tm), 0, 0)),
                  pl.BlockSpec((1, D_MODEL), const), pl.BlockSpec((1, D_MODEL), const),
                  pl.BlockSpec((N_EXPERTS, D_MODEL), const), pl.BlockSpec((N_EXPERTS, 1), const)],
        out_specs=[pl.BlockSpec((tm, D_MODEL), row), pl.BlockSpec((tm, D_MODEL), row),
                   pl.BlockSpec((TOP_K, tm), lambda i: (0, i)), pl.BlockSpec((TOP_K, tm), lambda i: (0, i))],
        out_shape=[jax.ShapeDtypeStruct((T_ALL, D_MODEL), F32), jax.ShapeDtypeStruct((T_ALL, D_MODEL), F32),
                   jax.ShapeDtypeStruct((TOP_K, T_ALL), jnp.int32), jax.ShapeDtypeStruct((TOP_K, T_ALL), F32)],
        compiler_params=_cparams(("arbitrary",)),
        name="post_mix_router",
    )(y, w_o, x_ctx, x_dec, mods, ln_g, ln_b, w_router_t, b_router)


DISPATCH_TOK = 256


def _tile_rows(t):
    return pl.ds(pl.multiple_of(t * MOE_TM, MOE_TM), MOE_TM)


def _dispatch_kernel(t0_ref, nt_ref, nv_ref, pos_ref, x_hbm, xs_hbm, buf, zero_s, sem_in, sem_out, sem_z,
                     *, n_steps):
    i = pl.program_id(0)
    slot = i % 3

    def zero_copy(tile):
        return pltpu.make_async_copy(zero_s, xs_hbm.at[_tile_rows(tile)], sem_z)

    def in_copy(step, s):
        rows = pl.ds(pl.multiple_of(step * DISPATCH_TOK, DISPATCH_TOK), DISPATCH_TOK)
        return pltpu.make_async_copy(x_hbm.at[rows], buf.at[s], sem_in.at[s])

    def row_copy(s, t, dst):
        return pltpu.make_async_copy(buf.at[s, pl.ds(t, 1)], xs_hbm.at[pl.ds(dst, 1)], sem_out.at[s])

    def wait_rows(s):
        for _ in range(TOP_K * DISPATCH_TOK):
            row_copy(s, 0, 0).wait()

    @pl.when(i == 0)
    def _():
        in_copy(0, 0).start()
        zero_s[...] = jnp.zeros(zero_s.shape, zero_s.dtype)
        for e in range(N_EXPERTS):
            @pl.when(nt_ref[e] > 0)
            def _():
                zero_copy(t0_ref[e] + nt_ref[e] - 1).start()

        def tail_start(t, c):
            zero_copy(t).start()
            return c
        lax.fori_loop(nv_ref[0], MOE_TILES, tail_start, 0)
        for e in range(N_EXPERTS):
            @pl.when(nt_ref[e] > 0)
            def _():
                zero_copy(0).wait()

        def tail_wait(t, c):
            zero_copy(0).wait()
            return c
        lax.fori_loop(nv_ref[0], MOE_TILES, tail_wait, 0)

    in_copy(i, slot).wait()

    @pl.when(i + 1 < n_steps)
    def _():
        in_copy(i + 1, (i + 1) % 3).start()

    for k in range(TOP_K):
        for t in range(DISPATCH_TOK):
            row_copy(slot, t, pos_ref[0, k, t]).start(priority=t % 2)

    @pl.when(i > 0)
    def _():
        wait_rows((i + 2) % 3)

    @pl.when(i == n_steps - 1)
    def _():
        wait_rows(slot)


def _dispatch(h2, pos_blocks, tile_start, tiles, n_valid):
    n_steps = T_ALL // DISPATCH_TOK
    grid_spec = pltpu.PrefetchScalarGridSpec(
        num_scalar_prefetch=3,
        grid=(n_steps,),
        in_specs=[pl.BlockSpec((1, TOP_K, DISPATCH_TOK), lambda i, *_: (i, 0, 0), memory_space=pltpu.SMEM),
                  pl.BlockSpec(memory_space=pl.ANY)],
        out_specs=pl.BlockSpec(memory_space=pl.ANY),
        scratch_shapes=[pltpu.VMEM((3, DISPATCH_TOK, D_MODEL), F32),
                        pltpu.VMEM((MOE_TM, D_MODEL), F32),
                        pltpu.SemaphoreType.DMA((3,)), pltpu.SemaphoreType.DMA((3,)),
                        pltpu.SemaphoreType.DMA(())],
    )
    return pl.pallas_call(
        functools.partial(_dispatch_kernel, n_steps=n_steps),
        grid_spec=grid_spec,
        out_shape=jax.ShapeDtypeStruct((MOE_ROWS, D_MODEL), F32),
        compiler_params=_cparams(("arbitrary",)),
        name="moe_dispatch",
    )(tile_start, tiles, n_valid, pos_blocks, h2)


def _expert_tile_loop(t0, nt, in_copy, out_copy, compute, after_tile):
    @pl.when(nt > 0)
    def _():
        in_copy(t0, 0).start()

    def body(i, carry):
        slot = i % 2
        in_copy(t0 + i, slot).wait()

        @pl.when(i + 1 < nt)
        def _():
            in_copy(t0 + i + 1, 1 - slot).start()

        after_tile(i)

        @pl.when(i >= 2)
        def _():
            out_copy(t0 + i - 2, slot).wait()

        compute(slot)
        out_copy(t0 + i, slot).start()
        return carry

    lax.fori_loop(0, nt, body, 0)

    @pl.when(nt >= 2)
    def _():
        out_copy(t0, nt % 2).wait()

    @pl.when(nt >= 1)
    def _():
        out_copy(t0, (nt + 1) % 2).wait()


def _zero_tail_tiles(is_last, n_valid, obuf, out_copy):
    @pl.when(is_last)
    def _():
        obuf[0] = jnp.zeros(obuf.shape[1:], obuf.dtype)

        def body(t, carry):
            out_copy(t, 0).start()
            out_copy(t, 0).wait()
            return carry
        lax.fori_loop(n_valid, MOE_TILES, body, 0)


W_CHUNKS = 16


def _stream_weights(step, n_steps, chunks_per_tile, chunk_copy, stage, w_s, tile_loop):
    def start_chunks(target_step, lo, hi):
        def body(c, carry):
            chunk_copy(target_step, c).start()
            return carry
        lax.fori_loop(lo, hi, body, 0)

    @pl.when(step == 0)
    def _():
        start_chunks(0, 0, W_CHUNKS)

    for _ in range(W_CHUNKS):
        chunk_copy(0, 0).wait()
    w_s[...] = stage[...].astype(BF16)

    has_next = step + 1 < n_steps

    def after_tile(i):
        @pl.when(has_next)
        def _():
            start_chunks(step + 1, jnp.minimum(i * chunks_per_tile, W_CHUNKS),
                         jnp.minimum((i + 1) * chunks_per_tile, W_CHUNKS))

    n_tiles = tile_loop(after_tile)

    @pl.when(has_next)
    def _():
        start_chunks(step + 1, jnp.minimum(n_tiles * chunks_per_tile, W_CHUNKS), W_CHUNKS)


GU_CHUNKS = 16
GU_RING = 4


def _gate_up_kernel(t0_ref, nt_ref, cpt_ref, nv_ref, xs_hbm, w_hbm, bg_ref, bu_ref, hid_hbm,
                    ring, w_s, xbuf, obuf, sem_w, sem_in, sem_out):
    e = pl.program_id(0)
    cur = e % 2
    nxt = 1 - cur
    per_mat = GU_CHUNKS // 2
    rows = D_MODEL // per_mat

    def chunk_copy(expert, c):
        r = pl.multiple_of((c % per_mat) * rows, rows)
        col = pl.multiple_of((c // per_mat) * D_EXPERT, D_EXPERT)
        return pltpu.make_async_copy(w_hbm.at[expert, pl.ds(r, rows), pl.ds(col, D_EXPERT)],
                                     ring.at[c % GU_RING], sem_w.at[c % GU_RING])

    def issue(expert, lo, hi):
        def body(c, carry):
            chunk_copy(expert, c).start()
            return carry
        lax.fori_loop(lo, hi, body, 0)

    def land(buf, lo, hi):
        def body(c, carry):
            chunk_copy(0, c).wait()
            r = pl.multiple_of((c % per_mat) * rows, rows)
            w_s[buf, c // per_mat, pl.ds(r, rows), :] = ring[c % GU_RING].astype(BF16)
            return carry
        lax.fori_loop(lo, hi, body, 0)

    def fetch_exposed(expert, buf, lo):
        for rr in range(GU_CHUNKS // GU_RING):
            a = jnp.maximum(lo, rr * GU_RING)
            issue(expert, a, (rr + 1) * GU_RING)
            land(buf, a, (rr + 1) * GU_RING)

    @pl.when(e == 0)
    def _():
        fetch_exposed(0, 0, 0)

    has_next = e + 1 < N_EXPERTS
    k = cpt_ref[e]
    nt = nt_ref[e]
    clip = lambda v: jnp.clip(v, 0, GU_CHUNKS)

    def after_tile(i):
        @pl.when(has_next)
        def _():
            land(nxt, clip((i - 1) * k), clip(i * k))
            issue(e + 1, clip(i * k), clip((i + 1) * k))

    def in_copy(t, slot):
        return pltpu.make_async_copy(xs_hbm.at[_tile_rows(t)], xbuf.at[slot], sem_in.at[slot])

    def out_copy(t, slot):
        return pltpu.make_async_copy(obuf.at[slot], hid_hbm.at[_tile_rows(t)], sem_out.at[slot])

    def compute(slot):
        x = xbuf[slot].astype(BF16)
        gate = jnp.minimum(_dot(x, w_s[cur, 0]) + bg_ref[0], SWIGLU_LIMIT)
        up = jnp.clip(_dot(x, w_s[cur, 1]) + bu_ref[0], -SWIGLU_LIMIT, SWIGLU_LIMIT)
        hid = (up + 1.0) * gate * jax.nn.sigmoid(SWIGLU_ALPHA * gate)
        obuf[slot] = hid.astype(obuf.dtype)

    _expert_tile_loop(t0_ref[e], nt, in_copy, out_copy, compute, after_tile)

    @pl.when(has_next)
    def _():
        land(nxt, clip((nt - 1) * k), clip(nt * k))
        fetch_exposed(e + 1, nxt, clip(nt * k))

    _zero_tail_tiles(e == N_EXPERTS - 1, nv_ref[0], obuf, out_copy)


def _moe_gate_up(xs, plan, w_gate_up, b_gate_up):
    grid_spec = pltpu.PrefetchScalarGridSpec(
        num_scalar_prefetch=4,
        grid=(N_EXPERTS,),
        in_specs=[pl.BlockSpec(memory_space=pl.ANY), pl.BlockSpec(memory_space=pl.ANY),
                  pl.BlockSpec((1, 1, D_EXPERT), lambda e, *_: (e, 0, 0)),
                  pl.BlockSpec((1, 1, D_EXPERT), lambda e, *_: (e, 0, 1))],
        out_specs=pl.BlockSpec(memory_space=pl.ANY),
        scratch_shapes=[pltpu.VMEM((GU_RING, D_MODEL // (GU_CHUNKS // 2), D_EXPERT), F32),
                        pltpu.VMEM((2, 2, D_MODEL, D_EXPERT), BF16),
                        pltpu.VMEM((2, MOE_TM, D_MODEL), F32), pltpu.VMEM((2, MOE_TM, D_EXPERT), BF16),
                        pltpu.SemaphoreType.DMA((GU_RING,)), pltpu.SemaphoreType.DMA((2,)),
                        pltpu.SemaphoreType.DMA((2,))],
    )
    return pl.pallas_call(
        _gate_up_kernel,
        grid_spec=grid_spec,
        out_shape=jax.ShapeDtypeStruct((MOE_ROWS, D_EXPERT), BF16),
        compiler_params=_cparams(("arbitrary",)),
        name="moe_gate_up",
    )(*plan, xs, w_gate_up, b_gate_up, b_gate_up)


def _down_kernel(t0_ref, nt_ref, cpt_ref, nv_ref, hid_hbm, w_hbm, b_ref, ys_hbm,
                 stage, w_s, hbuf, obuf, sem_w, sem_in, sem_out):
    e = pl.program_id(0)
    rows = D_EXPERT // W_CHUNKS

    def chunk_copy(step, c):
        r = pl.ds(pl.multiple_of(c * rows, rows), rows)
        return pltpu.make_async_copy(w_hbm.at[step, r], stage.at[r], sem_w)

    def in_copy(t, slot):
        return pltpu.make_async_copy(hid_hbm.at[_tile_rows(t)], hbuf.at[slot], sem_in.at[slot])

    def out_copy(t, slot):
        return pltpu.make_async_copy(obuf.at[slot], ys_hbm.at[_tile_rows(t)], sem_out.at[slot])

    def compute(slot):
        obuf[slot] = _dot(hbuf[slot], w_s[...]) + b_ref[0]

    def tile_loop(after_tile):
        _expert_tile_loop(t0_ref[e], nt_ref[e], in_copy, out_copy, compute, after_tile)
        return nt_ref[e]

    _stream_weights(e, N_EXPERTS, cpt_ref[e], chunk_copy, stage, w_s, tile_loop)
    _zero_tail_tiles(e == N_EXPERTS - 1, nv_ref[0], obuf, out_copy)


def _moe_down(hid, plan, w_down, b_down):
    grid_spec = pltpu.PrefetchScalarGridSpec(
        num_scalar_prefetch=4,
        grid=(N_EXPERTS,),
        in_specs=[pl.BlockSpec(memory_space=pl.ANY), pl.BlockSpec(memory_space=pl.ANY),
                  pl.BlockSpec((1, 1, D_MODEL), lambda e, *_: (e, 0, 0))],
        out_specs=pl.BlockSpec(memory_space=pl.ANY),
        scratch_shapes=[pltpu.VMEM((D_EXPERT, D_MODEL), F32), pltpu.VMEM((D_EXPERT, D_MODEL), BF16),
                        pltpu.VMEM((2, MOE_TM, D_EXPERT), BF16), pltpu.VMEM((2, MOE_TM, D_MODEL), F32),
                        pltpu.SemaphoreType.DMA(()), pltpu.SemaphoreType.DMA((2,)),
                        pltpu.SemaphoreType.DMA((2,))],
    )
    return pl.pallas_call(
        _down_kernel,
        grid_spec=grid_spec,
        out_shape=jax.ShapeDtypeStruct((MOE_ROWS, D_MODEL), F32),
        compiler_params=_cparams(("arbitrary",)),
        name="moe_down",
    )(*plan, hid, w_down, b_down)


COMBINE_TOK = 64


def _start_rows(src_hbm, idx_ref, idx_lead, dst, sem, count):
    for r in range(count):
        pltpu.make_async_copy(src_hbm.at[pl.ds(idx_ref[idx_lead + (r,)], 1)], dst.at[pl.ds(r, 1)],
                              sem).start(priority=r % 2)


def _wait_rows(src_hbm, dst, sem, count):
    for _ in range(count):
        pltpu.make_async_copy(src_hbm.at[pl.ds(0, 1)], dst.at[pl.ds(0, 1)], sem).wait()


def _combine_kernel(pos_ref, pos_next_ref, ys_hbm, wgt_ref, x1_ref, m_ref, g_ref, b_ref, o_ref, buf, sem,
                    *, n_steps):
    i = pl.program_id(0)
    slot = i % 2

    def start(p_ref, s):
        for k in range(TOP_K):
            _start_rows(ys_hbm, p_ref, (0, k), buf.at[s, k], sem.at[s], COMBINE_TOK)

    @pl.when(i == 0)
    def _():
        start(pos_ref, 0)

    @pl.when(i + 1 < n_steps)
    def _():
        start(pos_next_ref, 1 - slot)

    _wait_rows(ys_hbm, buf.at[slot, 0], sem.at[slot], TOP_K * COMBINE_TOK)
    w = wgt_ref[...]
    ffn = w[:, 0:1] * buf[slot, 0]
    for k in range(1, TOP_K):
        ffn = ffn + w[:, k:k + 1] * buf[slot, k]
    x2 = _layer_norm(DN_ALPHA * x1_ref[...] + m_ref[0, 5:6, :] * ffn) * g_ref[...] + b_ref[...]
    o_ref[...] = x2


def _combine(ys, pos_blocks, wgt, x1, mods, ln_g, ln_b, *, rows, row0):
    tm = COMBINE_TOK
    b0 = row0 // tm
    n = rows // tm
    const = lambda i: (0, 0)
    pos_spec = lambda f: pl.BlockSpec((1, TOP_K, tm), f, memory_space=pltpu.SMEM)
    return pl.pallas_call(
        functools.partial(_combine_kernel, n_steps=n),
        grid=(n,),
        in_specs=[pos_spec(lambda i: (i + b0, 0, 0)),
                  pos_spec(lambda i: (jnp.minimum(i + 1, n - 1) + b0, 0, 0)),
                  pl.BlockSpec(memory_space=pl.ANY),
                  pl.BlockSpec((tm, TOP_K), lambda i: (i + b0, 0)),
                  pl.BlockSpec((tm, D_MODEL), lambda i: (i + b0, 0)),
                  pl.BlockSpec((1, 6, D_MODEL), lambda i: (_mod_row(i + b0, tm), 0, 0)),
                  pl.BlockSpec((1, D_MODEL), const), pl.BlockSpec((1, D_MODEL), const)],
        out_specs=pl.BlockSpec((tm, D_MODEL), lambda i: (i, 0)),
        out_shape=jax.ShapeDtypeStruct((rows, D_MODEL), F32),
        scratch_shapes=[pltpu.VMEM((2, TOP_K, tm, D_MODEL), F32), pltpu.SemaphoreType.DMA((2,))],
        compiler_params=_cparams(("arbitrary",)),
        name="moe_combine_ln2",
    )(pos_blocks, pos_blocks, ys, wgt, x1, mods, ln_g, ln_b)


def _moe_plan(idx_t):
    flat = idx_t.reshape(-1)
    onehot = (flat[:, None] == jnp.arange(N_EXPERTS)[None, :]).astype(jnp.int32)
    rank = jnp.take_along_axis(jnp.cumsum(onehot, axis=0), flat[:, None], axis=1)[:, 0] - 1
    counts = jnp.sum(onehot, axis=0)
    tiles = (counts + MOE_TM - 1) // MOE_TM
    tile_end = jnp.cumsum(tiles)
    tile_start = tile_end - tiles
    pos = tile_start[flat] * MOE_TM + rank
    n_valid = tile_end[-1]
    per_tile = lambda n_chunks: (n_chunks + jnp.maximum(tiles, 1) - 1) // jnp.maximum(tiles, 1)
    i32 = lambda a: a.astype(jnp.int32)
    plan = lambda rate: (i32(tile_start), i32(tiles), i32(rate), i32(n_valid.reshape(1)))
    return (i32(pos.reshape(TOP_K, T_ALL)), plan(jnp.minimum(per_tile(GU_CHUNKS), GU_RING)),
            plan(per_tile(W_CHUNKS)))


def kernel(x_prompt, x_sample, cache_na_k, cache_na_v, state_ssd_fwd, state_ssd_bwd, c, c_ctx, w_mod, b_mod, w_in, ssd_conv_w, ssd_conv_b, ssd_dt_bias, ssd_a_log, ssd_d, ssd_norm_w, na_rpb, w_ssd_out, w_na_out, w_o, ln1_g, ln1_b, ln2_g, ln2_b, w_router, b_router, w_gate_up, b_gate_up, w_down, b_down):
    assert w_mod.shape[0] == 1, "single-layer trunk"
    x_ctx = x_prompt.reshape(T_CTX, D_MODEL)
    x_dec = x_sample.reshape(T_DEC, D_MODEL)

    cvec = jnp.concatenate([c_ctx[None], c, jnp.zeros((8 - 1 - DEC_BATCH, D_MODEL), F32)], axis=0)
    mods = _modulation(cvec, w_mod[0], b_mod[0])[:1 + DEC_BATCH].reshape(1 + DEC_BATCH, 6, D_MODEL)

    h = _ln_modulate(x_ctx, x_dec, mods)
    w = w_in[0]
    n_main = 3 * NA_WIDTH + SSD_WIDTH + SSD_CONV_CH
    w_bf = w.astype(BF16)
    w_dt = jnp.pad(w[:, n_main:n_main + 2 * SSD_HEADS], ((0, 0), (0, 128 - 2 * SSD_HEADS))).astype(BF16)
    w_gates = w_bf[:, n_main + 2 * SSD_HEADS:]
    tm = 1024
    ctx_blocks = T_CTX // tm
    q_all = _matmul(h, w_bf, rows=T_ALL, col_block0=0, n_out=NA_WIDTH, out_dtype=BF16,
                    name="proj_q")
    k_ctx = _matmul(h, w_bf, rows=T_CTX, col_block0=2, n_out=NA_WIDTH, name="proj_k_ctx")
    k_dec = _matmul(h, w_bf, rows=T_DEC, row_block0=ctx_blocks, col_block0=2, n_out=NA_WIDTH, out_dtype=BF16,
                    name="proj_k_dec")
    v_ctx = _matmul(h, w_bf, rows=T_CTX, col_block0=4, n_out=NA_WIDTH, name="proj_v_ctx")
    v_dec = _matmul(h, w_bf, rows=T_DEC, row_block0=ctx_blocks, col_block0=4, n_out=NA_WIDTH, out_dtype=BF16,
                    name="proj_v_dec")
    zx = _matmul(h, w_bf, rows=T_ALL, col_block0=6, n_out=SSD_WIDTH + SSD_CONV_CH, name="proj_zxbc")
    gates = _matmul(h, w_gates, rows=T_ALL, name="proj_gates")
    dt_all = _matmul(h, w_dt, rows=T_ALL, tn=128, name="proj_dt")

    na_ctx = _context_attention(q_all, k_ctx, v_ctx)
    kc = cache_na_k[:, 0].reshape(DEC_BATCH, PAST_LEN, NA_WIDTH)
    vc = cache_na_v[:, 0].reshape(DEC_BATCH, PAST_LEN, NA_WIDTH)
    na_dec = _neighbourhood_attention(q_all, k_dec, v_dec, kc, vc, _rpb_table(na_rpb[0]))

    consts = _ssd_consts(ssd_conv_w[0], ssd_conv_b[0], ssd_dt_bias[0], ssd_a_log[0], ssd_d[0])
    dt_t = dt_all[:, :2 * SSD_HEADS].reshape(T_ALL // SSD_CHUNK, SSD_CHUNK, 2, SSD_GROUPS, HEADS_PER_GROUP)
    dt_t = jnp.transpose(dt_t, (0, 2, 3, 4, 1))
    u_ctx, h_f, h_b = _ssd_mixer(zx, dt_all, dt_t, consts, length=SEQ, n_seq=BATCH, seq_block0=0,
                                 emit_state=True)
    h0 = (state_ssd_fwd[:, 0].reshape(DEC_BATCH, SSD_GROUPS, HEADS_PER_GROUP, SSD_HEAD_DIM, SSD_STATE),
          state_ssd_bwd[:, 0].reshape(DEC_BATCH, SSD_GROUPS, HEADS_PER_GROUP, SSD_HEAD_DIM, SSD_STATE))
    (u_dec,) = _ssd_mixer(zx, dt_all, dt_t, consts, length=DEC_SEQ, n_seq=DEC_BATCH,
                          seq_block0=T_CTX // DEC_SEQ, h0=h0)

    y = _merge_branches(na_ctx, na_dec, u_ctx, u_dec, ssd_norm_w[0].reshape(1, SSD_WIDTH),
                        w_na_out[0].astype(BF16), w_ssd_out[0].astype(BF16), gates)
    x1, h2, idx_t, wgt_t = _post_mix(y, w_o[0].astype(BF16), x_ctx, x_dec, mods,
                                     ln1_g[0].reshape(1, D_MODEL), ln1_b[0].reshape(1, D_MODEL),
                                     jnp.transpose(w_router[0]), b_router[0].reshape(N_EXPERTS, 1))

    pos, plan_gu, plan_down = _moe_plan(idx_t)
    tile_start, tiles, _, n_valid = plan_gu
    pos_d = jnp.transpose(pos.reshape(TOP_K, T_ALL // DISPATCH_TOK, DISPATCH_TOK), (1, 0, 2))
    xs = _dispatch(h2, pos_d, tile_start, tiles, n_valid)
    hid = _moe_gate_up(xs, plan_gu, w_gate_up[0], b_gate_up[0].reshape(N_EXPERTS, 1, 2 * D_EXPERT))
    ys = _moe_down(hid, plan_down, w_down[0], b_down[0].reshape(N_EXPERTS, 1, D_MODEL))
    pos_c = jnp.transpose(pos.reshape(TOP_K, T_ALL // COMBINE_TOK, COMBINE_TOK), (1, 0, 2))
    wgt = jnp.transpose(wgt_t)
    g2, b2 = ln2_g[0].reshape(1, D_MODEL), ln2_b[0].reshape(1, D_MODEL)
    y_ctx = _combine(ys, pos_c, wgt, x1, mods, g2, b2, rows=T_CTX, row0=0)
    y_dec = _combine(ys, pos_c, wgt, x1, mods, g2, b2, rows=T_DEC, row0=T_CTX)

    return (y_ctx.reshape(BATCH, SEQ, D_MODEL),
            y_dec.reshape(DEC_BATCH, DEC_SEQ, D_MODEL),
            k_ctx.reshape(BATCH, 1, SEQ, NA_HEADS, NA_HEAD_DIM),
            v_ctx.reshape(BATCH, 1, SEQ, NA_HEADS, NA_HEAD_DIM),
            h_f.reshape(BATCH, 1, SSD_HEADS, SSD_HEAD_DIM, SSD_STATE),
            h_b.reshape(BATCH, 1, SSD_HEADS, SSD_HEAD_DIM, SSD_STATE))
```
